```python
import jax, jax.numpy as jnp
from jax import lax
import numpy as np

D_MODEL = 2048
BATCH = 8
SEQ = 4096
DEPTH = 1
DEC_BATCH = 8
DEC_SEQ = 64
PAST_LEN = 1024

CHUNK = 64
HEAD_DIM = 128
N_FOX_HEADS = 8
N_SB_HEADS = 8
FOX_WIDTH = N_FOX_HEADS * HEAD_DIM
SB_WIDTH = N_SB_HEADS * HEAD_DIM
MIX_WIDTH = FOX_WIDTH + SB_WIDTH
IN_COLS = 3 * FOX_WIDTH + N_FOX_HEADS + 3 * SB_WIDTH
Q_BLOCK = 128
N_GROUPS = 4
EXPERTS_PER_GROUP = 8
N_EXPERTS = N_GROUPS * EXPERTS_PER_GROUP
TOP_K = 2
D_EXPERT = 1024
MOE_BLOCK = 128
FORGET_BIAS_INIT = 3.0
EPS = 1e-6

kernel_name = "fox_stickbreak_hier_moe_stream_step"


def _rmsnorm(x, g):
    xf = x.astype(jnp.float32)
    y = xf * lax.rsqrt(jnp.mean(xf * xf, axis=-1, keepdims=True) + EPS)
    return (y * g.astype(jnp.float32)).astype(x.dtype)


def _project(hn, w_in, b_f):
    B, T, _ = hn.shape
    p = hn @ w_in
    splits = [FOX_WIDTH, 2 * FOX_WIDTH, 3 * FOX_WIDTH, 3 * FOX_WIDTH + N_FOX_HEADS,
              3 * FOX_WIDTH + N_FOX_HEADS + SB_WIDTH, 3 * FOX_WIDTH + N_FOX_HEADS + 2 * SB_WIDTH]
    qf, kf, vf, fl, qs, ks, vs = jnp.split(p, splits, axis=-1)
    hf = lambda a: a.reshape(B, T, N_FOX_HEADS, HEAD_DIM)
    hs = lambda a: a.reshape(B, T, N_SB_HEADS, HEAD_DIM)
    logf = jax.nn.log_sigmoid(fl.astype(jnp.float32) + b_f.astype(jnp.float32))
    return hf(qf), hf(kf), hf(vf), logf, hs(qs), hs(ks), hs(vs)


def _fox_block(q, cq, qpos, k, v, ck, kpos):
    s = jnp.einsum('bqhd,bkhd->bhqk', q, k).astype(jnp.float32) * (HEAD_DIM ** -0.5)
    s = s + jnp.swapaxes(cq, 1, 2)[:, :, :, None] - jnp.swapaxes(ck, 1, 2)[:, :, None, :]
    mask = kpos[None, :] <= qpos[:, None]
    p = jax.nn.softmax(jnp.where(mask, s, -jnp.inf), axis=-1)
    return jnp.einsum('bhqk,bkhd->bqhd', p.astype(v.dtype), v)


def _sb_block(q, qpos, k, v, kpos):
    z = jnp.einsum('bqhd,bkhd->bhqk', q, k).astype(jnp.float32) * (HEAD_DIM ** -0.5)
    valid = kpos[None, :] < qpos[:, None]
    l = jnp.where(valid, jax.nn.log_sigmoid(-z), 0.0)
    suffix = lax.cumsum(l, axis=3, reverse=True) - l
    a = jnp.where(valid, jnp.exp(jax.nn.log_sigmoid(z) + suffix), 0.0)
    return jnp.einsum('bhqk,bkhd->bqhd', a.astype(v.dtype), v)


def _to_blocks(a):
    B, S = a.shape[:2]
    return jnp.swapaxes(a.reshape((B, S // Q_BLOCK, Q_BLOCK) + a.shape[2:]), 0, 1)


def _from_blocks(a):
    nb, B, qb = a.shape[:3]
    return jnp.swapaxes(a, 0, 1).reshape((B, nb * qb) + a.shape[3:])


def _prompt_attention(qf, kf, vf, cf, qs, ks, vs):
    S = qf.shape[1]
    pos = jnp.arange(S, dtype=jnp.int32)
    pos_b = pos.reshape(S // Q_BLOCK, Q_BLOCK)
    o_f = lax.map(lambda a: _fox_block(a[0], a[1], a[2], kf, vf, cf, pos),
                  (_to_blocks(qf), _to_blocks(cf), pos_b))
    o_s = lax.map(lambda a: _sb_block(a[0], a[1], ks, vs, pos), (_to_blocks(qs), pos_b))
    return _from_blocks(o_f), _from_blocks(o_s)


def _merge(o_f, o_s, g_of, g_os, w_out):
    B, T = o_f.shape[:2]
    nf = _rmsnorm(o_f, g_of.reshape(N_FOX_HEADS, HEAD_DIM)).reshape(B, T, FOX_WIDTH)
    ns = _rmsnorm(o_s, g_os.reshape(N_SB_HEADS, HEAD_DIM)).reshape(B, T, SB_WIDTH)
    return jnp.concatenate([nf, ns], axis=-1) @ w_out


def _hier_moe(x, w_group, b_group, w_router, b_router, w1, w3, w2):
    B, T, D = x.shape
    h = x.reshape(B * T, D)
    N = B * T
    g_logits = (h @ w_group).astype(jnp.float32) + b_group.astype(jnp.float32)
    g_sel = jnp.argmax(g_logits, axis=-1)
    p_g = jnp.take_along_axis(jax.nn.softmax(g_logits, axis=-1), g_sel[:, None], axis=1)[:, 0]
    e_logits = ((h @ w_router).astype(jnp.float32) + b_router.astype(jnp.float32)).reshape(N, N_GROUPS, EXPERTS_PER_GROUP)
    in_grp = jnp.take_along_axis(e_logits, g_sel[:, None, None], axis=1)[:, 0]
    top_p, top_i = lax.top_k(jax.nn.softmax(in_grp, axis=-1), TOP_K)
    gates = p_g[:, None] * top_p / jnp.sum(top_p, axis=-1, keepdims=True)
    expert = g_sel[:, None].astype(jnp.int32) * EXPERTS_PER_GROUP + top_i.astype(jnp.int32)
    M = N * TOP_K
    flat_e = expert.reshape(M)
    flat_tok = jnp.repeat(jnp.arange(N, dtype=jnp.int32), TOP_K)
    flat_g = gates.reshape(M)
    order = jnp.argsort(flat_e)
    se, stok, sg = flat_e[order], flat_tok[order], flat_g[order]
    counts = jnp.bincount(flat_e, length=N_EXPERTS).astype(jnp.int32)
    start = jnp.cumsum(counts) - counts
    padded = (counts + MOE_BLOCK - 1) // MOE_BLOCK * MOE_BLOCK
    pend = jnp.cumsum(padded)
    pstart = pend - padded
    dest = pstart[se] + (jnp.arange(M, dtype=jnp.int32) - start[se])
    n_blocks = -(-(M + N_EXPERTS * (MOE_BLOCK - 1)) // MOE_BLOCK)
    P = n_blocks * MOE_BLOCK
    row_tok = jnp.zeros((P,), jnp.int32).at[dest].set(stok)
    row_gate = jnp.zeros((P,), h.dtype).at[dest].set(sg.astype(h.dtype))
    block_start = jnp.arange(n_blocks, dtype=jnp.int32) * MOE_BLOCK
    block_expert = jnp.minimum(jnp.searchsorted(pend, block_start, side='right'), N_EXPERTS - 1)
    xs = h[row_tok].reshape(n_blocks, MOE_BLOCK, D)

    def _expert(args):
        xb, e = args
        return (jax.nn.silu(xb @ w1[e]) * (xb @ w3[e])) @ w2[e]

    out = lax.map(_expert, (xs, block_expert)).reshape(P, D)
    y = jnp.zeros((N, D), h.dtype).at[row_tok].add(out * row_gate[:, None])
    return y.reshape(B, T, D)


def setup_inputs(seed: int = 0) -> dict:
    key = jax.random.key(seed)
    k = jax.random.split(key, 24)
    nrm = lambda kk, shape, scale: scale * jax.random.normal(kk, shape, jnp.float32)
    fox_shape = (DEPTH, DEC_BATCH, PAST_LEN, N_FOX_HEADS, HEAD_DIM)
    sb_shape = (DEPTH, DEC_BATCH, PAST_LEN, N_SB_HEADS, HEAD_DIM)
    return {
        "x_prompt": nrm(k[0], (BATCH, SEQ, D_MODEL), 1.0),
        "x_sample": nrm(k[1], (DEC_BATCH, DEC_SEQ, D_MODEL), 1.0),
        "cache_fox_k": nrm(k[2], fox_shape, 1.0),
        "cache_fox_v": nrm(k[3], fox_shape, 1.0),
        "cache_fox_logf": jax.nn.log_sigmoid(FORGET_BIAS_INIT + nrm(k[4], (DEPTH, DEC_BATCH, PAST_LEN, N_FOX_HEADS), 1.0)),
        "cache_sb_k": nrm(k[5], sb_shape, 1.0),
        "cache_sb_v": nrm(k[6], sb_shape, 1.0),
        "w_in": nrm(k[7], (DEPTH, D_MODEL, IN_COLS), D_MODEL ** -0.5),
        "b_f": FORGET_BIAS_INIT + nrm(k[8], (DEPTH, N_FOX_HEADS), 0.5),
        "g_attn": 1.0 + nrm(k[9], (DEPTH, D_MODEL), 0.1),
        "g_out_fox": 1.0 + nrm(k[10], (DEPTH, FOX_WIDTH), 0.1),
        "g_out_sb": 1.0 + nrm(k[11], (DEPTH, SB_WIDTH), 0.1),
        "w_out": nrm(k[12], (DEPTH, MIX_WIDTH, D_MODEL), MIX_WIDTH ** -0.5),
        "g_ffn": 1.0 + nrm(k[13], (DEPTH, D_MODEL), 0.1),
        "w_group": nrm(k[14], (DEPTH, D_MODEL, N_GROUPS), D_MODEL ** -0.5),
        "b_group": nrm(k[15], (DEPTH, N_GROUPS), 0.01),
        "w_router": nrm(k[16], (DEPTH, D_MODEL, N_EXPERTS), D_MODEL ** -0.5),
        "b_router": nrm(k[17], (DEPTH, N_EXPERTS), 0.01),
        "w1": nrm(k[18], (DEPTH, N_EXPERTS, D_MODEL, D_EXPERT), D_MODEL ** -0.5),
        "w3": nrm(k[19], (DEPTH, N_EXPERTS, D_MODEL, D_EXPERT), D_MODEL ** -0.5),
        "w2": nrm(k[20], (DEPTH, N_EXPERTS, D_EXPERT, D_MODEL), D_EXPERT ** -0.5),
        "g_final": 1.0 + nrm(k[21], (D_MODEL,), 0.1),
    }


def reference(x_prompt, x_sample, cache_fox_k, cache_fox_v, cache_fox_logf, cache_sb_k, cache_sb_v,
              w_in, b_f, g_attn, g_out_fox, g_out_sb, w_out, g_ffn, w_group, b_group, w_router, b_router,
              w1, w3, w2, g_final):
    hp, hs = x_prompt, x_sample
    past = cache_fox_logf.shape[2]
    T = hs.shape[1]
    qpos_s = past + jnp.arange(T, dtype=jnp.int32)
    kpos_s = jnp.arange(past + T, dtype=jnp.int32)
    p_fk, p_fv, p_fl, p_sk, p_sv = [], [], [], [], []
    s_fk, s_fv, s_fl, s_sk, s_sv = [], [], [], [], []
    for l in range(DEPTH):
        qf, kf, vf, logf, qs, ks, vs = _project(_rmsnorm(hp, g_attn[l]), w_in[l], b_f[l])
        cf = jnp.cumsum(logf, axis=1)
        o_f, o_s = _prompt_attention(qf, kf, vf, cf, qs, ks, vs)
        hp = hp + _merge(o_f, o_s, g_out_fox[l], g_out_sb[l], w_out[l])
        hp = hp + _hier_moe(_rmsnorm(hp, g_ffn[l]), w_group[l], b_group[l], w_router[l], b_router[l], w1[l], w3[l], w2[l])
        p_fk.append(kf); p_fv.append(vf); p_fl.append(logf); p_sk.append(ks); p_sv.append(vs)
        qf2, kf2, vf2, logf2, qs2, ks2, vs2 = _project(_rmsnorm(hs, g_attn[l]), w_in[l], b_f[l])
        Kf = jnp.concatenate([cache_fox_k[l].astype(kf2.dtype), kf2], axis=1)
        Vf = jnp.concatenate([cache_fox_v[l].astype(vf2.dtype), vf2], axis=1)
        cf_all = jnp.cumsum(jnp.concatenate([cache_fox_logf[l].astype(jnp.float32), logf2], axis=1), axis=1)
        Ks = jnp.concatenate([cache_sb_k[l].astype(ks2.dtype), ks2], axis=1)
        Vs = jnp.concatenate([cache_sb_v[l].astype(vs2.dtype), vs2], axis=1)
        o_f2 = _fox_block(qf2, cf_all[:, past:], qpos_s, Kf, Vf, cf_all, kpos_s)
        o_s2 = _sb_block(qs2, qpos_s, Ks, Vs, kpos_s)
        hs = hs + _merge(o_f2, o_s2, g_out_fox[l], g_out_sb[l], w_out[l])
        hs = hs + _hier_moe(_rmsnorm(hs, g_ffn[l]), w_group[l], b_group[l], w_router[l], b_router[l], w1[l], w3[l], w2[l])
        s_fk.append(kf2); s_fv.append(vf2); s_fl.append(logf2); s_sk.append(ks2); s_sv.append(vs2)
    y_prompt = _rmsnorm(hp, g_final)
    y_sample = _rmsnorm(hs, g_final)
    return (y_prompt, y_sample,
            jnp.stack(p_fk), jnp.stack(p_fv), jnp.stack(p_fl), jnp.stack(p_sk), jnp.stack(p_sv),
            jnp.stack(s_fk), jnp.stack(s_fv), jnp.stack(s_fl), jnp.stack(s_sk), jnp.stack(s_sv))
```

```python
import functools

import jax
import jax.numpy as jnp
from jax import lax
from jax.experimental import pallas as pl
from jax.experimental.pallas import tpu as pltpu

F32 = jnp.float32
BF16 = jnp.bfloat16

HEAD_DIM = 128
TOP_K = 2
EPS = 1e-6
LANES = 128
NEG_BIG = -1e30
VMEM_LIMIT = 56 * 1024 * 1024

ROW_TILE = 256
EXPERT_BLOCK = 512
COMBINE_TILE = 128


def _params(*sem):
    return pltpu.CompilerParams(dimension_semantics=sem, vmem_limit_bytes=VMEM_LIMIT)


def _log_sigmoid(x):
    return jnp.minimum(x, 0.0) - jnp.log1p(jnp.exp(-jnp.abs(x)))


def _softplus(x):
    return jnp.maximum(x, 0.0) + jnp.log1p(jnp.exp(-jnp.abs(x)))


def _rms(x, g):
    return x * lax.rsqrt(jnp.mean(x * x, axis=-1, keepdims=True) + EPS) * g


def _in_proj_kernel(x_ref, g_ref, wm_ref, wf_ref, bf_ref,
                    qf_ref, kf_ref, vf_ref, lf_ref, qs_ref, ks_ref, vs_ref, *, width, n_heads):
    hn = _rms(x_ref[...], g_ref[...]).astype(BF16)
    for c, o_ref in enumerate((qf_ref, kf_ref, vf_ref, qs_ref, ks_ref, vs_ref)):
        o_ref[...] = jnp.dot(hn, wm_ref[:, c * width:(c + 1) * width],
                             preferred_element_type=F32).astype(o_ref.dtype)
    fl = jnp.dot(hn, wf_ref[...], preferred_element_type=F32) + bf_ref[...]
    lf_ref[...] = _log_sigmoid(fl)[:, :n_heads]


def _in_proj(x, g, wm, wf, bf, n_heads):
    n, d = x.shape
    width = wm.shape[1] // 6
    tm = ROW_TILE
    row = lambda w: pl.BlockSpec((tm, w), lambda i: (i, 0))
    const = lambda a: pl.BlockSpec(a.shape, lambda i: (0, 0), pipeline_mode=pl.Buffered(1))
    wide = lambda dt: jax.ShapeDtypeStruct((n, width), dt)
    return pl.pallas_call(
        functools.partial(_in_proj_kernel, width=width, n_heads=n_heads),
        grid=(n // tm,),
        in_specs=[row(d), const(g), const(wm), const(wf), const(bf)],
        out_specs=[row(width), row(width), row(width), row(n_heads), row(width), row(width), row(width)],
        out_shape=[wide(BF16), wide(F32), wide(F32), jax.ShapeDtypeStruct((n, n_heads), F32),
                   wide(BF16), wide(F32), wide(F32)],
        compiler_params=_params("arbitrary"),
        name="in_proj",
    )(x, g, wm, wf, bf)


def _cumsum_kernel(lf_ref, cf_ref, carry_ref):
    tt = lf_ref.shape[0]

    @pl.when(pl.program_id(1) == 0)
    def _():
        carry_ref[...] = jnp.zeros_like(carry_ref)

    r = lax.broadcasted_iota(jnp.int32, (tt, tt), 0)
    c = lax.broadcasted_iota(jnp.int32, (tt, tt), 1)
    tri = jnp.where(c <= r, 1.0, 0.0).astype(F32)
    cf = jnp.dot(tri, lf_ref[...], precision=lax.Precision.HIGHEST,
                 preferred_element_type=F32) + carry_ref[...]
    cf_ref[...] = cf
    carry_ref[...] = cf[tt - 1:tt, :]


def _cumsum(lf, tt):
    b, t, h = lf.shape
    return pl.pallas_call(
        _cumsum_kernel,
        grid=(b, t // tt),
        in_specs=[pl.BlockSpec((None, tt, h), lambda i, j: (i, j, 0))],
        out_specs=pl.BlockSpec((None, tt, h), lambda i, j: (i, j, 0)),
        out_shape=jax.ShapeDtypeStruct((b, t, h), F32),
        scratch_shapes=[pltpu.VMEM((1, h), F32)],
        compiler_params=_params("arbitrary", "arbitrary"),
        name="cumsum",
    )(lf)


def _head_norm_store(o, g_ref, o_ref):
    o_ref[...] = _rms(o, g_ref[...]).astype(o_ref.dtype)


def _fox_kernel(q_ref, k_ref, v_ref, cq_ref, ck_ref, g_ref, o_ref, *, tq, tk, q_off):
    h = pl.program_id(1)
    qi = pl.program_id(2)
    q = q_ref[...]
    lane = lax.broadcasted_iota(jnp.int32, cq_ref.shape, 1)
    cq = jnp.sum(jnp.where(lane == h, cq_ref[...], 0.0), axis=1, keepdims=True)
    q_lo = q_off + qi * tq
    qpos = q_lo + lax.broadcasted_iota(jnp.int32, (tq, tk), 0)
    kiota = lax.broadcasted_iota(jnp.int32, (tq, tk), 1)
    scale = HEAD_DIM ** -0.5

    def chunk(j, carry, masked):
        m, l, acc = carry
        k0 = pl.multiple_of(j * tk, tk)
        kc = k_ref[pl.ds(k0, tk), :].astype(BF16)
        vc = v_ref[pl.ds(k0, tk), :].astype(BF16)
        s = lax.dot_general(q, kc, (((1,), (1,)), ((), ())), preferred_element_type=F32) * scale
        s = s + cq - ck_ref[:, pl.ds(k0, tk)]
        if masked:
            s = jnp.where(k0 + kiota <= qpos, s, NEG_BIG)
        m_new = jnp.maximum(m, jnp.max(s, axis=1, keepdims=True))
        p = jnp.exp(s - m_new)
        alpha = jnp.exp(m - m_new)
        l = alpha * l + jnp.sum(p, axis=1, keepdims=True)
        acc = alpha * acc + jnp.dot(p.astype(BF16), vc, preferred_element_type=F32)
        return m_new, l, acc

    n_full = (q_lo + 1) // tk
    n_all = (q_lo + tq + tk - 1) // tk
    init = (jnp.full((tq, 1), NEG_BIG, F32), jnp.zeros((tq, 1), F32), jnp.zeros((tq, HEAD_DIM), F32))
    carry = lax.fori_loop(0, n_full, lambda j, c: chunk(j, c, False), init)
    m, l, acc = lax.fori_loop(n_full, n_all, lambda j, c: chunk(j, c, True), carry)
    _head_norm_store(acc / l, g_ref, o_ref)


def _sb_kernel(q_ref, k_ref, v_ref, g_ref, o_ref, *, tq, tk, q_off):
    qi = pl.program_id(2)
    q = q_ref[...]
    q_lo = q_off + qi * tq
    qpos = q_lo + lax.broadcasted_iota(jnp.int32, (tq, tk), 0)
    kiota = lax.broadcasted_iota(jnp.int32, (tq, tk), 1)
    r = lax.broadcasted_iota(jnp.int32, (tk, tk), 0)
    c = lax.broadcasted_iota(jnp.int32, (tk, tk), 1)
    later = jnp.where(r > c, 1.0, 0.0).astype(BF16)
    scale = HEAD_DIM ** -0.5

    def chunk(j, carry, masked):
        tail, acc = carry
        k0 = pl.multiple_of(j * tk, tk)
        kc = k_ref[pl.ds(k0, tk), :].astype(BF16)
        vc = v_ref[pl.ds(k0, tk), :].astype(BF16)
        z = lax.dot_general(q, kc, (((1,), (1,)), ((), ())), preferred_element_type=F32) * scale
        lneg = -_softplus(z)
        if masked:
            valid = k0 + kiota < qpos
            lneg = jnp.where(valid, lneg, 0.0)
        hi = lneg.astype(BF16)
        lo = (lneg - hi.astype(F32)).astype(BF16)
        suffix = (jnp.dot(hi, later, preferred_element_type=F32)
                  + jnp.dot(lo, later, preferred_element_type=F32))
        a = jnp.exp((z + lneg) + (suffix + tail))
        if masked:
            a = jnp.where(valid, a, 0.0)
        acc = acc + jnp.dot(a.astype(BF16), vc, preferred_element_type=F32)
        tail = tail + jnp.sum(lneg, axis=1, keepdims=True)
        return tail, acc

    n_full = q_lo // tk
    n_all = (q_lo + tq - 1 + tk - 1) // tk
    init = (jnp.zeros((tq, 1), F32), jnp.zeros((tq, HEAD_DIM), F32))
    carry = lax.fori_loop(0, n_all - n_full, lambda i, c: chunk(n_all - 1 - i, c, True), init)
    _, acc = lax.fori_loop(0, n_full, lambda i, c: chunk(n_full - 1 - i, c, False), carry)
    _head_norm_store(acc, g_ref, o_ref)


def _attention(kind, q, k, v, g, cq, ck, *, tq, tk, q_off):
    b, sq, width = q.shape
    sk = k.shape[1]
    n_heads = width // HEAD_DIM
    q_spec = pl.BlockSpec((None, tq, HEAD_DIM), lambda i, h, j: (i, j, h))
    kv_spec = pl.BlockSpec((None, sk, HEAD_DIM), lambda i, h, j: (i, 0, h))
    g_spec = pl.BlockSpec((1, HEAD_DIM), lambda i, h, j: (0, h))
    if kind == "fox":
        body = functools.partial(_fox_kernel, tq=tq, tk=tk, q_off=q_off)
        in_specs = [q_spec, kv_spec, kv_spec,
                    pl.BlockSpec((None, tq, n_heads), lambda i, h, j: (i, j, 0)),
                    pl.BlockSpec((None, None, 1, sk), lambda i, h, j: (i, h, 0, 0)), g_spec]
        args = (q, k, v, cq, ck[:, :, None, :], g)
    else:
        body = functools.partial(_sb_kernel, tq=tq, tk=tk, q_off=q_off)
        in_specs = [q_spec, kv_spec, kv_spec, g_spec]
        args = (q, k, v, g)
    return pl.pallas_call(
        body,
        grid=(b, n_heads, sq // tq),
        in_specs=in_specs,
        out_specs=q_spec,
        out_shape=jax.ShapeDtypeStruct((b, sq, width), BF16),
        compiler_params=_params("arbitrary", "arbitrary", "arbitrary"),
        name=kind,
    )(*args)


def _merge_kernel(nf_ref, ns_ref, x_ref, wo_ref, g_ref, wr_ref, br_ref, *rest):
    h_ref, hn_ref, lg_ref = rest[-3:]
    width = nf_ref.shape[1]
    h = x_ref[...] + (jnp.dot(nf_ref[...], wo_ref[:width, :], preferred_element_type=F32)
                      + jnp.dot(ns_ref[...], wo_ref[width:, :], preferred_element_type=F32))
    h_ref[...] = h
    hn = _rms(h, g_ref[...])
    hn_ref[...] = hn
    lg_ref[...] = jnp.dot(hn.astype(BF16), wr_ref[...], preferred_element_type=F32) + br_ref[...]


def _merge(nf, ns, x, wo, g, wr, br, n_total, row_off, prev):
    n, d = x.shape
    width = nf.shape[1]
    tm = ROW_TILE
    off = row_off // tm
    row = lambda w: pl.BlockSpec((tm, w), lambda i: (i, 0))
    out_row = lambda w: pl.BlockSpec((tm, w), lambda i: (i + off, 0))
    const = lambda a: pl.BlockSpec(a.shape, lambda i: (0, 0), pipeline_mode=pl.Buffered(1))
    in_specs = [row(width), row(width), row(d), const(wo), const(g), const(wr), const(br)]
    args = [nf, ns, x, wo, g, wr, br]
    aliases = {}
    if prev is not None:
        in_specs += [pl.BlockSpec(memory_space=pl.ANY)] * 3
        aliases = {len(args) + i: i for i in range(3)}
        args += list(prev)
    return pl.pallas_call(
        _merge_kernel,
        grid=(n // tm,),
        in_specs=in_specs,
        out_specs=[out_row(d), out_row(d), out_row(LANES)],
        out_shape=[jax.ShapeDtypeStruct((n_total, d), F32), jax.ShapeDtypeStruct((n_total, d), F32),
                   jax.ShapeDtypeStruct((n_total, LANES), F32)],
        input_output_aliases=aliases,
        compiler_params=_params("arbitrary"),
        name="merge",
    )(*args)


def _route_kernel(lg_ref, idx_ref, gate_ref, cnt_ref, carry_ref, *, n_groups, per_group):
    tr = lg_ref.shape[0]

    @pl.when(pl.program_id(0) == 0)
    def _():
        carry_ref[...] = jnp.zeros_like(carry_ref)

    x = lg_ref[...]
    lane = lax.broadcasted_iota(jnp.int32, x.shape, 1).astype(F32)
    far = float(LANES)
    rmax = lambda a: jnp.max(a, axis=1, keepdims=True)
    first = lambda hit: jnp.min(jnp.where(hit, lane, far), axis=1, keepdims=True)

    gmask = lane < n_groups
    gl = jnp.where(gmask, x, -jnp.inf)
    gmax = rmax(gl)
    gsel = first(gl == gmax)
    p_g = 1.0 / jnp.sum(jnp.where(gmask, jnp.exp(x - gmax), 0.0), axis=1, keepdims=True)
    lo = n_groups + gsel * per_group
    emask = (lane >= lo) & (lane < lo + per_group)
    el = jnp.where(emask, x, -jnp.inf)
    m1 = rmax(el)
    i1 = first(el == m1)
    el2 = jnp.where(lane == i1, -jnp.inf, el)
    m2 = rmax(el2)
    i2 = first(el2 == m2)
    zsum = jnp.sum(jnp.where(emask, jnp.exp(x - m1), 0.0), axis=1, keepdims=True)
    p1 = 1.0 / zsum
    p2 = jnp.exp(m2 - m1) / zsum
    psum = p1 + p2
    gate1 = p_g * p1 / psum
    gate2 = p_g * p2 / psum
    e1 = i1 - n_groups
    e2 = i2 - n_groups
    hit1 = lane == e1
    hit2 = lane == e2
    onehot = jnp.where(hit1 | hit2, 1.0, 0.0)
    r = lax.broadcasted_iota(jnp.int32, (tr, tr), 0)
    c = lax.broadcasted_iota(jnp.int32, (tr, tr), 1)
    before = jnp.where(c < r, 1.0, 0.0).astype(BF16)
    seen = jnp.dot(before, onehot.astype(BF16), preferred_element_type=F32) + carry_ref[...]
    r1 = jnp.sum(jnp.where(hit1, seen, 0.0), axis=1, keepdims=True)
    r2 = jnp.sum(jnp.where(hit2, seen, 0.0), axis=1, keepdims=True)
    carry_ref[...] = carry_ref[...] + jnp.sum(onehot, axis=0, keepdims=True)
    cnt_ref[...] = carry_ref[...]

    col = lax.broadcasted_iota(jnp.int32, idx_ref.shape, 1)
    idx = jnp.where(col == 0, e1, jnp.where(col == 1, e2, jnp.where(col == 2, r1, r2)))
    idx_ref[...] = idx.astype(jnp.int32)
    gate_ref[...] = jnp.where(col == 0, gate1, jnp.where(col == 1, gate2, 0.0))


def _route(logits, n_groups, per_group):
    n = logits.shape[0]
    tr = ROW_TILE
    narrow = pl.BlockSpec((tr, 8), lambda i: (i, 0))
    return pl.pallas_call(
        functools.partial(_route_kernel, n_groups=n_groups, per_group=per_group),
        grid=(n // tr,),
        in_specs=[pl.BlockSpec((tr, LANES), lambda i: (i, 0))],
        out_specs=[narrow, narrow, pl.BlockSpec((1, LANES), lambda i: (0, 0))],
        out_shape=[jax.ShapeDtypeStruct((n, 8), jnp.int32), jax.ShapeDtypeStruct((n, 8), F32),
                   jax.ShapeDtypeStruct((1, LANES), F32)],
        scratch_shapes=[pltpu.VMEM((1, LANES), F32)],
        compiler_params=_params("arbitrary"),
        name="route",
    )(logits)


def _row_copy(src_ref, src_row, dst_ref, dst_row, sem):
    return pltpu.make_async_copy(src_ref.at[pl.ds(src_row, 1)], dst_ref.at[pl.ds(dst_row, 1)], sem)


def _dispatch_kernel(dest_ref, hn_ref, xs_ref, sem):
    td = hn_ref.shape[0]

    def start(r, _):
        for k in range(TOP_K):
            _row_copy(hn_ref, r, xs_ref, dest_ref[0, 0, TOP_K * r + k], sem).start()
        return 0

    def wait(r, _):
        for k in range(TOP_K):
            _row_copy(hn_ref, r, xs_ref, dest_ref[0, 0, TOP_K * r + k], sem).wait()
        return 0

    lax.fori_loop(0, td, start, 0)
    lax.fori_loop(0, td, wait, 0)


def _dispatch(dest, hn, n_rows):
    n, d = hn.shape
    td = ROW_TILE
    return pl.pallas_call(
        _dispatch_kernel,
        grid=(n // td,),
        in_specs=[pl.BlockSpec((1, 1, TOP_K * td), lambda i: (i, 0, 0), memory_space=pltpu.SMEM),
                  pl.BlockSpec((td, d), lambda i: (i, 0))],
        out_specs=pl.BlockSpec(memory_space=pl.ANY),
        out_shape=jax.ShapeDtypeStruct((n_rows, d), F32),
        scratch_shapes=[pltpu.SemaphoreType.DMA(())],
        compiler_params=_params("arbitrary"),
        name="dispatch",
    )(dest.reshape(n // td, 1, TOP_K * td), hn)


def _expert_kernel(be_ref, nv_ref, xs_ref, w1_ref, w3_ref, w2_ref, o_ref):
    nv = nv_ref[pl.program_id(0)]

    @pl.when(nv > 0)
    def _():
        rows = lax.broadcasted_iota(jnp.int32, xs_ref.shape, 0)
        x = jnp.where(rows < nv, xs_ref[...], 0.0).astype(BF16)
        a = jnp.dot(x, w1_ref[...], preferred_element_type=F32)
        b = jnp.dot(x, w3_ref[...], preferred_element_type=F32)
        act = (a * jax.nn.sigmoid(a) * b).astype(BF16)
        o_ref[...] = jnp.dot(act, w2_ref[...], preferred_element_type=F32)

    @pl.when(nv == 0)
    def _():
        o_ref[...] = jnp.zeros_like(o_ref)


def _experts(block_expert, n_valid, xs, w1, w3, w2):
    p, d = xs.shape
    de = w1.shape[2]
    blk = EXPERT_BLOCK
    return pl.pallas_call(
        _expert_kernel,
        grid_spec=pltpu.PrefetchScalarGridSpec(
            num_scalar_prefetch=2,
            grid=(p // blk,),
            in_specs=[pl.BlockSpec((blk, d), lambda i, be, nv: (i, 0)),
                      pl.BlockSpec((None, d, de), lambda i, be, nv: (be[i], 0, 0)),
                      pl.BlockSpec((None, d, de), lambda i, be, nv: (be[i], 0, 0)),
                      pl.BlockSpec((None, de, d), lambda i, be, nv: (be[i], 0, 0))],
            out_specs=pl.BlockSpec((blk, d), lambda i, be, nv: (i, 0))),
        out_shape=jax.ShapeDtypeStruct((p, d), F32),
        compiler_params=_params("arbitrary"),
        name="experts",
    )(block_expert, n_valid, xs, w1, w3, w2)


def _combine_kernel(dest_ref, gate_ref, h_ref, rows_ref, g_ref, y_ref, buf, sem, *, final_norm):
    tc = h_ref.shape[0]

    def start(r, _):
        for k in range(TOP_K):
            _row_copy(rows_ref, dest_ref[0, 0, TOP_K * r + k], buf.at[k], r, sem).start()
        return 0

    def wait(r, _):
        for k in range(TOP_K):
            _row_copy(rows_ref, dest_ref[0, 0, TOP_K * r + k], buf.at[k], r, sem).wait()
        return 0

    lax.fori_loop(0, tc, start, 0)
    lax.fori_loop(0, tc, wait, 0)
    gates = gate_ref[...]
    y = gates[:, 0:1] * buf[0] + gates[:, 1:2] * buf[1]
    h = h_ref[...] + y
    y_ref[...] = _rms(h, g_ref[...]) if final_norm else h


def _combine(dest, gates, h, rows, g, n, row_off, final_norm):
    d = h.shape[1]
    tc = COMBINE_TILE
    off = row_off // tc
    return pl.pallas_call(
        functools.partial(_combine_kernel, final_norm=final_norm),
        grid=(n // tc,),
        in_specs=[pl.BlockSpec((1, 1, TOP_K * tc), lambda i: (i + off, 0, 0), memory_space=pltpu.SMEM),
                  pl.BlockSpec((tc, 8), lambda i: (i + off, 0)),
                  pl.BlockSpec((tc, d), lambda i: (i + off, 0)),
                  pl.BlockSpec(memory_space=pl.ANY),
                  pl.BlockSpec((1, d), lambda i: (0, 0))],
        out_specs=pl.BlockSpec((tc, d), lambda i: (i, 0)),
        out_shape=jax.ShapeDtypeStruct((n, d), F32),
        scratch_shapes=[pltpu.VMEM((TOP_K, tc, d), F32), pltpu.SemaphoreType.DMA(())],
        compiler_params=_params("arbitrary"),
        name="combine",
    )(dest.reshape(-1, 1, TOP_K * tc), gates, h, rows, g)


def _round_up(a, m):
    return -(-a // m) * m


def _moe_plan(idx, counts, n_experts):
    blk = EXPERT_BLOCK
    n = idx.shape[0]
    counts = counts[0, :n_experts].astype(jnp.int32)
    padded = (counts + blk - 1) // blk * blk
    pend = jnp.cumsum(padded)
    pstart = pend - padded
    dest = pstart[idx[:, :TOP_K]] + idx[:, TOP_K:2 * TOP_K]
    n_blocks = -(-(n * TOP_K + n_experts * (blk - 1)) // blk)
    block_start = jnp.arange(n_blocks, dtype=jnp.int32) * blk
    block_expert = jnp.minimum(jnp.searchsorted(pend, block_start, side="right"), n_experts - 1).astype(jnp.int32)
    seg_end = (pstart + counts)[block_expert]
    n_valid = jnp.clip(seg_end - block_start, 0, blk).astype(jnp.int32)
    return dest.astype(jnp.int32), block_expert, n_valid, n_blocks * blk


def _split_w_in(w_in, width, n_heads):
    wm = jnp.concatenate([w_in[:, :3 * width], w_in[:, 3 * width + n_heads:]], axis=1).astype(BF16)
    wf = jnp.pad(w_in[:, 3 * width:3 * width + n_heads], ((0, 0), (0, LANES - n_heads))).astype(BF16)
    return wm, wf


def kernel(x_prompt, x_sample, cache_fox_k, cache_fox_v, cache_fox_logf, cache_sb_k, cache_sb_v,
           w_in, b_f, g_attn, g_out_fox, g_out_sb, w_out, g_ffn, w_group, b_group, w_router, b_router,
           w1, w3, w2, g_final):
    depth = w_in.shape[0]
    bp, sp, d = x_prompt.shape
    bs, ss, _ = x_sample.shape
    past = cache_fox_logf.shape[2]
    n_heads = cache_fox_logf.shape[3]
    width = n_heads * HEAD_DIM
    n_groups = w_group.shape[2]
    n_experts = w_router.shape[2]
    per_group = n_experts // n_groups
    n_p, n_s = bp * sp, bs * ss
    n_total = n_p + n_s
    sk_s = _round_up(past + ss, LANES)
    row2 = lambda a: a.reshape(1, -1)

    hp = x_prompt.reshape(n_p, d)
    hs = x_sample.reshape(n_s, d)
    outs_p, outs_s = [], []
    for l in range(depth):
        wm, wf = _split_w_in(w_in[l], width, n_heads)
        bfp = jnp.pad(b_f[l], (0, LANES - n_heads)).reshape(1, LANES)
        wo = w_out[l].astype(BF16)
        wr = jnp.pad(jnp.concatenate([w_group[l], w_router[l]], axis=1),
                     ((0, 0), (0, LANES - n_groups - n_experts))).astype(BF16)
        br = jnp.pad(jnp.concatenate([b_group[l], b_router[l]]), (0, LANES - n_groups - n_experts)).reshape(1, LANES)
        g_a, g_f = row2(g_attn[l]), row2(g_ffn[l])
        g_of, g_os = row2(g_out_fox[l]), row2(g_out_sb[l])

        qf, kf, vf, lf, qs, ks, vs = _in_proj(hp, g_a, wm, wf, bfp, n_heads)
        b3 = lambda a: a.reshape(bp, sp, -1)
        cf = _cumsum(b3(lf), min(sp, 512))
        tq = min(sp, 256)
        nf = _attention("fox", b3(qf), b3(kf), b3(vf), g_of, cf, jnp.swapaxes(cf, 1, 2), tq=tq, tk=tq, q_off=0)
        ns = _attention("sb", b3(qs), b3(ks), b3(vs), g_os, None, None, tq=tq, tk=tq, q_off=0)
        shared = _merge(nf.reshape(n_p, width), ns.reshape(n_p, width), hp, wo, g_f, wr, br, n_total, 0, None)
        outs_p.append((kf, vf, lf, ks, vs))

        qf2, kf2, vf2, lf2, qs2, ks2, vs2 = _in_proj(hs, g_a, wm, wf, bfp, n_heads)
        c3 = lambda a: a.reshape(bs, ss, -1)
        keys = lambda cache, new: jnp.concatenate(
            [cache.reshape(bs, past, width), c3(new), jnp.zeros((bs, sk_s - past - ss, width), F32)], axis=1)
        cf_all = _cumsum(jnp.concatenate([cache_fox_logf[l].astype(F32), c3(lf2)], axis=1), past + ss)
        ck = jnp.pad(jnp.swapaxes(cf_all, 1, 2), ((0, 0), (0, 0), (0, sk_s - past - ss)))
        nf2 = _attention("fox", c3(qf2), keys(cache_fox_k[l], kf2), keys(cache_fox_v[l], vf2), g_of,
                         cf_all[:, past:], ck, tq=ss, tk=LANES, q_off=past)
        ns2 = _attention("sb", c3(qs2), keys(cache_sb_k[l], ks2), keys(cache_sb_v[l], vs2), g_os,
                         None, None, tq=ss, tk=LANES, q_off=past)
        h_all, hn_all, logits = _merge(nf2.reshape(n_s, width), ns2.reshape(n_s, width), hs, wo, g_f, wr, br,
                                       n_total, n_p, shared)
        outs_s.append((kf2, vf2, lf2, ks2, vs2))

        idx, gates, counts = _route(logits, n_groups, per_group)
        dest, block_expert, n_valid, n_rows = _moe_plan(idx, counts, n_experts)
        xs = _dispatch(dest, hn_all, n_rows)
        rows = _experts(block_expert, n_valid, xs, w1[l].astype(BF16), w3[l].astype(BF16), w2[l].astype(BF16))
        last = l == depth - 1
        g_last = row2(g_final)
        hp = _combine(dest, gates, h_all, rows, g_last, n_p, 0, last)
        hs = _combine(dest, gates, h_all, rows, g_last, n_s, n_p, last)

    heads = lambda a, b, s: a.reshape(b, s, n_heads, HEAD_DIM)
    stack = lambda outs, i, f: jnp.stack([f(o[i]) for o in outs])
    hp_f = lambda a: heads(a, bp, sp)
    hs_f = lambda a: heads(a, bs, ss)
    return (hp.reshape(bp, sp, d), hs.reshape(bs, ss, d),
            stack(outs_p, 0, hp_f), stack(outs_p, 1, hp_f), stack(outs_p, 2, lambda a: a.reshape(bp, sp, n_heads)),
            stack(outs_p, 3, hp_f), stack(outs_p, 4, hp_f),
            stack(outs_s, 0, hs_f), stack(outs_s, 1, hs_f), stack(outs_s, 2, lambda a: a.reshape(bs, ss, n_heads)),
            stack(outs_s, 3, hs_f), stack(outs_s, 4, hs_f))
```

```python
import functools
import math

import jax
import jax.numpy as jnp
from jax import lax
from jax.experimental import pallas as pl
from jax.experimental.pallas import tpu as pltpu

F32 = jnp.float32
BF16 = jnp.bfloat16

HEAD_DIM = 128
TOP_K = 2
EPS = 1e-6
LANES = 128
MXU_DIM = 256
NEG_BIG = -1e30
VMEM_LIMIT = 56 * 1024 * 1024
LOG2E = math.log2(math.e)
SCALE = HEAD_DIM ** -0.5
SCALE_LOG2 = SCALE * LOG2E

ROW_TILE = 256
ATTN_TILE = 512
EXPERT_BLOCK = 512
COMBINE_TILE = 128


def _params(*sem):
    return pltpu.CompilerParams(dimension_semantics=sem, vmem_limit_bytes=VMEM_LIMIT)


def _log_sigmoid(x):
    return jnp.minimum(x, 0.0) - jnp.log1p(jnp.exp(-jnp.abs(x)))


def _rms(x, g):
    return x * lax.rsqrt(jnp.mean(x * x, axis=-1, keepdims=True) + EPS) * g


def _in_proj_kernel(x_ref, g_ref, wm_ref, wf_ref, bf_ref,
                    qf_ref, kf_ref, vf_ref, lf_ref, qs_ref, ks_ref, vs_ref, *, width, n_heads):
    hn = _rms(x_ref[...], g_ref[...]).astype(BF16)
    for c, o_ref in enumerate((qf_ref, kf_ref, vf_ref, qs_ref, ks_ref, vs_ref)):
        o_ref[...] = jnp.dot(hn, wm_ref[:, c * width:(c + 1) * width],
                             preferred_element_type=F32).astype(o_ref.dtype)
    fl = jnp.dot(hn, wf_ref[...], preferred_element_type=F32) + bf_ref[...]
    lf_ref[...] = _log_sigmoid(fl)[:, :n_heads]


def _in_proj(x, g, wm, wf, bf, n_heads):
    n, d = x.shape
    width = wm.shape[1] // 6
    tm = ROW_TILE
    row = lambda w: pl.BlockSpec((tm, w), lambda i: (i, 0))
    const = lambda a: pl.BlockSpec(a.shape, lambda i: (0, 0), pipeline_mode=pl.Buffered(1))
    wide = lambda dt: jax.ShapeDtypeStruct((n, width), dt)
    return pl.pallas_call(
        functools.partial(_in_proj_kernel, width=width, n_heads=n_heads),
        grid=(n // tm,),
        in_specs=[row(d), const(g), const(wm), const(wf), const(bf)],
        out_specs=[row(width), row(width), row(width), row(n_heads), row(width), row(width), row(width)],
        out_shape=[wide(BF16), wide(F32), wide(F32), jax.ShapeDtypeStruct((n, n_heads), F32),
                   wide(BF16), wide(F32), wide(F32)],
        compiler_params=_params("arbitrary"),
        name="in_proj",
    )(x, g, wm, wf, bf)


def _cumsum_kernel(lf_ref, cf_ref, carry_ref):
    tt = lf_ref.shape[0]

    @pl.when(pl.program_id(1) == 0)
    def _():
        carry_ref[...] = jnp.zeros_like(carry_ref)

    r = lax.broadcasted_iota(jnp.int32, (tt, tt), 0)
    c = lax.broadcasted_iota(jnp.int32, (tt, tt), 1)
    tri = jnp.where(c <= r, 1.0, 0.0).astype(F32)
    cf = jnp.dot(tri, lf_ref[...], precision=lax.Precision.HIGHEST,
                 preferred_element_type=F32) + carry_ref[...]
    cf_ref[...] = cf
    carry_ref[...] = cf[tt - 1:tt, :]


def _cumsum(lf, tt):
    b, t, h = lf.shape
    return pl.pallas_call(
        _cumsum_kernel,
        grid=(b, t // tt),
        in_specs=[pl.BlockSpec((None, tt, h), lambda i, j: (i, j, 0))],
        out_specs=pl.BlockSpec((None, tt, h), lambda i, j: (i, j, 0)),
        out_shape=jax.ShapeDtypeStruct((b, t, h), F32),
        scratch_shapes=[pltpu.VMEM((1, h), F32)],
        compiler_params=_params("arbitrary", "arbitrary"),
        name="cumsum",
    )(lf)


def _split3(x):
    hi = x.astype(BF16).astype(F32)
    r1 = x - hi
    mid = r1.astype(BF16).astype(F32)
    lo = (r1 - mid).astype(BF16).astype(F32)
    return hi, mid, lo


def _bias_block(c, as_query):
    hi, mid, lo = _split3(c * (1.0 / SCALE))
    lane = lax.broadcasted_iota(jnp.int32, (c.shape[0], LANES), 1)
    if as_query:
        parts, base, ones_lo = (hi, mid, lo), 0, 3
    else:
        parts, base, ones_lo = (-hi, -mid, -lo), 3, 0
    blk = jnp.where((lane >= ones_lo) & (lane < ones_lo + 3), 1.0, 0.0)
    for i, p in enumerate(parts):
        blk = jnp.where(lane == base + i, p, blk)
    return blk.astype(BF16)


def _select_col(x, h):
    lane = lax.broadcasted_iota(jnp.int32, x.shape, 1)
    return jnp.sum(jnp.where(lane == h, x, 0.0), axis=1, keepdims=True)


def _fox_kernel(q_ref, k_ref, v_ref, cq_ref, ck_ref, g_ref, o_ref, kaug, vbf, *, tq, tk, q_off, prep):
    h = pl.program_id(1)
    qi = pl.program_id(2)
    sk = k_ref.shape[0]

    @pl.when(qi == 0)
    def _():
        for r0 in range(0, sk, prep):
            rows = pl.ds(r0, prep)
            kaug[rows, :HEAD_DIM] = k_ref[rows, :].astype(BF16)
            kaug[rows, HEAD_DIM:] = _bias_block(_select_col(ck_ref[rows, :], h), False)
            vbf[rows, :] = v_ref[rows, :].astype(BF16)

    qa = jnp.concatenate([q_ref[...], _bias_block(_select_col(cq_ref[...], h), True)], axis=1)
    q_lo = q_off + qi * tq
    qpos = q_lo + lax.broadcasted_iota(jnp.int32, (tq, tk), 0)
    kiota = lax.broadcasted_iota(jnp.int32, (tq, tk), 1)

    def chunk(j, carry, masked):
        m, l, acc = carry
        k0 = pl.multiple_of(j * tk, tk)
        s = lax.dot_general(qa, kaug[pl.ds(k0, tk), :], (((1,), (1,)), ((), ())),
                            preferred_element_type=F32) * SCALE_LOG2
        if masked:
            s = jnp.where(k0 + kiota <= qpos, s, NEG_BIG)
        m_new = jnp.maximum(m, jnp.max(s, axis=1, keepdims=True))
        p = jnp.exp2(s - m_new)
        alpha = jnp.exp2(m - m_new)
        l = alpha * l + jnp.sum(p, axis=1, keepdims=True)
        acc = alpha * acc + jnp.dot(p.astype(BF16), vbf[pl.ds(k0, tk), :], preferred_element_type=F32)
        return m_new, l, acc

    n_full = (q_lo + 1) // tk
    n_all = (q_lo + tq + tk - 1) // tk
    init = (jnp.full((tq, 1), NEG_BIG, F32), jnp.zeros((tq, 1), F32), jnp.zeros((tq, HEAD_DIM), F32))
    carry = lax.fori_loop(0, n_full, lambda j, c: chunk(j, c, False), init)
    m, l, acc = lax.fori_loop(n_full, n_all, lambda j, c: chunk(j, c, True), carry)
    o_ref[...] = _rms(acc / l, g_ref[...]).astype(o_ref.dtype)


def _sb_kernel(q_ref, k_ref, v_ref, g_ref, o_ref, kbf, vbf, *, tq, tk, q_off, prep):
    qi = pl.program_id(2)
    sk = k_ref.shape[0]

    @pl.when(qi == 0)
    def _():
        for r0 in range(0, sk, prep):
            rows = pl.ds(r0, prep)
            kbf[rows, :] = k_ref[rows, :].astype(BF16)
            vbf[rows, :] = v_ref[rows, :].astype(BF16)

    q = q_ref[...]
    q_lo = q_off + qi * tq
    qpos = q_lo + lax.broadcasted_iota(jnp.int32, (tq, tk), 0)
    kiota = lax.broadcasted_iota(jnp.int32, (tq, tk), 1)
    ts = min(tk, MXU_DIM)
    r = lax.broadcasted_iota(jnp.int32, (ts, ts), 0)
    c = lax.broadcasted_iota(jnp.int32, (ts, ts), 1)
    neg_from = jnp.where(r >= c, -1.0, 0.0).astype(BF16)
    sign_bit = jnp.uint32(0x80000000)
    high_half = jnp.uint32(0xFFFF0000)

    def chunk(j, carry, masked):
        tail, acc = carry
        k0 = pl.multiple_of(j * tk, tk)
        z = lax.dot_general(q, kbf[pl.ds(k0, tk), :], (((1,), (1,)), ((), ())),
                            preferred_element_type=F32) * SCALE_LOG2
        neg_abs = pltpu.bitcast(pltpu.bitcast(z, jnp.uint32) | sign_bit, F32)
        sp = jnp.maximum(z, 0.0) + jnp.log2(1.0 + jnp.exp2(neg_abs))
        if masked:
            valid = k0 + kiota < qpos
            sp = jnp.where(valid, sp, 0.0)
        hi = pltpu.bitcast(pltpu.bitcast(sp, jnp.uint32) & high_half, F32)
        lo = (sp - hi).astype(BF16)
        hi = hi.astype(BF16)
        parts = []
        after = tail
        for b in reversed(range(tk // ts)):
            cols = slice(b * ts, (b + 1) * ts)
            sfx = (jnp.dot(hi[:, cols], neg_from, preferred_element_type=F32)
                   + jnp.dot(lo[:, cols], neg_from, preferred_element_type=F32)) + after
            parts.append(sfx)
            after = sfx[:, 0:1]
        suffix = parts[0] if len(parts) == 1 else jnp.concatenate(parts[::-1], axis=1)
        a = jnp.exp2(z + suffix)
        if masked:
            a = jnp.where(valid, a, 0.0)
        acc = acc + jnp.dot(a.astype(BF16), vbf[pl.ds(k0, tk), :], preferred_element_type=F32)
        return after, acc

    n_full = q_lo // tk
    n_all = (q_lo + tq - 1 + tk - 1) // tk
    init = (jnp.zeros((tq, 1), F32), jnp.zeros((tq, HEAD_DIM), F32))
    carry = lax.fori_loop(0, n_all - n_full, lambda i, c: chunk(n_all - 1 - i, c, True), init)
    _, acc = lax.fori_loop(0, n_full, lambda i, c: chunk(n_full - 1 - i, c, False), carry)
    o_ref[...] = _rms(acc, g_ref[...]).astype(o_ref.dtype)


def _attention(kind, q, k, v, g, cf, *, tq, tk, q_off):
    b, sq, width = q.shape
    sk = k.shape[1]
    n_heads = width // HEAD_DIM
    assert sq % tq == 0 and sk % tk == 0 and q_off % tq == 0
    prep = math.gcd(sk, 512)
    q_spec = pl.BlockSpec((None, tq, HEAD_DIM), lambda i, h, j: (i, j, h))
    kv_spec = pl.BlockSpec((None, sk, HEAD_DIM), lambda i, h, j: (i, 0, h))
    g_spec = pl.BlockSpec((1, HEAD_DIM), lambda i, h, j: (0, h))
    kv_scratch = pltpu.VMEM((sk, HEAD_DIM), BF16)
    if kind == "fox":
        body = functools.partial(_fox_kernel, tq=tq, tk=tk, q_off=q_off, prep=prep)
        q_blk = q_off // tq
        in_specs = [q_spec, kv_spec, kv_spec,
                    pl.BlockSpec((None, tq, n_heads), lambda i, h, j: (i, j + q_blk, 0)),
                    pl.BlockSpec((None, sk, n_heads), lambda i, h, j: (i, 0, 0)), g_spec]
        args = (q, k, v, cf, cf, g)
        scratch = [pltpu.VMEM((sk, 2 * HEAD_DIM), BF16), kv_scratch]
    else:
        body = functools.partial(_sb_kernel, tq=tq, tk=tk, q_off=q_off, prep=prep)
        in_specs = [q_spec, kv_spec, kv_spec, g_spec]
        args = (q, k, v, g)
        scratch = [kv_scratch, kv_scratch]
    return pl.pallas_call(
        body,
        grid=(b, n_heads, sq // tq),
        in_specs=in_specs,
        out_specs=q_spec,
        out_shape=jax.ShapeDtypeStruct((b, sq, width), BF16),
        scratch_shapes=scratch,
        compiler_params=_params("arbitrary", "arbitrary", "arbitrary"),
        name=kind,
    )(*args)


def _merge_kernel(nfp_ref, nsp_ref, xp_ref, nfs_ref, nss_ref, xs_ref, wo_ref, g_ref, wr_ref, br_ref,
                  h_ref, hn_ref, lg_ref, *, prompt_blocks):
    def body(nf_ref, ns_ref, x_ref):
        width = nf_ref.shape[1]
        h = x_ref[...] + (jnp.dot(nf_ref[...], wo_ref[:width, :], preferred_element_type=F32)
                          + jnp.dot(ns_ref[...], wo_ref[width:, :], preferred_element_type=F32))
        h_ref[...] = h
        hn = _rms(h, g_ref[...])
        hn_ref[...] = hn
        lg_ref[...] = jnp.dot(hn.astype(BF16), wr_ref[...], preferred_element_type=F32) + br_ref[...]

    is_prompt = pl.program_id(0) < prompt_blocks
    pl.when(is_prompt)(lambda: body(nfp_ref, nsp_ref, xp_ref))
    pl.when(jnp.logical_not(is_prompt))(lambda: body(nfs_ref, nss_ref, xs_ref))


def _merge(prompt, sample, wo, g, wr, br):
    n_p, d = prompt[2].shape
    n_s = sample[2].shape[0]
    width = prompt[0].shape[1]
    tm = ROW_TILE
    pb, sb = n_p // tm, n_s // tm
    n_total = n_p + n_s
    p_row = lambda w: pl.BlockSpec((tm, w), lambda i: (jnp.minimum(i, pb - 1), 0))
    s_row = lambda w: pl.BlockSpec((tm, w), lambda i: (jnp.maximum(i - pb, 0), 0))
    out_row = lambda w: pl.BlockSpec((tm, w), lambda i: (i, 0))
    const = lambda a: pl.BlockSpec(a.shape, lambda i: (0, 0), pipeline_mode=pl.Buffered(1))
    return pl.pallas_call(
        functools.partial(_merge_kernel, prompt_blocks=pb),
        grid=(pb + sb,),
        in_specs=[p_row(width), p_row(width), p_row(d), s_row(width), s_row(width), s_row(d),
                  const(wo), const(g), const(wr), const(br)],
        out_specs=[out_row(d), out_row(d), out_row(LANES)],
        out_shape=[jax.ShapeDtypeStruct((n_total, d), F32), jax.ShapeDtypeStruct((n_total, d), F32),
                   jax.ShapeDtypeStruct((n_total, LANES), F32)],
        compiler_params=_params("arbitrary"),
        name="merge",
    )(*prompt, *sample, wo, g, wr, br)


def _route_kernel(lg_ref, idx_ref, gate_ref, cnt_ref, carry_ref, *, n_groups, per_group):
    tr = lg_ref.shape[0]

    @pl.when(pl.program_id(0) == 0)
    def _():
        carry_ref[...] = jnp.zeros_like(carry_ref)

    x = lg_ref[...]
    lane = lax.broadcasted_iota(jnp.int32, x.shape, 1).astype(F32)
    far = float(LANES)
    rmax = lambda a: jnp.max(a, axis=1, keepdims=True)
    first = lambda hit: jnp.min(jnp.where(hit, lane, far), axis=1, keepdims=True)

    gmask = lane < n_groups
    gl = jnp.where(gmask, x, -jnp.inf)
    gmax = rmax(gl)
    gsel = first(gl == gmax)
    p_g = 1.0 / jnp.sum(jnp.where(gmask, jnp.exp(x - gmax), 0.0), axis=1, keepdims=True)
    lo = n_groups + gsel * per_group
    emask = (lane >= lo) & (lane < lo + per_group)
    el = jnp.where(emask, x, -jnp.inf)
    m1 = rmax(el)
    i1 = first(el == m1)
    el2 = jnp.where(lane == i1, -jnp.inf, el)
    m2 = rmax(el2)
    i2 = first(el2 == m2)
    zsum = jnp.sum(jnp.where(emask, jnp.exp(x - m1), 0.0), axis=1, keepdims=True)
    p1 = 1.0 / zsum
    p2 = jnp.exp(m2 - m1) / zsum
    psum = p1 + p2
    gate1 = p_g * p1 / psum
    gate2 = p_g * p2 / psum
    e1 = i1 - n_groups
    e2 = i2 - n_groups
    hit1 = lane == e1
    hit2 = lane == e2
    onehot = jnp.where(hit1 | hit2, 1.0, 0.0)
    r = lax.broadcasted_iota(jnp.int32, (tr, tr), 0)
    c = lax.broadcasted_iota(jnp.int32, (tr, tr), 1)
    before = jnp.where(c < r, 1.0, 0.0).astype(BF16)
    seen = jnp.dot(before, onehot.astype(BF16), preferred_element_type=F32) + carry_ref[...]
    r1 = jnp.sum(jnp.where(hit1, seen, 0.0), axis=1, keepdims=True)
    r2 = jnp.sum(jnp.where(hit2, seen, 0.0), axis=1, keepdims=True)
    carry_ref[...] = carry_ref[...] + jnp.sum(onehot, axis=0, keepdims=True)
    cnt_ref[...] = carry_ref[...]

    col = lax.broadcasted_iota(jnp.int32, idx_ref.shape, 1)
    idx = jnp.where(col == 0, e1, jnp.where(col == 1, e2, jnp.where(col == 2, r1, r2)))
    idx_ref[...] = idx.astype(jnp.int32)
    gate_ref[...] = jnp.where(col == 0, gate1, jnp.where(col == 1, gate2, 0.0))


def _route(logits, n_groups, per_group):
    n = logits.shape[0]
    tr = ROW_TILE
    narrow = pl.BlockSpec((tr, 8), lambda i: (i, 0))
    return pl.pallas_call(
        functools.partial(_route_kernel, n_groups=n_groups, per_group=per_group),
        grid=(n // tr,),
        in_specs=[pl.BlockSpec((tr, LANES), lambda i: (i, 0))],
        out_specs=[narrow, narrow, pl.BlockSpec((1, LANES), lambda i: (0, 0))],
        out_shape=[jax.ShapeDtypeStruct((n, 8), jnp.int32), jax.ShapeDtypeStruct((n, 8), F32),
                   jax.ShapeDtypeStruct((1, LANES), F32)],
        scratch_shapes=[pltpu.VMEM((1, LANES), F32)],
        compiler_params=_params("arbitrary"),
        name="route",
    )(logits)


def _row_copy(src_ref, src_row, dst_ref, dst_row, sem):
    return pltpu.make_async_copy(src_ref.at[pl.ds(src_row, 1)], dst_ref.at[pl.ds(dst_row, 1)], sem)


def _dispatch_kernel(dest_ref, hn_ref, xs_ref, sem):
    td = hn_ref.shape[0]

    def start(r, _):
        for k in range(TOP_K):
            _row_copy(hn_ref, r, xs_ref, dest_ref[0, 0, TOP_K * r + k], sem).start()
        return 0

    def wait(r, _):
        for k in range(TOP_K):
            _row_copy(hn_ref, r, xs_ref, dest_ref[0, 0, TOP_K * r + k], sem).wait()
        return 0

    lax.fori_loop(0, td, start, 0)
    lax.fori_loop(0, td, wait, 0)


def _dispatch(dest, hn):
    n, d = hn.shape
    td = ROW_TILE
    return pl.pallas_call(
        _dispatch_kernel,
        grid=(n // td,),
        in_specs=[pl.BlockSpec((1, 1, TOP_K * td), lambda i: (i, 0, 0), memory_space=pltpu.SMEM),
                  pl.BlockSpec((td, d), lambda i: (i, 0))],
        out_specs=pl.BlockSpec(memory_space=pl.ANY),
        out_shape=jax.ShapeDtypeStruct((n * TOP_K, d), F32),
        scratch_shapes=[pltpu.SemaphoreType.DMA(())],
        compiler_params=_params("arbitrary"),
        name="dispatch",
    )(dest.reshape(n // td, 1, TOP_K * td), hn)


def _expert_kernel(blk_ref, exp_ref, lo_ref, hi_ref, first_ref, xs_ref, w1_ref, w3_ref, w2_ref, o_ref):
    w = pl.program_id(0)
    lo, hi = lo_ref[w], hi_ref[w]

    def swiglu():
        rows = lax.broadcasted_iota(jnp.int32, xs_ref.shape, 0)
        x = jnp.where((rows >= lo) & (rows < hi), xs_ref[...], 0.0).astype(BF16)
        a = jnp.dot(x, w1_ref[...], preferred_element_type=F32)
        b = jnp.dot(x, w3_ref[...], preferred_element_type=F32)
        act = (a * jax.nn.sigmoid(a) * b).astype(BF16)
        return jnp.dot(act, w2_ref[...], preferred_element_type=F32)

    @pl.when((hi > lo) & (first_ref[w] == 1))
    def _():
        o_ref[...] = swiglu()

    @pl.when((hi > lo) & (first_ref[w] == 0))
    def _():
        o_ref[...] += swiglu()


def _experts(plan, xs, w1, w3, w2):
    m, d = xs.shape
    de = w1.shape[2]
    blk = EXPERT_BLOCK
    n_items = plan[0].shape[0]
    row_blk = lambda i, rb, ex, lo, hi, fs: (rb[i], 0)
    weight = lambda i, rb, ex, lo, hi, fs: (ex[i], 0, 0)
    return pl.pallas_call(
        _expert_kernel,
        grid_spec=pltpu.PrefetchScalarGridSpec(
            num_scalar_prefetch=5,
            grid=(n_items,),
            in_specs=[pl.BlockSpec((blk, d), row_blk),
                      pl.BlockSpec((None, d, de), weight),
                      pl.BlockSpec((None, d, de), weight),
                      pl.BlockSpec((None, de, d), weight)],
            out_specs=pl.BlockSpec((blk, d), row_blk)),
        out_shape=jax.ShapeDtypeStruct((m, d), F32),
        compiler_params=_params("arbitrary"),
        name="experts",
    )(*plan, xs, w1, w3, w2)


def _combine_kernel(dest_ref, gate_ref, h_ref, rows_ref, g_ref, y_ref, buf, sem, *, final_norm):
    tc = h_ref.shape[0]

    def start(r, _):
        for k in range(TOP_K):
            _row_copy(rows_ref, dest_ref[0, 0, TOP_K * r + k], buf.at[k], r, sem).start()
        return 0

    def wait(r, _):
        for k in range(TOP_K):
            _row_copy(rows_ref, dest_ref[0, 0, TOP_K * r + k], buf.at[k], r, sem).wait()
        return 0

    lax.fori_loop(0, tc, start, 0)
    lax.fori_loop(0, tc, wait, 0)
    gates = gate_ref[...]
    y = gates[:, 0:1] * buf[0] + gates[:, 1:2] * buf[1]
    h = h_ref[...] + y
    y_ref[...] = _rms(h, g_ref[...]) if final_norm else h


def _combine(dest, gates, h, rows, g, n, row_off, final_norm):
    d = h.shape[1]
    tc = COMBINE_TILE
    off = row_off // tc
    return pl.pallas_call(
        functools.partial(_combine_kernel, final_norm=final_norm),
        grid=(n // tc,),
        in_specs=[pl.BlockSpec((1, 1, TOP_K * tc), lambda i: (i + off, 0, 0), memory_space=pltpu.SMEM),
                  pl.BlockSpec((tc, 8), lambda i: (i + off, 0)),
                  pl.BlockSpec((tc, d), lambda i: (i + off, 0)),
                  pl.BlockSpec(memory_space=pl.ANY),
                  pl.BlockSpec((1, d), lambda i: (0, 0))],
        out_specs=pl.BlockSpec((tc, d), lambda i: (i, 0)),
        out_shape=jax.ShapeDtypeStruct((n, d), F32),
        scratch_shapes=[pltpu.VMEM((TOP_K, tc, d), F32), pltpu.SemaphoreType.DMA(())],
        compiler_params=_params("arbitrary"),
        name="combine",
    )(dest.reshape(-1, 1, TOP_K * tc), gates, h, rows, g)


def _round_up(a, m):
    return -(-a // m) * m


def _moe_plan(idx, counts, n_experts):
    blk = EXPERT_BLOCK
    n = idx.shape[0]
    i32 = lambda a: a.astype(jnp.int32)
    counts = i32(counts[0, :n_experts])
    end = jnp.cumsum(counts)
    start = end - counts
    dest = i32(start[idx[:, :TOP_K]] + idx[:, TOP_K:2 * TOP_K])
    n_blocks = n * TOP_K // blk
    block_lo = jnp.arange(n_blocks, dtype=jnp.int32) * blk
    e_first = jnp.minimum(i32(jnp.searchsorted(end, block_lo, side="right")), n_experts - 1)
    e_last = jnp.minimum(i32(jnp.searchsorted(end, block_lo + blk - 1, side="right")), n_experts - 1)
    item_end = jnp.cumsum(e_last - e_first + 1)
    item_start = item_end - (e_last - e_first + 1)
    n_items = n_blocks + n_experts - 1
    w = jnp.arange(n_items, dtype=jnp.int32)
    live = w < item_end[-1]
    rb = jnp.minimum(i32(jnp.searchsorted(item_end, w, side="right")), n_blocks - 1)
    ex = jnp.where(live, e_first[rb] + (w - item_start[rb]), e_last[-1])
    lo = jnp.where(live, jnp.clip(start[ex] - block_lo[rb], 0, blk), 0)
    hi = jnp.where(live, jnp.clip(end[ex] - block_lo[rb], 0, blk), 0)
    first = i32(w == item_start[rb])
    return dest, (rb, i32(ex), i32(lo), i32(hi), first)


def _split_w_in(w_in, width, n_heads):
    wm = jnp.concatenate([w_in[:, :3 * width], w_in[:, 3 * width + n_heads:]], axis=1).astype(BF16)
    wf = jnp.pad(w_in[:, 3 * width:3 * width + n_heads], ((0, 0), (0, LANES - n_heads))).astype(BF16)
    return wm, wf


def kernel(x_prompt, x_sample, cache_fox_k, cache_fox_v, cache_fox_logf, cache_sb_k, cache_sb_v,
           w_in, b_f, g_attn, g_out_fox, g_out_sb, w_out, g_ffn, w_group, b_group, w_router, b_router,
           w1, w3, w2, g_final):
    depth = w_in.shape[0]
    bp, sp, d = x_prompt.shape
    bs, ss, _ = x_sample.shape
    past = cache_fox_logf.shape[2]
    n_heads = cache_fox_logf.shape[3]
    width = n_heads * HEAD_DIM
    n_groups = w_group.shape[2]
    n_experts = w_router.shape[2]
    per_group = n_experts // n_groups
    n_p, n_s = bp * sp, bs * ss
    sk_s = _round_up(past + ss, LANES)
    pad_s = sk_s - past - ss
    row2 = lambda a: a.reshape(1, -1)

    hp = x_prompt.reshape(n_p, d)
    hs = x_sample.reshape(n_s, d)
    outs_p, outs_s = [], []
    for l in range(depth):
        wm, wf = _split_w_in(w_in[l], width, n_heads)
        bfp = jnp.pad(b_f[l], (0, LANES - n_heads)).reshape(1, LANES)
        wo = w_out[l].astype(BF16)
        wr = jnp.pad(jnp.concatenate([w_group[l], w_router[l]], axis=1),
                     ((0, 0), (0, LANES - n_groups - n_experts))).astype(BF16)
        br = jnp.pad(jnp.concatenate([b_group[l], b_router[l]]), (0, LANES - n_groups - n_experts)).reshape(1, LANES)
        g_a, g_f = row2(g_attn[l]), row2(g_ffn[l])
        g_of, g_os = row2(g_out_fox[l]), row2(g_out_sb[l])

        qf, kf, vf, lf, qs, ks, vs = _in_proj(hp, g_a, wm, wf, bfp, n_heads)
        b3 = lambda a: a.reshape(bp, sp, -1)
        cf = _cumsum(b3(lf), min(sp, 512))
        tq = min(sp, ATTN_TILE)
        nf = _attention("fox", b3(qf), b3(kf), b3(vf), g_of, cf, tq=tq, tk=tq, q_off=0)
        ns = _attention("sb", b3(qs), b3(ks), b3(vs), g_os, None, tq=tq, tk=tq, q_off=0)
        outs_p.append((kf, vf, lf, ks, vs))

        qf2, kf2, vf2, lf2, qs2, ks2, vs2 = _in_proj(hs, g_a, wm, wf, bfp, n_heads)
        c3 = lambda a: a.reshape(bs, ss, -1)
        keys = lambda cache, new: jnp.concatenate(
            [cache.reshape(bs, past, width), c3(new), jnp.zeros((bs, pad_s, width), F32)], axis=1)
        cf_all = _cumsum(jnp.concatenate([cache_fox_logf[l].astype(F32), c3(lf2)], axis=1), past + ss)
        cf_all = jnp.pad(cf_all, ((0, 0), (0, pad_s), (0, 0)))
        nf2 = _attention("fox", c3(qf2), keys(cache_fox_k[l], kf2), keys(cache_fox_v[l], vf2), g_of, cf_all,
                         tq=ss, tk=LANES, q_off=past)
        ns2 = _attention("sb", c3(qs2), keys(cache_sb_k[l], ks2), keys(cache_sb_v[l], vs2), g_os, None,
                         tq=ss, tk=LANES, q_off=past)
        outs_s.append((kf2, vf2, lf2, ks2, vs2))

        h_all, hn_all, logits = _merge((nf.reshape(n_p, width), ns.reshape(n_p, width), hp),
                                       (nf2.reshape(n_s, width), ns2.reshape(n_s, width), hs), wo, g_f, wr, br)
        idx, gates, counts = _route(logits, n_groups, per_group)
        dest, plan = _moe_plan(idx, counts, n_experts)
        xs = _dispatch(dest, hn_all)
        rows = _experts(plan, xs, w1[l].astype(BF16), w3[l].astype(BF16), w2[l].astype(BF16))
        last = l == depth - 1
        g_last = row2(g_final)
        hp = _combine(dest, gates, h_all, rows, g_last, n_p, 0, last)
        hs = _combine(dest, gates, h_all, rows, g_last, n_s, n_p, last)

    heads = lambda a, b, s: a.reshape(b, s, n_heads, HEAD_DIM)
    stack = lambda outs, i, f: jnp.stack([f(o[i]) for o in outs])
    hp_f = lambda a: heads(a, bp, sp)
    hs_f = lambda a: heads(a, bs, ss)
    return (hp.reshape(bp, sp, d), hs.reshape(bs, ss, d),
            stack(outs_p, 0, hp_f), stack(outs_p, 1, hp_f), stack(outs_p, 2, lambda a: a.reshape(bp, sp, n_heads)),
            stack(outs_p, 3, hp_f), stack(outs_p, 4, hp_f),
            stack(outs_s, 0, hs_f), stack(outs_s, 1, hs_f), stack(outs_s, 2, lambda a: a.reshape(bs, ss, n_heads)),
            stack(outs_s, 3, hs_f), stack(outs_s, 4, hs_f))
```

```python
import functools
import math

import jax
import jax.numpy as jnp
from jax import lax
from jax.experimental import pallas as pl
from jax.experimental.pallas import tpu as pltpu

F32 = jnp.float32
BF16 = jnp.bfloat16

HEAD_DIM = 128
TOP_K = 2
EPS = 1e-6
LANES = 128
MXU_DIM = 256
NEG_BIG = -1e30
VMEM_LIMIT = 56 * 1024 * 1024
LOG2E = math.log2(math.e)
SCALE = HEAD_DIM ** -0.5
SCALE_LOG2 = SCALE * LOG2E

ROW_TILE = 256
ATTN_TILE = 512
EXPERT_BLOCK = 512
COMBINE_TILE = 128
ISSUE_UNROLL = 8


def _params(*sem):
    return pltpu.CompilerParams(dimension_semantics=sem, vmem_limit_bytes=VMEM_LIMIT)


def _log_sigmoid(x):
    return jnp.minimum(x, 0.0) - jnp.log1p(jnp.exp(-jnp.abs(x)))


def _rms(x, g):
    return x * lax.rsqrt(jnp.mean(x * x, axis=-1, keepdims=True) + EPS) * g


def _in_proj_kernel(x_ref, g_ref, wm_ref, wf_ref, bf_ref, lf_ref, *out_refs, width, n_heads):
    hn = _rms(x_ref[...], g_ref[...]).astype(BF16)
    refs = iter(out_refs)
    for c in range(6):
        res = jnp.dot(hn, wm_ref[:, c * width:(c + 1) * width], preferred_element_type=F32)
        next(refs)[...] = res.astype(BF16)
        if c % 3:
            kv_ref = next(refs)
            kv_ref[...] = res.reshape(kv_ref.shape)
    fl = jnp.dot(hn, wf_ref[...], preferred_element_type=F32) + bf_ref[...]
    lf_ref[...] = _log_sigmoid(fl)[:, :n_heads]


def _in_proj(x, g, wm, wf, bf, n_heads):
    n, d = x.shape
    width = wm.shape[1] // 6
    tm = ROW_TILE
    row = lambda w: pl.BlockSpec((tm, w), lambda i: (i, 0))
    const = lambda a: pl.BlockSpec(a.shape, lambda i: (0, 0), pipeline_mode=pl.Buffered(1))
    per_head = pl.BlockSpec((tm, n_heads, HEAD_DIM), lambda i: (i, 0, 0))
    rows_out = jax.ShapeDtypeStruct((n, width), BF16)
    heads_out = jax.ShapeDtypeStruct((n, n_heads, HEAD_DIM), F32)
    mixer_specs = [row(width)] + [row(width), per_head] * 2
    mixer_shapes = [rows_out] + [rows_out, heads_out] * 2
    return pl.pallas_call(
        functools.partial(_in_proj_kernel, width=width, n_heads=n_heads),
        grid=(n // tm,),
        in_specs=[row(d), const(g), const(wm), const(wf), const(bf)],
        out_specs=[row(n_heads)] + mixer_specs * 2,
        out_shape=[jax.ShapeDtypeStruct((n, n_heads), F32)] + mixer_shapes * 2,
        compiler_params=_params("arbitrary"),
        name="in_proj",
    )(x, g, wm, wf, bf)


def _cumsum_kernel(lf_ref, cf_ref, carry_ref):
    tt = lf_ref.shape[0]

    @pl.when(pl.program_id(1) == 0)
    def _():
        carry_ref[...] = jnp.zeros_like(carry_ref)

    r = lax.broadcasted_iota(jnp.int32, (tt, tt), 0)
    c = lax.broadcasted_iota(jnp.int32, (tt, tt), 1)
    tri = jnp.where(c <= r, 1.0, 0.0).astype(F32)
    cf = jnp.dot(tri, lf_ref[...], precision=lax.Precision.HIGHEST,
                 preferred_element_type=F32) + carry_ref[...]
    cf_ref[...] = cf
    carry_ref[...] = cf[tt - 1:tt, :]


def _cumsum(lf, tt):
    b, t, h = lf.shape
    return pl.pallas_call(
        _cumsum_kernel,
        grid=(b, t // tt),
        in_specs=[pl.BlockSpec((None, tt, h), lambda i, j: (i, j, 0))],
        out_specs=pl.BlockSpec((None, tt, h), lambda i, j: (i, j, 0)),
        out_shape=jax.ShapeDtypeStruct((b, t, h), F32),
        scratch_shapes=[pltpu.VMEM((1, h), F32)],
        compiler_params=_params("arbitrary", "arbitrary"),
        name="cumsum",
    )(lf)


def _split3(x):
    hi = x.astype(BF16).astype(F32)
    r1 = x - hi
    mid = r1.astype(BF16).astype(F32)
    lo = (r1 - mid).astype(BF16).astype(F32)
    return hi, mid, lo


def _bias_block(c, as_query):
    hi, mid, lo = _split3(c * (1.0 / SCALE))
    lane = lax.broadcasted_iota(jnp.int32, (c.shape[0], LANES), 1)
    if as_query:
        parts, base, ones_lo = (hi, mid, lo), 0, 3
    else:
        parts, base, ones_lo = (-hi, -mid, -lo), 3, 0
    blk = jnp.where((lane >= ones_lo) & (lane < ones_lo + 3), 1.0, 0.0)
    for i, p in enumerate(parts):
        blk = jnp.where(lane == base + i, p, blk)
    return blk.astype(BF16)


def _select_col(x, h):
    lane = lax.broadcasted_iota(jnp.int32, x.shape, 1)
    return jnp.sum(jnp.where(lane == h, x, 0.0), axis=1, keepdims=True)


def _fox_kernel(q_ref, k_ref, v_ref, cq_ref, ck_ref, g_ref, o_ref, kaug, *, tq, tk, q_off, prep):
    h = pl.program_id(1)
    qi = pl.program_id(2)
    sk = k_ref.shape[0]

    @pl.when(qi == 0)
    def _():
        for r0 in range(0, sk, prep):
            rows = pl.ds(r0, prep)
            kaug[rows, :HEAD_DIM] = k_ref[rows, :]
            kaug[rows, HEAD_DIM:] = _bias_block(_select_col(ck_ref[rows, :], h), False)

    qa = jnp.concatenate([q_ref[...], _bias_block(_select_col(cq_ref[...], h), True)], axis=1)
    q_lo = q_off + qi * tq
    qpos = q_lo + lax.broadcasted_iota(jnp.int32, (tq, tk), 0)
    kiota = lax.broadcasted_iota(jnp.int32, (tq, tk), 1)

    def chunk(j, carry, masked):
        m, l, acc = carry
        k0 = pl.multiple_of(j * tk, tk)
        s = lax.dot_general(qa, kaug[pl.ds(k0, tk), :], (((1,), (1,)), ((), ())),
                            preferred_element_type=F32) * SCALE_LOG2
        if masked:
            s = jnp.where(k0 + kiota <= qpos, s, NEG_BIG)
        m_new = jnp.maximum(m, jnp.max(s, axis=1, keepdims=True))
        p = jnp.exp2(s - m_new)
        alpha = jnp.exp2(m - m_new)
        l = alpha * l + jnp.sum(p, axis=1, keepdims=True)
        acc = alpha * acc + jnp.dot(p.astype(BF16), v_ref[pl.ds(k0, tk), :], preferred_element_type=F32)
        return m_new, l, acc

    n_full = (q_lo + 1) // tk
    n_all = (q_lo + tq + tk - 1) // tk
    init = (jnp.full((tq, 1), NEG_BIG, F32), jnp.zeros((tq, 1), F32), jnp.zeros((tq, HEAD_DIM), F32))
    carry = lax.fori_loop(0, n_full, lambda j, c: chunk(j, c, False), init)
    m, l, acc = lax.fori_loop(n_full, n_all, lambda j, c: chunk(j, c, True), carry)
    o_ref[...] = _rms(acc / l, g_ref[...]).astype(o_ref.dtype)


def _sb_kernel(q_ref, k_ref, v_ref, g_ref, o_ref, *, tq, tk, q_off):
    qi = pl.program_id(2)
    q = q_ref[...]
    q_lo = q_off + qi * tq
    ts = MXU_DIM if tk % MXU_DIM == 0 else LANES
    n_sub = tk // ts
    qpos = q_lo + lax.broadcasted_iota(jnp.int32, (tq, ts), 0)
    kiota = lax.broadcasted_iota(jnp.int32, (tq, ts), 1)
    r = lax.broadcasted_iota(jnp.int32, (ts, ts), 0)
    c = lax.broadcasted_iota(jnp.int32, (ts, ts), 1)
    neg_from = jnp.where(r >= c, -1.0, 0.0).astype(BF16)
    sign_bit = jnp.uint32(0x80000000)
    high_half = jnp.uint32(0xFFFF0000)

    def chunk(j, carry, masked):
        after, acc = carry
        k0 = pl.multiple_of(j * tk, tk)

        def logits(b):
            z = lax.dot_general(q, k_ref[pl.ds(k0 + b * ts, ts), :], (((1,), (1,)), ((), ())),
                                preferred_element_type=F32) * SCALE_LOG2
            neg_abs = pltpu.bitcast(pltpu.bitcast(z, jnp.uint32) | sign_bit, F32)
            sp = jnp.maximum(z, 0.0) + jnp.log2(1.0 + jnp.exp2(neg_abs))
            valid = None
            if masked:
                valid = k0 + b * ts + kiota < qpos
                sp = jnp.where(valid, sp, 0.0)
            hi = pltpu.bitcast(pltpu.bitcast(sp, jnp.uint32) & high_half, F32)
            lo = (sp - hi).astype(BF16)
            return z, hi.astype(BF16), lo, valid

        def weights(state, after):
            z, hi, lo, valid = state
            sfx = (jnp.dot(hi, neg_from, preferred_element_type=F32)
                   + jnp.dot(lo, neg_from, preferred_element_type=F32)) + after
            a = jnp.exp2(z + sfx)
            if masked:
                a = jnp.where(valid, a, 0.0)
            return sfx[:, 0:1], a.astype(BF16)

        order = list(reversed(range(n_sub)))
        staged, ready = {}, {}
        for step in range(n_sub + 2):
            if step < n_sub:
                staged[step] = logits(order[step])
            if 0 <= step - 1 < n_sub:
                after, ready[step - 1] = weights(staged.pop(step - 1), after)
            if 0 <= step - 2 < n_sub:
                rows = pl.ds(k0 + order[step - 2] * ts, ts)
                acc = acc + jnp.dot(ready.pop(step - 2), v_ref[rows, :], preferred_element_type=F32)
        return after, acc

    n_full = q_lo // tk
    n_all = (q_lo + tq - 1 + tk - 1) // tk
    init = (jnp.zeros((tq, 1), F32), jnp.zeros((tq, HEAD_DIM), F32))
    carry = lax.fori_loop(0, n_all - n_full, lambda i, c: chunk(n_all - 1 - i, c, True), init)
    _, acc = lax.fori_loop(0, n_full, lambda i, c: chunk(n_full - 1 - i, c, False), carry)
    o_ref[...] = _rms(acc, g_ref[...]).astype(o_ref.dtype)


def _attention(kind, q, k, v, g, cf, *, tq, tk, q_off):
    b, sq, width = q.shape
    sk = k.shape[1]
    n_heads = width // HEAD_DIM
    assert sq % tq == 0 and sk % tk == 0 and q_off % tq == 0
    q_spec = pl.BlockSpec((None, tq, HEAD_DIM), lambda i, h, j: (i, j, h))
    kv_spec = pl.BlockSpec((None, sk, HEAD_DIM), lambda i, h, j: (i, 0, h))
    g_spec = pl.BlockSpec((1, HEAD_DIM), lambda i, h, j: (0, h))
    if kind == "fox":
        body = functools.partial(_fox_kernel, tq=tq, tk=tk, q_off=q_off, prep=math.gcd(sk, 512))
        q_blk = q_off // tq
        in_specs = [q_spec, kv_spec, kv_spec,
                    pl.BlockSpec((None, tq, n_heads), lambda i, h, j: (i, j + q_blk, 0)),
                    pl.BlockSpec((None, sk, n_heads), lambda i, h, j: (i, 0, 0)), g_spec]
        args = (q, k, v, cf, cf, g)
        scratch = [pltpu.VMEM((sk, 2 * HEAD_DIM), BF16)]
    else:
        body = functools.partial(_sb_kernel, tq=tq, tk=tk, q_off=q_off)
        in_specs = [q_spec, kv_spec, kv_spec, g_spec]
        args = (q, k, v, g)
        scratch = []
    return pl.pallas_call(
        body,
        grid=(b, n_heads, sq // tq),
        in_specs=in_specs,
        out_specs=q_spec,
        out_shape=jax.ShapeDtypeStruct((b, sq, width), BF16),
        scratch_shapes=scratch,
        compiler_params=_params("arbitrary", "arbitrary", "arbitrary"),
        name=kind,
    )(*args)


def _merge_kernel(nfp_ref, nsp_ref, xp_ref, nfs_ref, nss_ref, xs_ref, wo_ref, g_ref, wr_ref, br_ref,
                  h_ref, hn_ref, lg_ref, *, prompt_blocks):
    def body(nf_ref, ns_ref, x_ref):
        width = nf_ref.shape[1]
        h = x_ref[...] + (jnp.dot(nf_ref[...], wo_ref[:width, :], preferred_element_type=F32)
                          + jnp.dot(ns_ref[...], wo_ref[width:, :], preferred_element_type=F32))
        h_ref[...] = h
        hn = _rms(h, g_ref[...])
        hn_ref[...] = hn
        lg_ref[...] = jnp.dot(hn.astype(BF16), wr_ref[...], preferred_element_type=F32) + br_ref[...]

    is_prompt = pl.program_id(0) < prompt_blocks
    pl.when(is_prompt)(lambda: body(nfp_ref, nsp_ref, xp_ref))
    pl.when(jnp.logical_not(is_prompt))(lambda: body(nfs_ref, nss_ref, xs_ref))


def _merge(prompt, sample, wo, g, wr, br):
    n_p, d = prompt[2].shape
    n_s = sample[2].shape[0]
    width = prompt[0].shape[1]
    tm = ROW_TILE
    pb, sb = n_p // tm, n_s // tm
    n_total = n_p + n_s
    p_row = lambda w: pl.BlockSpec((tm, w), lambda i: (jnp.minimum(i, pb - 1), 0))
    s_row = lambda w: pl.BlockSpec((tm, w), lambda i: (jnp.maximum(i - pb, 0), 0))
    out_row = lambda w: pl.BlockSpec((tm, w), lambda i: (i, 0))
    const = lambda a: pl.BlockSpec(a.shape, lambda i: (0, 0), pipeline_mode=pl.Buffered(1))
    return pl.pallas_call(
        functools.partial(_merge_kernel, prompt_blocks=pb),
        grid=(pb + sb,),
        in_specs=[p_row(width), p_row(width), p_row(d), s_row(width), s_row(width), s_row(d),
                  const(wo), const(g), const(wr), const(br)],
        out_specs=[out_row(d), out_row(d), out_row(LANES)],
        out_shape=[jax.ShapeDtypeStruct((n_total, d), F32), jax.ShapeDtypeStruct((n_total, d), F32),
                   jax.ShapeDtypeStruct((n_total, LANES), F32)],
        compiler_params=_params("arbitrary"),
        name="merge",
    )(*prompt, *sample, wo, g, wr, br)


def _route_kernel(lg_ref, idx_ref, gate_ref, cnt_ref, carry_ref, *, n_groups, per_group):
    tr = lg_ref.shape[0]

    @pl.when(pl.program_id(0) == 0)
    def _():
        carry_ref[...] = jnp.zeros_like(carry_ref)

    x = lg_ref[...]
    lane = lax.broadcasted_iota(jnp.int32, x.shape, 1).astype(F32)
    far = float(LANES)
    rmax = lambda a: jnp.max(a, axis=1, keepdims=True)
    first = lambda hit: jnp.min(jnp.where(hit, lane, far), axis=1, keepdims=True)

    gmask = lane < n_groups
    gl = jnp.where(gmask, x, -jnp.inf)
    gmax = rmax(gl)
    gsel = first(gl == gmax)
    p_g = 1.0 / jnp.sum(jnp.where(gmask, jnp.exp(x - gmax), 0.0), axis=1, keepdims=True)
    lo = n_groups + gsel * per_group
    emask = (lane >= lo) & (lane < lo + per_group)
    el = jnp.where(emask, x, -jnp.inf)
    m1 = rmax(el)
    i1 = first(el == m1)
    el2 = jnp.where(lane == i1, -jnp.inf, el)
    m2 = rmax(el2)
    i2 = first(el2 == m2)
    zsum = jnp.sum(jnp.where(emask, jnp.exp(x - m1), 0.0), axis=1, keepdims=True)
    p1 = 1.0 / zsum
    p2 = jnp.exp(m2 - m1) / zsum
    psum = p1 + p2
    gate1 = p_g * p1 / psum
    gate2 = p_g * p2 / psum
    e1 = i1 - n_groups
    e2 = i2 - n_groups
    hit1 = lane == e1
    hit2 = lane == e2
    onehot = jnp.where(hit1 | hit2, 1.0, 0.0)
    r = lax.broadcasted_iota(jnp.int32, (tr, tr), 0)
    c = lax.broadcasted_iota(jnp.int32, (tr, tr), 1)
    before = jnp.where(c < r, 1.0, 0.0).astype(BF16)
    seen = jnp.dot(before, onehot.astype(BF16), preferred_element_type=F32) + carry_ref[...]
    r1 = jnp.sum(jnp.where(hit1, seen, 0.0), axis=1, keepdims=True)
    r2 = jnp.sum(jnp.where(hit2, seen, 0.0), axis=1, keepdims=True)
    carry_ref[...] = carry_ref[...] + jnp.sum(onehot, axis=0, keepdims=True)
    cnt_ref[...] = carry_ref[...]

    col = lax.broadcasted_iota(jnp.int32, idx_ref.shape, 1)
    idx = jnp.where(col == 0, e1, jnp.where(col == 1, e2, jnp.where(col == 2, r1, r2)))
    idx_ref[...] = idx.astype(jnp.int32)
    gate_ref[...] = jnp.where(col == 0, gate1, jnp.where(col == 1, gate2, 0.0))


def _route(logits, n_groups, per_group):
    n = logits.shape[0]
    tr = ROW_TILE
    narrow = pl.BlockSpec((tr, 8), lambda i: (i, 0))
    return pl.pallas_call(
        functools.partial(_route_kernel, n_groups=n_groups, per_group=per_group),
        grid=(n // tr,),
        in_specs=[pl.BlockSpec((tr, LANES), lambda i: (i, 0))],
        out_specs=[narrow, narrow, pl.BlockSpec((1, LANES), lambda i: (0, 0))],
        out_shape=[jax.ShapeDtypeStruct((n, 8), jnp.int32), jax.ShapeDtypeStruct((n, 8), F32),
                   jax.ShapeDtypeStruct((1, LANES), F32)],
        scratch_shapes=[pltpu.VMEM((1, LANES), F32)],
        compiler_params=_params("arbitrary"),
        name="route",
    )(logits)


def _row_copy(src_ref, src_row, dst_ref, dst_row, sem):
    return pltpu.make_async_copy(src_ref.at[pl.ds(src_row, 1)], dst_ref.at[pl.ds(dst_row, 1)], sem)


def _wait_rows(src_ref, dst_ref, sem, n):
    pltpu.make_async_copy(src_ref.at[pl.ds(0, n)], dst_ref.at[pl.ds(0, n)], sem).wait()


def _dispatch_kernel(dest_ref, hn_ref, xs_ref, sem):
    td = hn_ref.shape[0]

    def start(r, _):
        for k in range(TOP_K):
            _row_copy(hn_ref, r, xs_ref, dest_ref[0, 0, TOP_K * r + k], sem).start()
        return 0

    lax.fori_loop(0, td, start, 0, unroll=ISSUE_UNROLL)
    for _ in range(TOP_K):
        _wait_rows(hn_ref, xs_ref, sem, td)


def _dispatch(dest, hn):
    n, d = hn.shape
    td = ROW_TILE
    return pl.pallas_call(
        _dispatch_kernel,
        grid=(n // td,),
        in_specs=[pl.BlockSpec((1, 1, TOP_K * td), lambda i: (i, 0, 0), memory_space=pltpu.SMEM),
                  pl.BlockSpec((td, d), lambda i: (i, 0))],
        out_specs=pl.BlockSpec(memory_space=pl.ANY),
        out_shape=jax.ShapeDtypeStruct((n * TOP_K, d), F32),
        scratch_shapes=[pltpu.SemaphoreType.DMA(())],
        compiler_params=_params("arbitrary"),
        name="dispatch",
    )(dest.reshape(n // td, 1, TOP_K * td), hn)


def _expert_kernel(blk_ref, exp_ref, lo_ref, hi_ref, first_ref, xs_ref, w1_ref, w3_ref, w2_ref, o_ref):
    w = pl.program_id(0)
    lo, hi = lo_ref[w], hi_ref[w]

    def swiglu():
        rows = lax.broadcasted_iota(jnp.int32, xs_ref.shape, 0)
        x = jnp.where((rows >= lo) & (rows < hi), xs_ref[...], 0.0).astype(BF16)
        a = jnp.dot(x, w1_ref[...], preferred_element_type=F32)
        b = jnp.dot(x, w3_ref[...], preferred_element_type=F32)
        act = (a * jax.nn.sigmoid(a) * b).astype(BF16)
        return jnp.dot(act, w2_ref[...], preferred_element_type=F32)

    @pl.when((hi > lo) & (first_ref[w] == 1))
    def _():
        o_ref[...] = swiglu()

    @pl.when((hi > lo) & (first_ref[w] == 0))
    def _():
        o_ref[...] += swiglu()


def _experts(plan, xs, w1, w3, w2):
    m, d = xs.shape
    de = w1.shape[2]
    blk = EXPERT_BLOCK
    n_items = plan[0].shape[0]
    row_blk = lambda i, rb, ex, lo, hi, fs: (rb[i], 0)
    weight = lambda i, rb, ex, lo, hi, fs: (ex[i], 0, 0)
    return pl.pallas_call(
        _expert_kernel,
        grid_spec=pltpu.PrefetchScalarGridSpec(
            num_scalar_prefetch=5,
            grid=(n_items,),
            in_specs=[pl.BlockSpec((blk, d), row_blk),
                      pl.BlockSpec((None, d, de), weight),
                      pl.BlockSpec((None, d, de), weight),
                      pl.BlockSpec((None, de, d), weight)],
            out_specs=pl.BlockSpec((blk, d), row_blk)),
        out_shape=jax.ShapeDtypeStruct((m, d), F32),
        compiler_params=_params("arbitrary"),
        name="experts",
    )(*plan, xs, w1, w3, w2)


def _combine_kernel(dest_ref, next_ref, gate_ref, h_ref, rows_ref, g_ref, y_ref, buf, sem, *, final_norm):
    i = pl.program_id(0)
    tc = h_ref.shape[0]
    slot = i % 2

    def gather(idx_ref, into):
        def start(r, _):
            for k in range(TOP_K):
                _row_copy(rows_ref, idx_ref[0, 0, TOP_K * r + k], buf.at[into, k], r, sem.at[into]).start()
            return 0

        lax.fori_loop(0, tc, start, 0, unroll=ISSUE_UNROLL)

    pl.when(i == 0)(lambda: gather(dest_ref, 0))
    pl.when(i + 1 < pl.num_programs(0))(lambda: gather(next_ref, 1 - slot))
    for k in range(TOP_K):
        _wait_rows(rows_ref, buf.at[slot, k], sem.at[slot], tc)
    gates = gate_ref[...]
    y = gates[:, 0:1] * buf[slot, 0] + gates[:, 1:2] * buf[slot, 1]
    h = h_ref[...] + y
    y_ref[...] = _rms(h, g_ref[...]) if final_norm else h


def _combine(dest, gates, h, rows, g, n, row_off, final_norm):
    d = h.shape[1]
    tc = COMBINE_TILE
    off = row_off // tc
    last = off + n // tc - 1
    idx_spec = lambda f: pl.BlockSpec((1, 1, TOP_K * tc), lambda i: (f(i), 0, 0), memory_space=pltpu.SMEM)
    dest3 = dest.reshape(-1, 1, TOP_K * tc)
    return pl.pallas_call(
        functools.partial(_combine_kernel, final_norm=final_norm),
        grid=(n // tc,),
        in_specs=[idx_spec(lambda i: i + off), idx_spec(lambda i: jnp.minimum(i + off + 1, last)),
                  pl.BlockSpec((tc, 8), lambda i: (i + off, 0)),
                  pl.BlockSpec((tc, d), lambda i: (i + off, 0)),
                  pl.BlockSpec(memory_space=pl.ANY),
                  pl.BlockSpec((1, d), lambda i: (0, 0))],
        out_specs=pl.BlockSpec((tc, d), lambda i: (i, 0)),
        out_shape=jax.ShapeDtypeStruct((n, d), F32),
        scratch_shapes=[pltpu.VMEM((2, TOP_K, tc, d), F32), pltpu.SemaphoreType.DMA((2,))],
        compiler_params=_params("arbitrary"),
        name="combine",
    )(dest3, dest3, gates, h, rows, g)


def _round_up(a, m):
    return -(-a // m) * m


def _moe_plan(idx, counts, n_experts):
    blk = EXPERT_BLOCK
    n = idx.shape[0]
    i32 = lambda a: a.astype(jnp.int32)
    counts = i32(counts[0, :n_experts])
    end = jnp.cumsum(counts)
    start = end - counts
    dest = i32(start[idx[:, :TOP_K]] + idx[:, TOP_K:2 * TOP_K])
    n_blocks = n * TOP_K // blk
    block_lo = jnp.arange(n_blocks, dtype=jnp.int32) * blk
    e_first = jnp.minimum(i32(jnp.searchsorted(end, block_lo, side="right")), n_experts - 1)
    e_last = jnp.minimum(i32(jnp.searchsorted(end, block_lo + blk - 1, side="right")), n_experts - 1)
    item_end = jnp.cumsum(e_last - e_first + 1)
    item_start = item_end - (e_last - e_first + 1)
    n_items = n_blocks + n_experts - 1
    w = jnp.arange(n_items, dtype=jnp.int32)
    live = w < item_end[-1]
    rb = jnp.minimum(i32(jnp.searchsorted(item_end, w, side="right")), n_blocks - 1)
    ex = jnp.where(live, e_first[rb] + (w - item_start[rb]), e_last[-1])
    lo = jnp.where(live, jnp.clip(start[ex] - block_lo[rb], 0, blk), 0)
    hi = jnp.where(live, jnp.clip(end[ex] - block_lo[rb], 0, blk), 0)
    first = i32(w == item_start[rb])
    return dest, (rb, i32(ex), i32(lo), i32(hi), first)


def _split_w_in(w_in, width, n_heads):
    wm = jnp.concatenate([w_in[:, :3 * width], w_in[:, 3 * width + n_heads:]], axis=1).astype(BF16)
    wf = jnp.pad(w_in[:, 3 * width:3 * width + n_heads], ((0, 0), (0, LANES - n_heads))).astype(BF16)
    return wm, wf


def kernel(x_prompt, x_sample, cache_fox_k, cache_fox_v, cache_fox_logf, cache_sb_k, cache_sb_v,
           w_in, b_f, g_attn, g_out_fox, g_out_sb, w_out, g_ffn, w_group, b_group, w_router, b_router,
           w1, w3, w2, g_final):
    depth = w_in.shape[0]
    bp, sp, d = x_prompt.shape
    bs, ss, _ = x_sample.shape
    past = cache_fox_logf.shape[2]
    n_heads = cache_fox_logf.shape[3]
    width = n_heads * HEAD_DIM
    n_groups = w_group.shape[2]
    n_experts = w_router.shape[2]
    per_group = n_experts // n_groups
    n_p, n_s = bp * sp, bs * ss
    sk_s = _round_up(past + ss, LANES)
    pad_s = sk_s - past - ss
    row2 = lambda a: a.reshape(1, -1)

    hp = x_prompt.reshape(n_p, d)
    hs = x_sample.reshape(n_s, d)
    outs_p, outs_s = [], []
    for l in range(depth):
        wm, wf = _split_w_in(w_in[l], width, n_heads)
        bfp = jnp.pad(b_f[l], (0, LANES - n_heads)).reshape(1, LANES)
        wo = w_out[l].astype(BF16)
        wr = jnp.pad(jnp.concatenate([w_group[l], w_router[l]], axis=1),
                     ((0, 0), (0, LANES - n_groups - n_experts))).astype(BF16)
        br = jnp.pad(jnp.concatenate([b_group[l], b_router[l]]), (0, LANES - n_groups - n_experts)).reshape(1, LANES)
        g_a, g_f = row2(g_attn[l]), row2(g_ffn[l])
        g_of, g_os = row2(g_out_fox[l]), row2(g_out_sb[l])

        lf, qf, kfb, kf, vfb, vf, qs, ksb, ks, vsb, vs = _in_proj(hp, g_a, wm, wf, bfp, n_heads)
        b3 = lambda a: a.reshape(bp, sp, -1)
        cf = _cumsum(b3(lf), min(sp, 512))
        tq = min(sp, ATTN_TILE)
        nf = _attention("fox", b3(qf), b3(kfb), b3(vfb), g_of, cf, tq=tq, tk=tq, q_off=0)
        ns = _attention("sb", b3(qs), b3(ksb), b3(vsb), g_os, None, tq=tq, tk=tq, q_off=0)
        outs_p.append((kf, vf, lf, ks, vs))

        lf2, qf2, kfb2, kf2, vfb2, vf2, qs2, ksb2, ks2, vsb2, vs2 = _in_proj(hs, g_a, wm, wf, bfp, n_heads)
        c3 = lambda a: a.reshape(bs, ss, -1)
        keys = lambda cache, new: jnp.concatenate(
            [cache.reshape(bs, past, width).astype(BF16), c3(new), jnp.zeros((bs, pad_s, width), BF16)], axis=1)
        cf_all = _cumsum(jnp.concatenate([cache_fox_logf[l].astype(F32), c3(lf2)], axis=1), past + ss)
        cf_all = jnp.pad(cf_all, ((0, 0), (0, pad_s), (0, 0)))
        nf2 = _attention("fox", c3(qf2), keys(cache_fox_k[l], kfb2), keys(cache_fox_v[l], vfb2), g_of, cf_all,
                         tq=ss, tk=sk_s, q_off=past)
        ns2 = _attention("sb", c3(qs2), keys(cache_sb_k[l], ksb2), keys(cache_sb_v[l], vsb2), g_os, None,
                         tq=ss, tk=sk_s, q_off=past)
        outs_s.append((kf2, vf2, lf2, ks2, vs2))

        h_all, hn_all, logits = _merge((nf.reshape(n_p, width), ns.reshape(n_p, width), hp),
                                       (nf2.reshape(n_s, width), ns2.reshape(n_s, width), hs), wo, g_f, wr, br)
        idx, gates, counts = _route(logits, n_groups, per_group)
        dest, plan = _moe_plan(idx, counts, n_experts)
        xs = _dispatch(dest, hn_all)
        rows = _experts(plan, xs, w1[l].astype(BF16), w3[l].astype(BF16), w2[l].astype(BF16))
        last = l == depth - 1
        g_last = row2(g_final)
        hp = _combine(dest, gates, h_all, rows, g_last, n_p, 0, last)
        hs = _combine(dest, gates, h_all, rows, g_last, n_s, n_p, last)

    heads = lambda a, b, s: a.reshape(b, s, n_heads, HEAD_DIM)
    stack = lambda outs, i, f: jnp.stack([f(o[i]) for o in outs])
    hp_f = lambda a: heads(a, bp, sp)
    hs_f = lambda a: heads(a, bs, ss)
    return (hp.reshape(bp, sp, d), hs.reshape(bs, ss, d),
            stack(outs_p, 0, hp_f), stack(outs_p, 1, hp_f), stack(outs_p, 2, lambda a: a.reshape(bp, sp, n_heads)),
            stack(outs_p, 3, hp_f), stack(outs_p, 4, hp_f),
            stack(outs_s, 0, hs_f), stack(outs_s, 1, hs_f), stack(outs_s, 2, lambda a: a.reshape(bs, ss, n_heads)),
            stack(outs_s, 3, hs_f), stack(outs_s, 4, hs_f))
```

```python
import functools
import math

import jax
import jax.numpy as jnp
from jax import lax
from jax.experimental import pallas as pl
from jax.experimental.pallas import tpu as pltpu

F32 = jnp.float32
BF16 = jnp.bfloat16

HEAD_DIM = 128
TOP_K = 2
EPS = 1e-6
LANES = 128
MXU_DIM = 256
NEG_BIG = -1e30
VMEM_LIMIT = 56 * 1024 * 1024
LOG2E = math.log2(math.e)
SCALE = HEAD_DIM ** -0.5
SCALE_LOG2 = SCALE * LOG2E

ROW_TILE = 256
ATTN_TILE = 512
FOX_GROUP = 4
SB_GROUP = 2
EXPERT_BLOCK = 512
COMBINE_TILE = 128
ISSUE_UNROLL = 8


def _params(*sem):
    return pltpu.CompilerParams(dimension_semantics=sem, vmem_limit_bytes=VMEM_LIMIT)


def _log_sigmoid(x):
    return jnp.minimum(x, 0.0) - jnp.log1p(jnp.exp(-jnp.abs(x)))


def _rms(x, g):
    return x * lax.rsqrt(jnp.mean(x * x, axis=-1, keepdims=True) + EPS) * g


def _in_proj_kernel(x_ref, g_ref, wm_ref, wf_ref, bf_ref, lf_ref, *out_refs, width, n_heads):
    hn = _rms(x_ref[...], g_ref[...]).astype(BF16)
    refs = iter(out_refs)
    for c in range(6):
        res = jnp.dot(hn, wm_ref[:, c * width:(c + 1) * width], preferred_element_type=F32)
        next(refs)[...] = res.astype(BF16)
        if c % 3:
            kv_ref = next(refs)
            kv_ref[...] = res.reshape(kv_ref.shape)
    fl = jnp.dot(hn, wf_ref[...], preferred_element_type=F32) + bf_ref[...]
    lf_ref[...] = _log_sigmoid(fl)[:, :n_heads]


def _in_proj(x, g, wm, wf, bf, n_heads):
    n, d = x.shape
    width = wm.shape[1] // 6
    tm = ROW_TILE
    row = lambda w: pl.BlockSpec((tm, w), lambda i: (i, 0))
    const = lambda a: pl.BlockSpec(a.shape, lambda i: (0, 0), pipeline_mode=pl.Buffered(1))
    per_head = pl.BlockSpec((tm, n_heads, HEAD_DIM), lambda i: (i, 0, 0))
    rows_out = jax.ShapeDtypeStruct((n, width), BF16)
    heads_out = jax.ShapeDtypeStruct((n, n_heads, HEAD_DIM), F32)
    mixer_specs = [row(width)] + [row(width), per_head] * 2
    mixer_shapes = [rows_out] + [rows_out, heads_out] * 2
    return pl.pallas_call(
        functools.partial(_in_proj_kernel, width=width, n_heads=n_heads),
        grid=(n // tm,),
        in_specs=[row(d), const(g), const(wm), const(wf), const(bf)],
        out_specs=[row(n_heads)] + mixer_specs * 2,
        out_shape=[jax.ShapeDtypeStruct((n, n_heads), F32)] + mixer_shapes * 2,
        compiler_params=_params("arbitrary"),
        name="in_proj",
    )(x, g, wm, wf, bf)


def _cumsum_kernel(lf_ref, cf_ref, carry_ref):
    tt = lf_ref.shape[0]

    @pl.when(pl.program_id(1) == 0)
    def _():
        carry_ref[...] = jnp.zeros_like(carry_ref)

    r = lax.broadcasted_iota(jnp.int32, (tt, tt), 0)
    c = lax.broadcasted_iota(jnp.int32, (tt, tt), 1)
    tri = jnp.where(c <= r, 1.0, 0.0).astype(F32)
    cf = jnp.dot(tri, lf_ref[...], precision=lax.Precision.HIGHEST,
                 preferred_element_type=F32) + carry_ref[...]
    cf_ref[...] = cf
    carry_ref[...] = cf[tt - 1:tt, :]


def _cumsum(lf, tt):
    b, t, h = lf.shape
    return pl.pallas_call(
        _cumsum_kernel,
        grid=(b, t // tt),
        in_specs=[pl.BlockSpec((None, tt, h), lambda i, j: (i, j, 0))],
        out_specs=pl.BlockSpec((None, tt, h), lambda i, j: (i, j, 0)),
        out_shape=jax.ShapeDtypeStruct((b, t, h), F32),
        scratch_shapes=[pltpu.VMEM((1, h), F32)],
        compiler_params=_params("arbitrary", "arbitrary"),
        name="cumsum",
    )(lf)


def _split3(x):
    hi = x.astype(BF16).astype(F32)
    r1 = x - hi
    mid = r1.astype(BF16).astype(F32)
    lo = (r1 - mid).astype(BF16).astype(F32)
    return hi, mid, lo


def _bias_block(c, as_query):
    hi, mid, lo = _split3(c * (1.0 / SCALE))
    lane = lax.broadcasted_iota(jnp.int32, (c.shape[0], LANES), 1)
    if as_query:
        parts, base, ones_lo = (hi, mid, lo), 0, 3
    else:
        parts, base, ones_lo = (-hi, -mid, -lo), 3, 0
    blk = jnp.where((lane >= ones_lo) & (lane < ones_lo + 3), 1.0, 0.0)
    for i, p in enumerate(parts):
        blk = jnp.where(lane == base + i, p, blk)
    return blk.astype(BF16)


def _select_col(x, h):
    lane = lax.broadcasted_iota(jnp.int32, x.shape, 1)
    return jnp.sum(jnp.where(lane == h, x, 0.0), axis=1, keepdims=True)


def _fox_kernel(q_ref, k_ref, v_ref, cq_ref, ck_ref, g_ref, o_ref, kaug, vt, *, tq, tk, q_off, prep, group):
    hg = pl.program_id(1)
    qi = pl.program_id(2)
    sk = k_ref.shape[0]
    hd = lambda g: slice(g * HEAD_DIM, (g + 1) * HEAD_DIM)

    @pl.when(qi == 0)
    def _():
        for g in range(group):
            for r0 in range(0, sk, prep):
                rows = pl.ds(r0, prep)
                kaug[g, rows, :HEAD_DIM] = k_ref[rows, hd(g)]
                kaug[g, rows, HEAD_DIM:] = _bias_block(_select_col(ck_ref[rows, :], hg * group + g), False)
                vt[g, :, rows] = v_ref[rows, hd(g)].astype(F32).T.astype(BF16)

    qa = [jnp.concatenate([q_ref[:, hd(g)], _bias_block(_select_col(cq_ref[...], hg * group + g), True)], axis=1)
          for g in range(group)]
    q_lo = q_off + qi * tq
    kiota = lax.broadcasted_iota(jnp.int32, (tk, tq), 0)
    qpos = q_lo + lax.broadcasted_iota(jnp.int32, (tk, tq), 1)

    def chunk(j, carry, masked):
        k0 = pl.multiple_of(j * tk, tk)

        def logits(g):
            s = lax.dot_general(kaug[g, pl.ds(k0, tk), :], qa[g], (((1,), (1,)), ((), ())),
                                preferred_element_type=F32) * SCALE_LOG2
            if masked:
                s = jnp.where(k0 + kiota <= qpos, s, NEG_BIG)
            return s

        def absorb(g, s):
            m, l, acc = carry[g]
            m_new = jnp.maximum(m, jnp.max(s, axis=0, keepdims=True))
            p = jnp.exp2(s - m_new)
            alpha = jnp.exp2(m - m_new)
            l = alpha * l + jnp.sum(p, axis=0, keepdims=True)
            acc = alpha * acc + jnp.dot(vt[g, :, pl.ds(k0, tk)], p.astype(BF16), preferred_element_type=F32)
            return m_new, l, acc

        out, s_prev = [], None
        for g in range(group + 1):
            s_next = logits(g) if g < group else None
            if g > 0:
                out.append(absorb(g - 1, s_prev))
            s_prev = s_next
        return tuple(out)

    n_full = (q_lo + 1) // tk
    n_all = (q_lo + tq + tk - 1) // tk
    init = tuple((jnp.full((1, tq), NEG_BIG, F32), jnp.zeros((1, tq), F32), jnp.zeros((HEAD_DIM, tq), F32))
                 for _ in range(group))
    carry = lax.fori_loop(0, n_full, lambda j, c: chunk(j, c, False), init)
    carry = lax.fori_loop(n_full, n_all, lambda j, c: chunk(j, c, True), carry)
    for g in range(group):
        m, l, acc = carry[g]
        o_ref[:, hd(g)] = _rms((acc / l).T, g_ref[:, hd(g)]).astype(o_ref.dtype)


def _sb_kernel(q_ref, k_ref, v_ref, g_ref, o_ref, *, tq, tk, q_off, group):
    qi = pl.program_id(2)
    hd = lambda g: slice(g * HEAD_DIM, (g + 1) * HEAD_DIM)
    q = [q_ref[:, hd(g)] for g in range(group)]
    q_lo = q_off + qi * tq
    ts = MXU_DIM if tk % MXU_DIM == 0 else LANES
    n_sub = tk // ts
    qpos = q_lo + lax.broadcasted_iota(jnp.int32, (tq, ts), 0)
    kiota = lax.broadcasted_iota(jnp.int32, (tq, ts), 1)
    r = lax.broadcasted_iota(jnp.int32, (ts, ts), 0)
    c = lax.broadcasted_iota(jnp.int32, (ts, ts), 1)
    neg_from = jnp.where(r >= c, -1.0, 0.0).astype(BF16)
    sign_bit = jnp.uint32(0x80000000)
    high_half = jnp.uint32(0xFFFF0000)

    def chunk(j, carry, masked):
        k0 = pl.multiple_of(j * tk, tk)

        def logits(g, b):
            z = lax.dot_general(q[g], k_ref[pl.ds(k0 + b * ts, ts), hd(g)], (((1,), (1,)), ((), ())),
                                preferred_element_type=F32) * SCALE_LOG2
            neg_abs = pltpu.bitcast(pltpu.bitcast(z, jnp.uint32) | sign_bit, F32)
            sp = jnp.maximum(z, 0.0) + jnp.log2(1.0 + jnp.exp2(neg_abs))
            valid = None
            if masked:
                valid = k0 + b * ts + kiota < qpos
                sp = jnp.where(valid, sp, 0.0)
            hi = pltpu.bitcast(pltpu.bitcast(sp, jnp.uint32) & high_half, F32)
            lo = (sp - hi).astype(BF16)
            return z, hi.astype(BF16), lo, valid

        def weights(state, after):
            z, hi, lo, valid = state
            sfx = (jnp.dot(hi, neg_from, preferred_element_type=F32)
                   + jnp.dot(lo, neg_from, preferred_element_type=F32)) + after
            a = jnp.exp2(z + sfx)
            if masked:
                a = jnp.where(valid, a, 0.0)
            return sfx[:, 0:1], a.astype(BF16)

        items = [(g, b) for b in reversed(range(n_sub)) for g in range(group)]
        after = [carry[g][0] for g in range(group)]
        acc = [carry[g][1] for g in range(group)]
        staged, ready = {}, {}
        n = len(items)
        for step in range(n + 2):
            if step < n:
                staged[step] = logits(*items[step])
            if 0 <= step - 1 < n:
                g, b = items[step - 1]
                after[g], ready[step - 1] = weights(staged.pop(step - 1), after[g])
            if 0 <= step - 2 < n:
                g, b = items[step - 2]
                rows = pl.ds(k0 + b * ts, ts)
                acc[g] = acc[g] + jnp.dot(ready.pop(step - 2), v_ref[rows, hd(g)], preferred_element_type=F32)
        return tuple((after[g], acc[g]) for g in range(group))

    n_full = q_lo // tk
    n_all = (q_lo + tq - 1 + tk - 1) // tk
    init = tuple((jnp.zeros((tq, 1), F32), jnp.zeros((tq, HEAD_DIM), F32)) for _ in range(group))
    carry = lax.fori_loop(0, n_all - n_full, lambda i, c: chunk(n_all - 1 - i, c, True), init)
    carry = lax.fori_loop(0, n_full, lambda i, c: chunk(n_full - 1 - i, c, False), carry)
    for g in range(group):
        o_ref[:, hd(g)] = _rms(carry[g][1], g_ref[:, hd(g)]).astype(o_ref.dtype)


def _attention(kind, q, k, v, g, cf, *, tq, tk, q_off):
    b, sq, width = q.shape
    sk = k.shape[1]
    n_heads = width // HEAD_DIM
    group = FOX_GROUP if kind == "fox" else SB_GROUP
    assert sq % tq == 0 and sk % tk == 0 and q_off % tq == 0 and n_heads % group == 0
    w = group * HEAD_DIM
    q_spec = pl.BlockSpec((None, tq, w), lambda i, h, j: (i, j, h))
    kv_spec = pl.BlockSpec((None, sk, w), lambda i, h, j: (i, 0, h))
    g_spec = pl.BlockSpec((1, w), lambda i, h, j: (0, h))
    if kind == "fox":
        body = functools.partial(_fox_kernel, tq=tq, tk=tk, q_off=q_off, prep=math.gcd(sk, 512), group=group)
        q_blk = q_off // tq
        in_specs = [q_spec, kv_spec, kv_spec,
                    pl.BlockSpec((None, tq, n_heads), lambda i, h, j: (i, j + q_blk, 0)),
                    pl.BlockSpec((None, sk, n_heads), lambda i, h, j: (i, 0, 0)), g_spec]
        args = (q, k, v, cf, cf, g)
        scratch = [pltpu.VMEM((group, sk, 2 * HEAD_DIM), BF16), pltpu.VMEM((group, HEAD_DIM, sk), BF16)]
    else:
        body = functools.partial(_sb_kernel, tq=tq, tk=tk, q_off=q_off, group=group)
        in_specs = [q_spec, kv_spec, kv_spec, g_spec]
        args = (q, k, v, g)
        scratch = []
    return pl.pallas_call(
        body,
        grid=(b, n_heads // group, sq // tq),
        in_specs=in_specs,
        out_specs=q_spec,
        out_shape=jax.ShapeDtypeStruct((b, sq, width), BF16),
        scratch_shapes=scratch,
        compiler_params=_params("arbitrary", "arbitrary", "arbitrary"),
        name=kind,
    )(*args)


def _merge_kernel(nfp_ref, nsp_ref, xp_ref, nfs_ref, nss_ref, xs_ref, wo_ref, g_ref, wr_ref, br_ref,
                  h_ref, hn_ref, lg_ref, *, prompt_blocks):
    def body(nf_ref, ns_ref, x_ref):
        width = nf_ref.shape[1]
        h = x_ref[...] + (jnp.dot(nf_ref[...], wo_ref[:width, :], preferred_element_type=F32)
                          + jnp.dot(ns_ref[...], wo_ref[width:, :], preferred_element_type=F32))
        h_ref[...] = h
        hn = _rms(h, g_ref[...])
        hn_ref[...] = hn
        lg_ref[...] = jnp.dot(hn.astype(BF16), wr_ref[...], preferred_element_type=F32) + br_ref[...]

    is_prompt = pl.program_id(0) < prompt_blocks
    pl.when(is_prompt)(lambda: body(nfp_ref, nsp_ref, xp_ref))
    pl.when(jnp.logical_not(is_prompt))(lambda: body(nfs_ref, nss_ref, xs_ref))


def _merge(prompt, sample, wo, g, wr, br):
    n_p, d = prompt[2].shape
    n_s = sample[2].shape[0]
    width = prompt[0].shape[1]
    tm = ROW_TILE
    pb, sb = n_p // tm, n_s // tm
    n_total = n_p + n_s
    p_row = lambda w: pl.BlockSpec((tm, w), lambda i: (jnp.minimum(i, pb - 1), 0))
    s_row = lambda w: pl.BlockSpec((tm, w), lambda i: (jnp.maximum(i - pb, 0), 0))
    out_row = lambda w: pl.BlockSpec((tm, w), lambda i: (i, 0))
    const = lambda a: pl.BlockSpec(a.shape, lambda i: (0, 0), pipeline_mode=pl.Buffered(1))
    return pl.pallas_call(
        functools.partial(_merge_kernel, prompt_blocks=pb),
        grid=(pb + sb,),
        in_specs=[p_row(width), p_row(width), p_row(d), s_row(width), s_row(width), s_row(d),
                  const(wo), const(g), const(wr), const(br)],
        out_specs=[out_row(d), out_row(d), out_row(LANES)],
        out_shape=[jax.ShapeDtypeStruct((n_total, d), F32), jax.ShapeDtypeStruct((n_total, d), F32),
                   jax.ShapeDtypeStruct((n_total, LANES), F32)],
        compiler_params=_params("arbitrary"),
        name="merge",
    )(*prompt, *sample, wo, g, wr, br)


def _route_kernel(lg_ref, idx_ref, gate_ref, cnt_ref, carry_ref, *, n_groups, per_group):
    tr = lg_ref.shape[0]

    @pl.when(pl.program_id(0) == 0)
    def _():
        carry_ref[...] = jnp.zeros_like(carry_ref)

    x = lg_ref[...]
    lane = lax.broadcasted_iota(jnp.int32, x.shape, 1).astype(F32)
    far = float(LANES)
    rmax = lambda a: jnp.max(a, axis=1, keepdims=True)
    first = lambda hit: jnp.min(jnp.where(hit, lane, far), axis=1, keepdims=True)

    gmask = lane < n_groups
    gl = jnp.where(gmask, x, -jnp.inf)
    gmax = rmax(gl)
    gsel = first(gl == gmax)
    p_g = 1.0 / jnp.sum(jnp.where(gmask, jnp.exp(x - gmax), 0.0), axis=1, keepdims=True)
    lo = n_groups + gsel * per_group
    emask = (lane >= lo) & (lane < lo + per_group)
    el = jnp.where(emask, x, -jnp.inf)
    m1 = rmax(el)
    i1 = first(el == m1)
    el2 = jnp.where(lane == i1, -jnp.inf, el)
    m2 = rmax(el2)
    i2 = first(el2 == m2)
    zsum = jnp.sum(jnp.where(emask, jnp.exp(x - m1), 0.0), axis=1, keepdims=True)
    p1 = 1.0 / zsum
    p2 = jnp.exp(m2 - m1) / zsum
    psum = p1 + p2
    gate1 = p_g * p1 / psum
    gate2 = p_g * p2 / psum
    e1 = i1 - n_groups
    e2 = i2 - n_groups
    hit1 = lane == e1
    hit2 = lane == e2
    onehot = jnp.where(hit1 | hit2, 1.0, 0.0)
    r = lax.broadcasted_iota(jnp.int32, (tr, tr), 0)
    c = lax.broadcasted_iota(jnp.int32, (tr, tr), 1)
    before = jnp.where(c < r, 1.0, 0.0).astype(BF16)
    seen = jnp.dot(before, onehot.astype(BF16), preferred_element_type=F32) + carry_ref[...]
    r1 = jnp.sum(jnp.where(hit1, seen, 0.0), axis=1, keepdims=True)
    r2 = jnp.sum(jnp.where(hit2, seen, 0.0), axis=1, keepdims=True)
    carry_ref[...] = carry_ref[...] + jnp.sum(onehot, axis=0, keepdims=True)
    cnt_ref[...] = carry_ref[...]

    col = lax.broadcasted_iota(jnp.int32, idx_ref.shape, 1)
    idx = jnp.where(col == 0, e1, jnp.where(col == 1, e2, jnp.where(col == 2, r1, r2)))
    idx_ref[...] = idx.astype(jnp.int32)
    gate_ref[...] = jnp.where(col == 0, gate1, jnp.where(col == 1, gate2, 0.0))


def _route(logits, n_groups, per_group):
    n = logits.shape[0]
    tr = ROW_TILE
    narrow = pl.BlockSpec((tr, 8), lambda i: (i, 0))
    return pl.pallas_call(
        functools.partial(_route_kernel, n_groups=n_groups, per_group=per_group),
        grid=(n // tr,),
        in_specs=[pl.BlockSpec((tr, LANES), lambda i: (i, 0))],
        out_specs=[narrow, narrow, pl.BlockSpec((1, LANES), lambda i: (0, 0))],
        out_shape=[jax.ShapeDtypeStruct((n, 8), jnp.int32), jax.ShapeDtypeStruct((n, 8), F32),
                   jax.ShapeDtypeStruct((1, LANES), F32)],
        scratch_shapes=[pltpu.VMEM((1, LANES), F32)],
        compiler_params=_params("arbitrary"),
        name="route",
    )(logits)


def _row_copy(src_ref, src_row, dst_ref, dst_row, sem):
    return pltpu.make_async_copy(src_ref.at[pl.ds(src_row, 1)], dst_ref.at[pl.ds(dst_row, 1)], sem)


def _wait_rows(src_ref, dst_ref, sem, n):
    pltpu.make_async_copy(src_ref.at[pl.ds(0, n)], dst_ref.at[pl.ds(0, n)], sem).wait()


def _dispatch_kernel(dest_ref, hn_ref, xs_ref, sem):
    td = hn_ref.shape[0]

    def start(r, _):
        for k in range(TOP_K):
            _row_copy(hn_ref, r, xs_ref, dest_ref[0, 0, TOP_K * r + k], sem).start()
        return 0

    lax.fori_loop(0, td, start, 0, unroll=ISSUE_UNROLL)
    for _ in range(TOP_K):
        _wait_rows(hn_ref, xs_ref, sem, td)


def _dispatch(dest, hn):
    n, d = hn.shape
    td = ROW_TILE
    return pl.pallas_call(
        _dispatch_kernel,
        grid=(n // td,),
        in_specs=[pl.BlockSpec((1, 1, TOP_K * td), lambda i: (i, 0, 0), memory_space=pltpu.SMEM),
                  pl.BlockSpec((td, d), lambda i: (i, 0))],
        out_specs=pl.BlockSpec(memory_space=pl.ANY),
        out_shape=jax.ShapeDtypeStruct((n * TOP_K, d), F32),
        scratch_shapes=[pltpu.SemaphoreType.DMA(())],
        compiler_params=_params("arbitrary"),
        name="dispatch",
    )(dest.reshape(n // td, 1, TOP_K * td), hn)


def _expert_kernel(blk_ref, exp_ref, lo_ref, hi_ref, first_ref, xs_ref, w1_ref, w3_ref, w2_ref, o_ref):
    w = pl.program_id(0)
    lo, hi = lo_ref[w], hi_ref[w]

    def swiglu():
        rows = lax.broadcasted_iota(jnp.int32, xs_ref.shape, 0)
        x = jnp.where((rows >= lo) & (rows < hi), xs_ref[...], 0.0).astype(BF16)
        a = jnp.dot(x, w1_ref[...], preferred_element_type=F32)
        b = jnp.dot(x, w3_ref[...], preferred_element_type=F32)
        act = (a * jax.nn.sigmoid(a) * b).astype(BF16)
        return jnp.dot(act, w2_ref[...], preferred_element_type=F32)

    @pl.when((hi > lo) & (first_ref[w] == 1))
    def _():
        o_ref[...] = swiglu()

    @pl.when((hi > lo) & (first_ref[w] == 0))
    def _():
        o_ref[...] += swiglu()


def _experts(plan, xs, w1, w3, w2):
    m, d = xs.shape
    de = w1.shape[2]
    blk = EXPERT_BLOCK
    n_items = plan[0].shape[0]
    row_blk = lambda i, rb, ex, lo, hi, fs: (rb[i], 0)
    weight = lambda i, rb, ex, lo, hi, fs: (ex[i], 0, 0)
    return pl.pallas_call(
        _expert_kernel,
        grid_spec=pltpu.PrefetchScalarGridSpec(
            num_scalar_prefetch=5,
            grid=(n_items,),
            in_specs=[pl.BlockSpec((blk, d), row_blk),
                      pl.BlockSpec((None, d, de), weight),
                      pl.BlockSpec((None, d, de), weight),
                      pl.BlockSpec((None, de, d), weight)],
            out_specs=pl.BlockSpec((blk, d), row_blk)),
        out_shape=jax.ShapeDtypeStruct((m, d), F32),
        compiler_params=_params("arbitrary"),
        name="experts",
    )(*plan, xs, w1, w3, w2)


def _combine_kernel(dest_ref, next_ref, gate_ref, h_ref, rows_ref, g_ref, y_ref, buf, sem, *, final_norm):
    i = pl.program_id(0)
    tc = h_ref.shape[0]
    slot = i % 2

    def gather(idx_ref, into):
        def start(r, _):
            for k in range(TOP_K):
                _row_copy(rows_ref, idx_ref[0, 0, TOP_K * r + k], buf.at[into, k], r, sem.at[into]).start()
            return 0

        lax.fori_loop(0, tc, start, 0, unroll=ISSUE_UNROLL)

    pl.when(i == 0)(lambda: gather(dest_ref, 0))
    pl.when(i + 1 < pl.num_programs(0))(lambda: gather(next_ref, 1 - slot))
    for k in range(TOP_K):
        _wait_rows(rows_ref, buf.at[slot, k], sem.at[slot], tc)
    gates = gate_ref[...]
    y = gates[:, 0:1] * buf[slot, 0] + gates[:, 1:2] * buf[slot, 1]
    h = h_ref[...] + y
    y_ref[...] = _rms(h, g_ref[...]) if final_norm else h


def _combine(dest, gates, h, rows, g, n, row_off, final_norm):
    d = h.shape[1]
    tc = COMBINE_TILE
    off = row_off // tc
    last = off + n // tc - 1
    idx_spec = lambda f: pl.BlockSpec((1, 1, TOP_K * tc), lambda i: (f(i), 0, 0), memory_space=pltpu.SMEM)
    dest3 = dest.reshape(-1, 1, TOP_K * tc)
    return pl.pallas_call(
        functools.partial(_combine_kernel, final_norm=final_norm),
        grid=(n // tc,),
        in_specs=[idx_spec(lambda i: i + off), idx_spec(lambda i: jnp.minimum(i + off + 1, last)),
                  pl.BlockSpec((tc, 8), lambda i: (i + off, 0)),
                  pl.BlockSpec((tc, d), lambda i: (i + off, 0)),
                  pl.BlockSpec(memory_space=pl.ANY),
                  pl.BlockSpec((1, d), lambda i: (0, 0))],
        out_specs=pl.BlockSpec((tc, d), lambda i: (i, 0)),
        out_shape=jax.ShapeDtypeStruct((n, d), F32),
        scratch_shapes=[pltpu.VMEM((2, TOP_K, tc, d), F32), pltpu.SemaphoreType.DMA((2,))],
        compiler_params=_params("arbitrary"),
        name="combine",
    )(dest3, dest3, gates, h, rows, g)


def _round_up(a, m):
    return -(-a // m) * m


def _moe_plan(idx, counts, n_experts):
    blk = EXPERT_BLOCK
    n = idx.shape[0]
    i32 = lambda a: a.astype(jnp.int32)
    counts = i32(counts[0, :n_experts])
    end = jnp.cumsum(counts)
    start = end - counts
    dest = i32(start[idx[:, :TOP_K]] + idx[:, TOP_K:2 * TOP_K])
    n_blocks = n * TOP_K // blk
    block_lo = jnp.arange(n_blocks, dtype=jnp.int32) * blk
    e_first = jnp.minimum(i32(jnp.searchsorted(end, block_lo, side="right")), n_experts - 1)
    e_last = jnp.minimum(i32(jnp.searchsorted(end, block_lo + blk - 1, side="right")), n_experts - 1)
    item_end = jnp.cumsum(e_last - e_first + 1)
    item_start = item_end - (e_last - e_first + 1)
    n_items = n_blocks + n_experts - 1
    w = jnp.arange(n_items, dtype=jnp.int32)
    live = w < item_end[-1]
    rb = jnp.minimum(i32(jnp.searchsorted(item_end, w, side="right")), n_blocks - 1)
    ex = jnp.where(live, e_first[rb] + (w - item_start[rb]), e_last[-1])
    lo = jnp.where(live, jnp.clip(start[ex] - block_lo[rb], 0, blk), 0)
    hi = jnp.where(live, jnp.clip(end[ex] - block_lo[rb], 0, blk), 0)
    first = i32(w == item_start[rb])
    return dest, (rb, i32(ex), i32(lo), i32(hi), first)


def _split_w_in(w_in, width, n_heads):
    wm = jnp.concatenate([w_in[:, :3 * width], w_in[:, 3 * width + n_heads:]], axis=1).astype(BF16)
    wf = jnp.pad(w_in[:, 3 * width:3 * width + n_heads], ((0, 0), (0, LANES - n_heads))).astype(BF16)
    return wm, wf


def kernel(x_prompt, x_sample, cache_fox_k, cache_fox_v, cache_fox_logf, cache_sb_k, cache_sb_v,
           w_in, b_f, g_attn, g_out_fox, g_out_sb, w_out, g_ffn, w_group, b_group, w_router, b_router,
           w1, w3, w2, g_final):
    depth = w_in.shape[0]
    bp, sp, d = x_prompt.shape
    bs, ss, _ = x_sample.shape
    past = cache_fox_logf.shape[2]
    n_heads = cache_fox_logf.shape[3]
    width = n_heads * HEAD_DIM
    n_groups = w_group.shape[2]
    n_experts = w_router.shape[2]
    per_group = n_experts // n_groups
    n_p, n_s = bp * sp, bs * ss
    sk_s = _round_up(past + ss, LANES)
    pad_s = sk_s - past - ss
    row2 = lambda a: a.reshape(1, -1)

    hp = x_prompt.reshape(n_p, d)
    hs = x_sample.reshape(n_s, d)
    outs_p, outs_s = [], []
    for l in range(depth):
        wm, wf = _split_w_in(w_in[l], width, n_heads)
        bfp = jnp.pad(b_f[l], (0, LANES - n_heads)).reshape(1, LANES)
        wo = w_out[l].astype(BF16)
        wr = jnp.pad(jnp.concatenate([w_group[l], w_router[l]], axis=1),
                     ((0, 0), (0, LANES - n_groups - n_experts))).astype(BF16)
        br = jnp.pad(jnp.concatenate([b_group[l], b_router[l]]), (0, LANES - n_groups - n_experts)).reshape(1, LANES)
        g_a, g_f = row2(g_attn[l]), row2(g_ffn[l])
        g_of, g_os = row2(g_out_fox[l]), row2(g_out_sb[l])

        lf, qf, kfb, kf, vfb, vf, qs, ksb, ks, vsb, vs = _in_proj(hp, g_a, wm, wf, bfp, n_heads)
        b3 = lambda a: a.reshape(bp, sp, -1)
        cf = _cumsum(b3(lf), min(sp, 512))
        tq = min(sp, ATTN_TILE)
        nf = _attention("fox", b3(qf), b3(kfb), b3(vfb), g_of, cf, tq=tq, tk=tq, q_off=0)
        ns = _attention("sb", b3(qs), b3(ksb), b3(vsb), g_os, None, tq=tq, tk=tq, q_off=0)
        outs_p.append((kf, vf, lf, ks, vs))

        lf2, qf2, kfb2, kf2, vfb2, vf2, qs2, ksb2, ks2, vsb2, vs2 = _in_proj(hs, g_a, wm, wf, bfp, n_heads)
        c3 = lambda a: a.reshape(bs, ss, -1)
        keys = lambda cache, new: jnp.concatenate(
            [cache.reshape(bs, past, width).astype(BF16), c3(new), jnp.zeros((bs, pad_s, width), BF16)], axis=1)
        cf_all = _cumsum(jnp.concatenate([cache_fox_logf[l].astype(F32), c3(lf2)], axis=1), past + ss)
        cf_all = jnp.pad(cf_all, ((0, 0), (0, pad_s), (0, 0)))
        nf2 = _attention("fox", c3(qf2), keys(cache_fox_k[l], kfb2), keys(cache_fox_v[l], vfb2), g_of, cf_all,
                         tq=ss, tk=sk_s, q_off=past)
        ns2 = _attention("sb", c3(qs2), keys(cache_sb_k[l], ksb2), keys(cache_sb_v[l], vsb2), g_os, None,
                         tq=ss, tk=sk_s, q_off=past)
        outs_s.append((kf2, vf2, lf2, ks2, vs2))

        h_all, hn_all, logits = _merge((nf.reshape(n_p, width), ns.reshape(n_p, width), hp),
                                       (nf2.reshape(n_s, width), ns2.reshape(n_s, width), hs), wo, g_f, wr, br)
        idx, gates, counts = _route(logits, n_groups, per_group)
        dest, plan = _moe_plan(idx, counts, n_experts)
        xs = _dispatch(dest, hn_all)
        rows = _experts(plan, xs, w1[l].astype(BF16), w3[l].astype(BF16), w2[l].astype(BF16))
        last = l == depth - 1
        g_last = row2(g_final)
        hp = _combine(dest, gates, h_all, rows, g_last, n_p, 0, last)
        hs = _combine(dest, gates, h_all, rows, g_last, n_s, n_p, last)

    heads = lambda a, b, s: a.reshape(b, s, n_heads, HEAD_DIM)
    stack = lambda outs, i, f: jnp.stack([f(o[i]) for o in outs])
    hp_f = lambda a: heads(a, bp, sp)
    hs_f = lambda a: heads(a, bs, ss)
    return (hp.reshape(bp, sp, d), hs.reshape(bs, ss, d),
            stack(outs_p, 0, hp_f), stack(outs_p, 1, hp_f), stack(outs_p, 2, lambda a: a.reshape(bp, sp, n_heads)),
            stack(outs_p, 3, hp_f), stack(outs_p, 4, hp_f),
            stack(outs_s, 0, hs_f), stack(outs_s, 1, hs_f), stack(outs_s, 2, lambda a: a.reshape(bs, ss, n_heads)),
            stack(outs_s, 3, hs_f), stack(outs_s, 4, hs_f))
```

```python
import functools
import math

import jax
import jax.numpy as jnp
from jax import lax
from jax.experimental import pallas as pl
from jax.experimental.pallas import tpu as pltpu

F32 = jnp.float32
BF16 = jnp.bfloat16

HEAD_DIM = 128
TOP_K = 2
EPS = 1e-6
LANES = 128
MXU_DIM = 256
NEG_BIG = -1e30
VMEM_LIMIT = 56 * 1024 * 1024
LOG2E = math.log2(math.e)
SCALE = HEAD_DIM ** -0.5
SCALE_LOG2 = SCALE * LOG2E

ROW_TILE = 256
ATTN_TILE = 512
FOX_GROUP = 4
SB_GROUP = 2
EXPERT_BLOCK = 512
COMBINE_TILE = 128
ISSUE_UNROLL = 8


def _params(*sem):
    return pltpu.CompilerParams(dimension_semantics=sem, vmem_limit_bytes=VMEM_LIMIT)


def _log_sigmoid(x):
    return jnp.minimum(x, 0.0) - jnp.log1p(jnp.exp(-jnp.abs(x)))


def _rms(x, g):
    return x * lax.rsqrt(jnp.mean(x * x, axis=-1, keepdims=True) + EPS) * g


def _in_proj_kernel(x_ref, g_ref, wm_ref, wf_ref, bf_ref, lf_ref, *out_refs, width, n_heads):
    hn = _rms(x_ref[...], g_ref[...]).astype(BF16)
    refs = iter(out_refs)
    for c in range(6):
        res = jnp.dot(hn, wm_ref[:, c * width:(c + 1) * width], preferred_element_type=F32)
        next(refs)[...] = (res if c % 3 else res * SCALE_LOG2).astype(BF16)
        if c % 3:
            kv_ref = next(refs)
            kv_ref[...] = res.reshape(kv_ref.shape)
    fl = jnp.dot(hn, wf_ref[...], preferred_element_type=F32) + bf_ref[...]
    lf_ref[...] = _log_sigmoid(fl)[:, :n_heads]


def _in_proj(x, g, wm, wf, bf, n_heads):
    n, d = x.shape
    width = wm.shape[1] // 6
    tm = ROW_TILE
    row = lambda w: pl.BlockSpec((tm, w), lambda i: (i, 0))
    const = lambda a: pl.BlockSpec(a.shape, lambda i: (0, 0), pipeline_mode=pl.Buffered(1))
    per_head = pl.BlockSpec((tm, n_heads, HEAD_DIM), lambda i: (i, 0, 0))
    rows_out = jax.ShapeDtypeStruct((n, width), BF16)
    heads_out = jax.ShapeDtypeStruct((n, n_heads, HEAD_DIM), F32)
    mixer_specs = [row(width)] + [row(width), per_head] * 2
    mixer_shapes = [rows_out] + [rows_out, heads_out] * 2
    return pl.pallas_call(
        functools.partial(_in_proj_kernel, width=width, n_heads=n_heads),
        grid=(n // tm,),
        in_specs=[row(d), const(g), const(wm), const(wf), const(bf)],
        out_specs=[row(n_heads)] + mixer_specs * 2,
        out_shape=[jax.ShapeDtypeStruct((n, n_heads), F32)] + mixer_shapes * 2,
        compiler_params=_params("arbitrary"),
        name="in_proj",
    )(x, g, wm, wf, bf)


def _cumsum_kernel(lf_ref, cf_ref, carry_ref):
    tt = lf_ref.shape[0]

    @pl.when(pl.program_id(1) == 0)
    def _():
        carry_ref[...] = jnp.zeros_like(carry_ref)

    r = lax.broadcasted_iota(jnp.int32, (tt, tt), 0)
    c = lax.broadcasted_iota(jnp.int32, (tt, tt), 1)
    tri = jnp.where(c <= r, 1.0, 0.0).astype(F32)
    cf = jnp.dot(tri, lf_ref[...], precision=lax.Precision.HIGHEST,
                 preferred_element_type=F32) + carry_ref[...]
    cf_ref[...] = cf
    carry_ref[...] = cf[tt - 1:tt, :]


def _cumsum(lf, tt):
    b, t, h = lf.shape
    return pl.pallas_call(
        _cumsum_kernel,
        grid=(b, t // tt),
        in_specs=[pl.BlockSpec((None, tt, h), lambda i, j: (i, j, 0))],
        out_specs=pl.BlockSpec((None, tt, h), lambda i, j: (i, j, 0)),
        out_shape=jax.ShapeDtypeStruct((b, t, h), F32),
        scratch_shapes=[pltpu.VMEM((1, h), F32)],
        compiler_params=_params("arbitrary", "arbitrary"),
        name="cumsum",
    )(lf)


def _split3(x):
    hi = x.astype(BF16).astype(F32)
    r1 = x - hi
    mid = r1.astype(BF16).astype(F32)
    lo = (r1 - mid).astype(BF16).astype(F32)
    return hi, mid, lo


def _bias_block(c, as_query):
    hi, mid, lo = _split3(c * LOG2E)
    lane = lax.broadcasted_iota(jnp.int32, (c.shape[0], LANES), 1)
    if as_query:
        parts, base, ones_lo = (hi, mid, lo), 0, 3
    else:
        parts, base, ones_lo = (-hi, -mid, -lo), 3, 0
    blk = jnp.where((lane >= ones_lo) & (lane < ones_lo + 3), 1.0, 0.0)
    for i, p in enumerate(parts):
        blk = jnp.where(lane == base + i, p, blk)
    return blk.astype(BF16)


def _select_col(x, h):
    lane = lax.broadcasted_iota(jnp.int32, x.shape, 1)
    return jnp.sum(jnp.where(lane == h, x, 0.0), axis=1, keepdims=True)


def _fox_kernel(q_ref, k_ref, v_ref, cq_ref, ck_ref, g_ref, o_ref, kaug, vt, *, tq, tk, q_off, prep, group, wide):
    hg = pl.program_id(1)
    qi = pl.program_id(2)
    sk = k_ref.shape[0]
    hd = lambda g: slice(g * HEAD_DIM, (g + 1) * HEAD_DIM)

    @pl.when(qi == 0)
    def _():
        for g in range(group):
            for r0 in range(0, sk, prep):
                rows = pl.ds(r0, prep)
                kaug[g, rows, :HEAD_DIM] = k_ref[rows, hd(g)]
                kaug[g, rows, HEAD_DIM:] = _bias_block(_select_col(ck_ref[rows, :], hg * group + g), False)
                vt[g, :, rows] = v_ref[rows, hd(g)].astype(F32).T.astype(BF16)

    qa = [jnp.concatenate([q_ref[:, hd(g)], _bias_block(_select_col(cq_ref[...], hg * group + g), True)], axis=1)
          for g in range(group)]
    q_lo = q_off + qi * tq
    kiota = lax.broadcasted_iota(jnp.int32, (tk, tq), 0)
    qpos = q_lo + lax.broadcasted_iota(jnp.int32, (tk, tq), 1)

    def chunk(j, carry, width, masked=False):
        k0 = pl.multiple_of(j * width, width)

        def logits(g):
            s = lax.dot_general(kaug[g, pl.ds(k0, width), :], qa[g], (((1,), (1,)), ((), ())),
                                preferred_element_type=F32)
            if masked:
                s = jnp.where(k0 + kiota <= qpos, s, NEG_BIG)
            return s

        def absorb(g, s):
            m, l, acc = carry[g]
            m_new = jnp.maximum(m, jnp.max(s, axis=0, keepdims=True))
            p = jnp.exp2(s - m_new)
            alpha = jnp.exp2(m - m_new)
            l = alpha * l + jnp.sum(p, axis=0, keepdims=True)
            acc = alpha * acc + jnp.dot(vt[g, :, pl.ds(k0, width)], p.astype(BF16), preferred_element_type=F32)
            return m_new, l, acc

        out, s_prev = [], None
        for g in range(group + 1):
            s_next = logits(g) if g < group else None
            if g > 0:
                out.append(absorb(g - 1, s_prev))
            s_prev = s_next
        return tuple(out)

    n_full = (q_lo + 1) // tk
    n_all = (q_lo + tq + tk - 1) // tk
    carry = tuple((jnp.full((1, tq), NEG_BIG, F32), jnp.zeros((1, tq), F32), jnp.zeros((HEAD_DIM, tq), F32))
                  for _ in range(group))
    n_narrow = n_full
    if wide:
        carry = lax.fori_loop(0, n_full // 2, lambda j, c: chunk(j, c, 2 * tk), carry)
        n_narrow = n_full % 2
    carry = lax.fori_loop(n_full - n_narrow, n_full, lambda j, c: chunk(j, c, tk), carry)
    carry = lax.fori_loop(n_full, n_all, lambda j, c: chunk(j, c, tk, True), carry)
    for g in range(group):
        m, l, acc = carry[g]
        o_ref[:, hd(g)] = _rms((acc / l).T, g_ref[:, hd(g)]).astype(o_ref.dtype)


def _sb_kernel(q_ref, k_ref, v_ref, g_ref, o_ref, *, tq, tk, q_off, group, wide):
    qi = pl.program_id(2)
    hd = lambda g: slice(g * HEAD_DIM, (g + 1) * HEAD_DIM)
    q = [q_ref[:, hd(g)] for g in range(group)]
    q_lo = q_off + qi * tq
    ts = MXU_DIM if tk % MXU_DIM == 0 else LANES
    qpos = q_lo + lax.broadcasted_iota(jnp.int32, (tq, ts), 0)
    kiota = lax.broadcasted_iota(jnp.int32, (tq, ts), 1)
    r = lax.broadcasted_iota(jnp.int32, (ts, ts), 0)
    c = lax.broadcasted_iota(jnp.int32, (ts, ts), 1)
    neg_from = jnp.where(r >= c, -1.0, 0.0).astype(BF16)
    sign_bit = jnp.uint32(0x80000000)

    def chunk(j, carry, width, masked=False):
        k0 = pl.multiple_of(j * width, width)
        n_sub = width // ts

        def logits(g, b):
            z = lax.dot_general(q[g], k_ref[pl.ds(k0 + b * ts, ts), hd(g)], (((1,), (1,)), ((), ())),
                                preferred_element_type=F32)
            neg_abs = pltpu.bitcast(pltpu.bitcast(z, jnp.uint32) | sign_bit, F32)
            sp = jnp.maximum(z, 0.0) + jnp.log2(1.0 + jnp.exp2(neg_abs))
            valid = None
            if masked:
                valid = k0 + b * ts + kiota < qpos
                sp = jnp.where(valid, sp, 0.0)
            hi = sp.astype(BF16)
            lo = (sp - hi.astype(F32)).astype(BF16)
            return z, hi, lo, valid

        def weights(state, after):
            z, hi, lo, valid = state
            sfx = (jnp.dot(hi, neg_from, preferred_element_type=F32)
                   + jnp.dot(lo, neg_from, preferred_element_type=F32)) + after
            a = jnp.exp2(z + sfx)
            if masked:
                a = jnp.where(valid, a, 0.0)
            return sfx[:, 0:1], a.astype(BF16)

        items = [(g, b) for b in reversed(range(n_sub)) for g in range(group)]
        after = [carry[g][0] for g in range(group)]
        acc = [carry[g][1] for g in range(group)]
        staged, ready = {}, {}
        n = len(items)
        for step in range(n + 2):
            if step < n:
                staged[step] = logits(*items[step])
            if 0 <= step - 1 < n:
                g, b = items[step - 1]
                after[g], ready[step - 1] = weights(staged.pop(step - 1), after[g])
            if 0 <= step - 2 < n:
                g, b = items[step - 2]
                rows = pl.ds(k0 + b * ts, ts)
                acc[g] = acc[g] + jnp.dot(ready.pop(step - 2), v_ref[rows, hd(g)], preferred_element_type=F32)
        return tuple((after[g], acc[g]) for g in range(group))

    n_full = q_lo // tk
    n_all = (q_lo + tq - 1 + tk - 1) // tk
    carry = tuple((jnp.zeros((tq, 1), F32), jnp.zeros((tq, HEAD_DIM), F32)) for _ in range(group))
    carry = lax.fori_loop(0, n_all - n_full, lambda i, c: chunk(n_all - 1 - i, c, tk, True), carry)
    n_narrow = n_full
    if wide:
        n_narrow = n_full % 2
    carry = lax.fori_loop(0, n_narrow, lambda i, c: chunk(n_full - 1 - i, c, tk), carry)
    if wide:
        carry = lax.fori_loop(0, n_full // 2, lambda i, c: chunk(n_full // 2 - 1 - i, c, 2 * tk), carry)
    for g in range(group):
        o_ref[:, hd(g)] = _rms(carry[g][1], g_ref[:, hd(g)]).astype(o_ref.dtype)


def _attention(kind, q, k, v, g, cf, *, tq, tk, q_off):
    b, sq, width = q.shape
    sk = k.shape[1]
    n_heads = width // HEAD_DIM
    group = FOX_GROUP if kind == "fox" else SB_GROUP
    assert sq % tq == 0 and sk % tk == 0 and q_off % tq == 0 and n_heads % group == 0
    w = group * HEAD_DIM
    wide = sk % (2 * tk) == 0
    q_spec = pl.BlockSpec((None, tq, w), lambda i, h, j: (i, j, h))
    kv_spec = pl.BlockSpec((None, sk, w), lambda i, h, j: (i, 0, h))
    g_spec = pl.BlockSpec((1, w), lambda i, h, j: (0, h))
    if kind == "fox":
        body = functools.partial(_fox_kernel, tq=tq, tk=tk, q_off=q_off, prep=math.gcd(sk, 512), group=group,
                                 wide=wide)
        q_blk = q_off // tq
        in_specs = [q_spec, kv_spec, kv_spec,
                    pl.BlockSpec((None, tq, n_heads), lambda i, h, j: (i, j + q_blk, 0)),
                    pl.BlockSpec((None, sk, n_heads), lambda i, h, j: (i, 0, 0)), g_spec]
        args = (q, k, v, cf, cf, g)
        scratch = [pltpu.VMEM((group, sk, 2 * HEAD_DIM), BF16), pltpu.VMEM((group, HEAD_DIM, sk), BF16)]
    else:
        body = functools.partial(_sb_kernel, tq=tq, tk=tk, q_off=q_off, group=group, wide=wide)
        in_specs = [q_spec, kv_spec, kv_spec, g_spec]
        args = (q, k, v, g)
        scratch = []
    return pl.pallas_call(
        body,
        grid=(b, n_heads // group, sq // tq),
        in_specs=in_specs,
        out_specs=q_spec,
        out_shape=jax.ShapeDtypeStruct((b, sq, width), BF16),
        scratch_shapes=scratch,
        compiler_params=_params("arbitrary", "arbitrary", "arbitrary"),
        name=kind,
    )(*args)


def _merge_kernel(nfp_ref, nsp_ref, xp_ref, nfs_ref, nss_ref, xs_ref, wo_ref, g_ref, wr_ref, br_ref,
                  h_ref, hn_ref, lg_ref, *, prompt_blocks):
    def body(nf_ref, ns_ref, x_ref):
        width = nf_ref.shape[1]
        h = x_ref[...] + (jnp.dot(nf_ref[...], wo_ref[:width, :], preferred_element_type=F32)
                          + jnp.dot(ns_ref[...], wo_ref[width:, :], preferred_element_type=F32))
        h_ref[...] = h
        hn = _rms(h, g_ref[...])
        hn_ref[...] = hn
        lg_ref[...] = jnp.dot(hn.astype(BF16), wr_ref[...], preferred_element_type=F32) + br_ref[...]

    is_prompt = pl.program_id(0) < prompt_blocks
    pl.when(is_prompt)(lambda: body(nfp_ref, nsp_ref, xp_ref))
    pl.when(jnp.logical_not(is_prompt))(lambda: body(nfs_ref, nss_ref, xs_ref))


def _merge(prompt, sample, wo, g, wr, br):
    n_p, d = prompt[2].shape
    n_s = sample[2].shape[0]
    width = prompt[0].shape[1]
    tm = ROW_TILE
    pb, sb = n_p // tm, n_s // tm
    n_total = n_p + n_s
    p_row = lambda w: pl.BlockSpec((tm, w), lambda i: (jnp.minimum(i, pb - 1), 0))
    s_row = lambda w: pl.BlockSpec((tm, w), lambda i: (jnp.maximum(i - pb, 0), 0))
    out_row = lambda w: pl.BlockSpec((tm, w), lambda i: (i, 0))
    const = lambda a: pl.BlockSpec(a.shape, lambda i: (0, 0), pipeline_mode=pl.Buffered(1))
    return pl.pallas_call(
        functools.partial(_merge_kernel, prompt_blocks=pb),
        grid=(pb + sb,),
        in_specs=[p_row(width), p_row(width), p_row(d), s_row(width), s_row(width), s_row(d),
                  const(wo), const(g), const(wr), const(br)],
        out_specs=[out_row(d), out_row(d), out_row(LANES)],
        out_shape=[jax.ShapeDtypeStruct((n_total, d), F32), jax.ShapeDtypeStruct((n_total, d), F32),
                   jax.ShapeDtypeStruct((n_total, LANES), F32)],
        compiler_params=_params("arbitrary"),
        name="merge",
    )(*prompt, *sample, wo, g, wr, br)


def _route_kernel(lg_ref, idx_ref, gate_ref, cnt_ref, carry_ref, *, n_groups, per_group):
    tr = lg_ref.shape[0]

    @pl.when(pl.program_id(0) == 0)
    def _():
        carry_ref[...] = jnp.zeros_like(carry_ref)

    x = lg_ref[...]
    lane = lax.broadcasted_iota(jnp.int32, x.shape, 1).astype(F32)
    far = float(LANES)
    rmax = lambda a: jnp.max(a, axis=1, keepdims=True)
    first = lambda hit: jnp.min(jnp.where(hit, lane, far), axis=1, keepdims=True)

    gmask = lane < n_groups
    gl = jnp.where(gmask, x, -jnp.inf)
    gmax = rmax(gl)
    gsel = first(gl == gmax)
    p_g = 1.0 / jnp.sum(jnp.where(gmask, jnp.exp(x - gmax), 0.0), axis=1, keepdims=True)
    lo = n_groups + gsel * per_group
    emask = (lane >= lo) & (lane < lo + per_group)
    el = jnp.where(emask, x, -jnp.inf)
    m1 = rmax(el)
    i1 = first(el == m1)
    el2 = jnp.where(lane == i1, -jnp.inf, el)
    m2 = rmax(el2)
    i2 = first(el2 == m2)
    zsum = jnp.sum(jnp.where(emask, jnp.exp(x - m1), 0.0), axis=1, keepdims=True)
    p1 = 1.0 / zsum
    p2 = jnp.exp(m2 - m1) / zsum
    psum = p1 + p2
    gate1 = p_g * p1 / psum
    gate2 = p_g * p2 / psum
    e1 = i1 - n_groups
    e2 = i2 - n_groups
    hit1 = lane == e1
    hit2 = lane == e2
    onehot = jnp.where(hit1 | hit2, 1.0, 0.0)
    r = lax.broadcasted_iota(jnp.int32, (tr, tr), 0)
    c = lax.broadcasted_iota(jnp.int32, (tr, tr), 1)
    before = jnp.where(c < r, 1.0, 0.0).astype(BF16)
    seen = jnp.dot(before, onehot.astype(BF16), preferred_element_type=F32) + carry_ref[...]
    r1 = jnp.sum(jnp.where(hit1, seen, 0.0), axis=1, keepdims=True)
    r2 = jnp.sum(jnp.where(hit2, seen, 0.0), axis=1, keepdims=True)
    carry_ref[...] = carry_ref[...] + jnp.sum(onehot, axis=0, keepdims=True)
    cnt_ref[...] = carry_ref[...]

    col = lax.broadcasted_iota(jnp.int32, idx_ref.shape, 1)
    idx = jnp.where(col == 0, e1, jnp.where(col == 1, e2, jnp.where(col == 2, r1, r2)))
    idx_ref[...] = idx.astype(jnp.int32)
    gate_ref[...] = jnp.where(col == 0, gate1, jnp.where(col == 1, gate2, 0.0))


def _route(logits, n_groups, per_group):
    n = logits.shape[0]
    tr = ROW_TILE
    narrow = pl.BlockSpec((tr, 8), lambda i: (i, 0))
    return pl.pallas_call(
        functools.partial(_route_kernel, n_groups=n_groups, per_group=per_group),
        grid=(n // tr,),
        in_specs=[pl.BlockSpec((tr, LANES), lambda i: (i, 0))],
        out_specs=[narrow, narrow, pl.BlockSpec((1, LANES), lambda i: (0, 0))],
        out_shape=[jax.ShapeDtypeStruct((n, 8), jnp.int32), jax.ShapeDtypeStruct((n, 8), F32),
                   jax.ShapeDtypeStruct((1, LANES), F32)],
        scratch_shapes=[pltpu.VMEM((1, LANES), F32)],
        compiler_params=_params("arbitrary"),
        name="route",
    )(logits)


def _row_copy(src_ref, src_row, dst_ref, dst_row, sem):
    return pltpu.make_async_copy(src_ref.at[pl.ds(src_row, 1)], dst_ref.at[pl.ds(dst_row, 1)], sem)


def _wait_rows(src_ref, dst_ref, sem, n):
    pltpu.make_async_copy(src_ref.at[pl.ds(0, n)], dst_ref.at[pl.ds(0, n)], sem).wait()


def _dispatch_kernel(dest_ref, hn_ref, xs_ref, sem):
    td = hn_ref.shape[0]

    def start(r, _):
        for k in range(TOP_K):
            _row_copy(hn_ref, r, xs_ref, dest_ref[0, 0, TOP_K * r + k], sem).start()
        return 0

    lax.fori_loop(0, td, start, 0, unroll=ISSUE_UNROLL)
    for _ in range(TOP_K):
        _wait_rows(hn_ref, xs_ref, sem, td)


def _dispatch(dest, hn):
    n, d = hn.shape
    td = ROW_TILE
    return pl.pallas_call(
        _dispatch_kernel,
        grid=(n // td,),
        in_specs=[pl.BlockSpec((1, 1, TOP_K * td), lambda i: (i, 0, 0), memory_space=pltpu.SMEM),
                  pl.BlockSpec((td, d), lambda i: (i, 0))],
        out_specs=pl.BlockSpec(memory_space=pl.ANY),
        out_shape=jax.ShapeDtypeStruct((n * TOP_K, d), F32),
        scratch_shapes=[pltpu.SemaphoreType.DMA(())],
        compiler_params=_params("arbitrary"),
        name="dispatch",
    )(dest.reshape(n // td, 1, TOP_K * td), hn)


def _expert_kernel(blk_ref, exp_ref, lo_ref, hi_ref, first_ref, xs_ref, w1_ref, w3_ref, w2_ref, o_ref):
    w = pl.program_id(0)
    lo, hi = lo_ref[w], hi_ref[w]

    def swiglu():
        rows = lax.broadcasted_iota(jnp.int32, xs_ref.shape, 0)
        x = jnp.where((rows >= lo) & (rows < hi), xs_ref[...], 0.0).astype(BF16)
        a = jnp.dot(x, w1_ref[...], preferred_element_type=F32)
        b = jnp.dot(x, w3_ref[...], preferred_element_type=F32)
        act = (a * jax.nn.sigmoid(a) * b).astype(BF16)
        return jnp.dot(act, w2_ref[...], preferred_element_type=F32)

    @pl.when((hi > lo) & (first_ref[w] == 1))
    def _():
        o_ref[...] = swiglu()

    @pl.when((hi > lo) & (first_ref[w] == 0))
    def _():
        o_ref[...] += swiglu()


def _experts(plan, xs, w1, w3, w2):
    m, d = xs.shape
    de = w1.shape[2]
    blk = EXPERT_BLOCK
    n_items = plan[0].shape[0]
    row_blk = lambda i, rb, ex, lo, hi, fs: (rb[i], 0)
    weight = lambda i, rb, ex, lo, hi, fs: (ex[i], 0, 0)
    return pl.pallas_call(
        _expert_kernel,
        grid_spec=pltpu.PrefetchScalarGridSpec(
            num_scalar_prefetch=5,
            grid=(n_items,),
            in_specs=[pl.BlockSpec((blk, d), row_blk),
                      pl.BlockSpec((None, d, de), weight),
                      pl.BlockSpec((None, d, de), weight),
                      pl.BlockSpec((None, de, d), weight)],
            out_specs=pl.BlockSpec((blk, d), row_blk)),
        out_shape=jax.ShapeDtypeStruct((m, d), F32),
        compiler_params=_params("arbitrary"),
        name="experts",
    )(*plan, xs, w1, w3, w2)


def _combine_kernel(dest_ref, next_ref, gate_ref, h_ref, rows_ref, g_ref, y_ref, buf, sem, *, final_norm):
    i = pl.program_id(0)
    tc = h_ref.shape[0]
    slot = i % 2

    def gather(idx_ref, into):
        def start(r, _):
            for k in range(TOP_K):
                _row_copy(rows_ref, idx_ref[0, 0, TOP_K * r + k], buf.at[into, k], r, sem.at[into]).start()
            return 0

        lax.fori_loop(0, tc, start, 0, unroll=ISSUE_UNROLL)

    pl.when(i == 0)(lambda: gather(dest_ref, 0))
    pl.when(i + 1 < pl.num_programs(0))(lambda: gather(next_ref, 1 - slot))
    for k in range(TOP_K):
        _wait_rows(rows_ref, buf.at[slot, k], sem.at[slot], tc)
    gates = gate_ref[...]
    y = gates[:, 0:1] * buf[slot, 0] + gates[:, 1:2] * buf[slot, 1]
    h = h_ref[...] + y
    y_ref[...] = _rms(h, g_ref[...]) if final_norm else h


def _combine(dest, gates, h, rows, g, n, row_off, final_norm):
    d = h.shape[1]
    tc = COMBINE_TILE
    off = row_off // tc
    last = off + n // tc - 1
    idx_spec = lambda f: pl.BlockSpec((1, 1, TOP_K * tc), lambda i: (f(i), 0, 0), memory_space=pltpu.SMEM)
    dest3 = dest.reshape(-1, 1, TOP_K * tc)
    return pl.pallas_call(
        functools.partial(_combine_kernel, final_norm=final_norm),
        grid=(n // tc,),
        in_specs=[idx_spec(lambda i: i + off), idx_spec(lambda i: jnp.minimum(i + off + 1, last)),
                  pl.BlockSpec((tc, 8), lambda i: (i + off, 0)),
                  pl.BlockSpec((tc, d), lambda i: (i + off, 0)),
                  pl.BlockSpec(memory_space=pl.ANY),
                  pl.BlockSpec((1, d), lambda i: (0, 0))],
        out_specs=pl.BlockSpec((tc, d), lambda i: (i, 0)),
        out_shape=jax.ShapeDtypeStruct((n, d), F32),
        scratch_shapes=[pltpu.VMEM((2, TOP_K, tc, d), F32), pltpu.SemaphoreType.DMA((2,))],
        compiler_params=_params("arbitrary"),
        name="combine",
    )(dest3, dest3, gates, h, rows, g)


def _round_up(a, m):
    return -(-a // m) * m


def _moe_plan(idx, counts, n_experts):
    blk = EXPERT_BLOCK
    n = idx.shape[0]
    i32 = lambda a: a.astype(jnp.int32)
    counts = i32(counts[0, :n_experts])
    end = jnp.cumsum(counts)
    start = end - counts
    dest = i32(start[idx[:, :TOP_K]] + idx[:, TOP_K:2 * TOP_K])
    n_blocks = n * TOP_K // blk
    block_lo = jnp.arange(n_blocks, dtype=jnp.int32) * blk
    e_first = jnp.minimum(i32(jnp.searchsorted(end, block_lo, side="right")), n_experts - 1)
    e_last = jnp.minimum(i32(jnp.searchsorted(end, block_lo + blk - 1, side="right")), n_experts - 1)
    item_end = jnp.cumsum(e_last - e_first + 1)
    item_start = item_end - (e_last - e_first + 1)
    n_items = n_blocks + n_experts - 1
    w = jnp.arange(n_items, dtype=jnp.int32)
    live = w < item_end[-1]
    rb = jnp.minimum(i32(jnp.searchsorted(item_end, w, side="right")), n_blocks - 1)
    ex = jnp.where(live, e_first[rb] + (w - item_start[rb]), e_last[-1])
    lo = jnp.where(live, jnp.clip(start[ex] - block_lo[rb], 0, blk), 0)
    hi = jnp.where(live, jnp.clip(end[ex] - block_lo[rb], 0, blk), 0)
    first = i32(w == item_start[rb])
    return dest, (rb, i32(ex), i32(lo), i32(hi), first)


def _split_w_in(w_in, width, n_heads):
    wm = jnp.concatenate([w_in[:, :3 * width], w_in[:, 3 * width + n_heads:]], axis=1).astype(BF16)
    wf = jnp.pad(w_in[:, 3 * width:3 * width + n_heads], ((0, 0), (0, LANES - n_heads))).astype(BF16)
    return wm, wf


def kernel(x_prompt, x_sample, cache_fox_k, cache_fox_v, cache_fox_logf, cache_sb_k, cache_sb_v,
           w_in, b_f, g_attn, g_out_fox, g_out_sb, w_out, g_ffn, w_group, b_group, w_router, b_router,
           w1, w3, w2, g_final):
    depth = w_in.shape[0]
    bp, sp, d = x_prompt.shape
    bs, ss, _ = x_sample.shape
    past = cache_fox_logf.shape[2]
    n_heads = cache_fox_logf.shape[3]
    width = n_heads * HEAD_DIM
    n_groups = w_group.shape[2]
    n_experts = w_router.shape[2]
    per_group = n_experts // n_groups
    n_p, n_s = bp * sp, bs * ss
    sk_s = _round_up(past + ss, LANES)
    pad_s = sk_s - past - ss
    row2 = lambda a: a.reshape(1, -1)

    hp = x_prompt.reshape(n_p, d)
    hs = x_sample.reshape(n_s, d)
    outs_p, outs_s = [], []
    for l in range(depth):
        wm, wf = _split_w_in(w_in[l], width, n_heads)
        bfp = jnp.pad(b_f[l], (0, LANES - n_heads)).reshape(1, LANES)
        wo = w_out[l].astype(BF16)
        wr = jnp.pad(jnp.concatenate([w_group[l], w_router[l]], axis=1),
                     ((0, 0), (0, LANES - n_groups - n_experts))).astype(BF16)
        br = jnp.pad(jnp.concatenate([b_group[l], b_router[l]]), (0, LANES - n_groups - n_experts)).reshape(1, LANES)
        g_a, g_f = row2(g_attn[l]), row2(g_ffn[l])
        g_of, g_os = row2(g_out_fox[l]), row2(g_out_sb[l])

        lf, qf, kfb, kf, vfb, vf, qs, ksb, ks, vsb, vs = _in_proj(hp, g_a, wm, wf, bfp, n_heads)
        b3 = lambda a: a.reshape(bp, sp, -1)
        cf = _cumsum(b3(lf), min(sp, 512))
        tq = min(sp, ATTN_TILE)
        nf = _attention("fox", b3(qf), b3(kfb), b3(vfb), g_of, cf, tq=tq, tk=tq, q_off=0)
        ns = _attention("sb", b3(qs), b3(ksb), b3(vsb), g_os, None, tq=tq, tk=tq, q_off=0)
        outs_p.append((kf, vf, lf, ks, vs))

        lf2, qf2, kfb2, kf2, vfb2, vf2, qs2, ksb2, ks2, vsb2, vs2 = _in_proj(hs, g_a, wm, wf, bfp, n_heads)
        c3 = lambda a: a.reshape(bs, ss, -1)
        keys = lambda cache, new: jnp.concatenate(
            [cache.reshape(bs, past, width).astype(BF16), c3(new), jnp.zeros((bs, pad_s, width), BF16)], axis=1)
        cf_all = _cumsum(jnp.concatenate([cache_fox_logf[l].astype(F32), c3(lf2)], axis=1), past + ss)
        cf_all = jnp.pad(cf_all, ((0, 0), (0, pad_s), (0, 0)))
        nf2 = _attention("fox", c3(qf2), keys(cache_fox_k[l], kfb2), keys(cache_fox_v[l], vfb2), g_of, cf_all,
                         tq=ss, tk=sk_s, q_off=past)
        ns2 = _attention("sb", c3(qs2), keys(cache_sb_k[l], ksb2), keys(cache_sb_v[l], vsb2), g_os, None,
                         tq=ss, tk=sk_s, q_off=past)
        outs_s.append((kf2, vf2, lf2, ks2, vs2))

        h_all, hn_all, logits = _merge((nf.reshape(n_p, width), ns.reshape(n_p, width), hp),
                                       (nf2.reshape(n_s, width), ns2.reshape(n_s, width), hs), wo, g_f, wr, br)
        idx, gates, counts = _route(logits, n_groups, per_group)
        dest, plan = _moe_plan(idx, counts, n_experts)
        xs = _dispatch(dest, hn_all)
        rows = _experts(plan, xs, w1[l].astype(BF16), w3[l].astype(BF16), w2[l].astype(BF16))
        last = l == depth - 1
        g_last = row2(g_final)
        hp = _combine(dest, gates, h_all, rows, g_last, n_p, 0, last)
        hs = _combine(dest, gates, h_all, rows, g_last, n_s, n_p, last)

    heads = lambda a, b, s: a.reshape(b, s, n_heads, HEAD_DIM)
    stack = lambda outs, i, f: jnp.stack([f(o[i]) for o in outs])
    hp_f = lambda a: heads(a, bp, sp)
    hs_f = lambda a: heads(a, bs, ss)
    return (hp.reshape(bp, sp, d), hs.reshape(bs, ss, d),
            stack(outs_p, 0, hp_f), stack(outs_p, 1, hp_f), stack(outs_p, 2, lambda a: a.reshape(bp, sp, n_heads)),
            stack(outs_p, 3, hp_f), stack(outs_p, 4, hp_f),
            stack(outs_s, 0, hs_f), stack(outs_s, 1, hs_f), stack(outs_s, 2, lambda a: a.reshape(bs, ss, n_heads)),
            stack(outs_s, 3, hs_f), stack(outs_s, 4, hs_f))
```

```python
import functools
import math

import jax
import jax.numpy as jnp
from jax import lax
from jax.experimental import pallas as pl
from jax.experimental.pallas import tpu as pltpu

F32 = jnp.float32
BF16 = jnp.bfloat16

HEAD_DIM = 128
TOP_K = 2
EPS = 1e-6
LANES = 128
MXU_DIM = 256
NEG_BIG = -1e30
VMEM_LIMIT = 56 * 1024 * 1024
LOG2E = math.log2(math.e)
SCALE = HEAD_DIM ** -0.5
SCALE_LOG2 = SCALE * LOG2E

ROW_TILE = 256
ATTN_TILE = 512
FOX_GROUP = 4
SB_GROUP = 2
EXPERT_BLOCK = 256
COMBINE_TILE = 128
ISSUE_UNROLL = 8


def _params(*sem):
    return pltpu.CompilerParams(dimension_semantics=sem, vmem_limit_bytes=VMEM_LIMIT)


def _log_sigmoid(x):
    return jnp.minimum(x, 0.0) - jnp.log1p(jnp.exp(-jnp.abs(x)))


def _rms(x, g):
    return x * lax.rsqrt(jnp.mean(x * x, axis=-1, keepdims=True) + EPS) * g


def _in_proj_kernel(x_ref, g_ref, wm_ref, wf_ref, bf_ref, lf_ref, *out_refs, width, n_heads):
    hn = _rms(x_ref[...], g_ref[...]).astype(BF16)
    refs = iter(out_refs)
    for c in range(6):
        res = jnp.dot(hn, wm_ref[:, c * width:(c + 1) * width], preferred_element_type=F32)
        next(refs)[...] = (res if c % 3 else res * SCALE_LOG2).astype(BF16)
        if c % 3:
            kv_ref = next(refs)
            kv_ref[...] = res.reshape(kv_ref.shape)
    fl = jnp.dot(hn, wf_ref[...], preferred_element_type=F32) + bf_ref[...]
    lf_ref[...] = _log_sigmoid(fl)[:, :n_heads]


def _in_proj(x, g, wm, wf, bf, n_heads):
    n, d = x.shape
    width = wm.shape[1] // 6
    tm = ROW_TILE
    row = lambda w: pl.BlockSpec((tm, w), lambda i: (i, 0))
    const = lambda a: pl.BlockSpec(a.shape, lambda i: (0, 0), pipeline_mode=pl.Buffered(1))
    per_head = pl.BlockSpec((tm, n_heads, HEAD_DIM), lambda i: (i, 0, 0))
    rows_out = jax.ShapeDtypeStruct((n, width), BF16)
    heads_out = jax.ShapeDtypeStruct((n, n_heads, HEAD_DIM), F32)
    mixer_specs = [row(width)] + [row(width), per_head] * 2
    mixer_shapes = [rows_out] + [rows_out, heads_out] * 2
    return pl.pallas_call(
        functools.partial(_in_proj_kernel, width=width, n_heads=n_heads),
        grid=(n // tm,),
        in_specs=[row(d), const(g), const(wm), const(wf), const(bf)],
        out_specs=[row(n_heads)] + mixer_specs * 2,
        out_shape=[jax.ShapeDtypeStruct((n, n_heads), F32)] + mixer_shapes * 2,
        compiler_params=_params("arbitrary"),
        name="in_proj",
    )(x, g, wm, wf, bf)


def _cumsum_kernel(lf_ref, cf_ref, carry_ref):
    tt = lf_ref.shape[0]

    @pl.when(pl.program_id(1) == 0)
    def _():
        carry_ref[...] = jnp.zeros_like(carry_ref)

    r = lax.broadcasted_iota(jnp.int32, (tt, tt), 0)
    c = lax.broadcasted_iota(jnp.int32, (tt, tt), 1)
    tri = jnp.where(c <= r, 1.0, 0.0).astype(F32)
    cf = jnp.dot(tri, lf_ref[...], precision=lax.Precision.HIGHEST,
                 preferred_element_type=F32) + carry_ref[...]
    cf_ref[...] = cf
    carry_ref[...] = cf[tt - 1:tt, :]


def _cumsum(lf, tt):
    b, t, h = lf.shape
    return pl.pallas_call(
        _cumsum_kernel,
        grid=(b, t // tt),
        in_specs=[pl.BlockSpec((None, tt, h), lambda i, j: (i, j, 0))],
        out_specs=pl.BlockSpec((None, tt, h), lambda i, j: (i, j, 0)),
        out_shape=jax.ShapeDtypeStruct((b, t, h), F32),
        scratch_shapes=[pltpu.VMEM((1, h), F32)],
        compiler_params=_params("arbitrary", "arbitrary"),
        name="cumsum",
    )(lf)


def _split3(x):
    hi = x.astype(BF16).astype(F32)
    r1 = x - hi
    mid = r1.astype(BF16).astype(F32)
    lo = (r1 - mid).astype(BF16).astype(F32)
    return hi, mid, lo


def _bias_block(c, as_query):
    hi, mid, lo = _split3(c * LOG2E)
    lane = lax.broadcasted_iota(jnp.int32, (c.shape[0], LANES), 1)
    if as_query:
        parts, base, ones_lo = (hi, mid, lo), 0, 3
    else:
        parts, base, ones_lo = (-hi, -mid, -lo), 3, 0
    blk = jnp.where((lane >= ones_lo) & (lane < ones_lo + 3), 1.0, 0.0)
    for i, p in enumerate(parts):
        blk = jnp.where(lane == base + i, p, blk)
    return blk.astype(BF16)


def _select_col(x, h):
    lane = lax.broadcasted_iota(jnp.int32, x.shape, 1)
    return jnp.sum(jnp.where(lane == h, x, 0.0), axis=1, keepdims=True)


def _fox_kernel(q_ref, k_ref, v_ref, cq_ref, ck_ref, g_ref, o_ref, kaug, vt, *, tq, tk, q_off, prep, group, wide):
    hg = pl.program_id(1)
    qi = pl.program_id(2)
    sk = k_ref.shape[0]
    hd = lambda g: slice(g * HEAD_DIM, (g + 1) * HEAD_DIM)

    @pl.when(qi == 0)
    def _():
        for g in range(group):
            for r0 in range(0, sk, prep):
                rows = pl.ds(r0, prep)
                kaug[g, rows, :HEAD_DIM] = k_ref[rows, hd(g)]
                kaug[g, rows, HEAD_DIM:] = _bias_block(_select_col(ck_ref[rows, :], hg * group + g), False)
                vt[g, :, rows] = v_ref[rows, hd(g)].astype(F32).T.astype(BF16)

    qa = [jnp.concatenate([q_ref[:, hd(g)], _bias_block(_select_col(cq_ref[...], hg * group + g), True)], axis=1)
          for g in range(group)]
    q_lo = q_off + qi * tq
    kiota = lax.broadcasted_iota(jnp.int32, (tk, tq), 0)
    qpos = q_lo + lax.broadcasted_iota(jnp.int32, (tk, tq), 1)

    def chunk(j, carry, width, masked=False):
        k0 = pl.multiple_of(j * width, width)

        def logits(g):
            s = lax.dot_general(kaug[g, pl.ds(k0, width), :], qa[g], (((1,), (1,)), ((), ())),
                                preferred_element_type=F32)
            if masked:
                s = jnp.where(k0 + kiota <= qpos, s, NEG_BIG)
            return s

        def absorb(g, s):
            m, l, acc = carry[g]
            m_new = jnp.maximum(m, jnp.max(s, axis=0, keepdims=True))
            p = jnp.exp2(s - m_new)
            alpha = jnp.exp2(m - m_new)
            l = alpha * l + jnp.sum(p, axis=0, keepdims=True)
            acc = alpha * acc + jnp.dot(vt[g, :, pl.ds(k0, width)], p.astype(BF16), preferred_element_type=F32)
            return m_new, l, acc

        out, s_prev = [], None
        for g in range(group + 1):
            s_next = logits(g) if g < group else None
            if g > 0:
                out.append(absorb(g - 1, s_prev))
            s_prev = s_next
        return tuple(out)

    n_full = (q_lo + 1) // tk
    n_all = (q_lo + tq + tk - 1) // tk
    carry = tuple((jnp.full((1, tq), NEG_BIG, F32), jnp.zeros((1, tq), F32), jnp.zeros((HEAD_DIM, tq), F32))
                  for _ in range(group))
    n_narrow = n_full
    if wide:
        carry = lax.fori_loop(0, n_full // 2, lambda j, c: chunk(j, c, 2 * tk), carry)
        n_narrow = n_full % 2
    carry = lax.fori_loop(n_full - n_narrow, n_full, lambda j, c: chunk(j, c, tk), carry)
    carry = lax.fori_loop(n_full, n_all, lambda j, c: chunk(j, c, tk, True), carry)
    for g in range(group):
        m, l, acc = carry[g]
        o_ref[:, hd(g)] = _rms((acc / l).T, g_ref[:, hd(g)]).astype(o_ref.dtype)


def _sb_kernel(q_ref, k_ref, v_ref, g_ref, o_ref, *, tq, tk, q_off, group, wide):
    qi = pl.program_id(2)
    hd = lambda g: slice(g * HEAD_DIM, (g + 1) * HEAD_DIM)
    q = [q_ref[:, hd(g)] for g in range(group)]
    q_lo = q_off + qi * tq
    ts = MXU_DIM if tk % MXU_DIM == 0 else LANES
    r = lax.broadcasted_iota(jnp.int32, (ts, ts), 0)
    c = lax.broadcasted_iota(jnp.int32, (ts, ts), 1)
    neg_from = jnp.where(r >= c, -1.0, 0.0).astype(BF16)
    sign_bit = jnp.uint32(0x80000000)

    def chunk(j, carry, width, masked=False):
        k0 = pl.multiple_of(j * width, width)
        n_sub = width // ts
        first_row = (lambda b: b * ts) if (masked and tq == tk) else (lambda b: 0)

        def logits(g, b):
            r0 = first_row(b)
            z = lax.dot_general(q[g][r0:], k_ref[pl.ds(k0 + b * ts, ts), hd(g)], (((1,), (1,)), ((), ())),
                                preferred_element_type=F32)
            neg_abs = pltpu.bitcast(pltpu.bitcast(z, jnp.uint32) | sign_bit, F32)
            sp = jnp.maximum(z, 0.0) + jnp.log2(1.0 + jnp.exp2(neg_abs))
            valid = None
            if masked:
                qpos = q_lo + r0 + lax.broadcasted_iota(jnp.int32, z.shape, 0)
                valid = k0 + b * ts + lax.broadcasted_iota(jnp.int32, z.shape, 1) < qpos
                sp = jnp.where(valid, sp, 0.0)
            hi = sp.astype(BF16)
            lo = (sp - hi.astype(F32)).astype(BF16)
            return z, hi, lo, valid

        def weights(state, after, r0):
            z, hi, lo, valid = state
            sfx = (jnp.dot(hi, neg_from, preferred_element_type=F32)
                   + jnp.dot(lo, neg_from, preferred_element_type=F32)) + after[r0:]
            a = jnp.exp2(z + sfx)
            if masked:
                a = jnp.where(valid, a, 0.0)
            total = sfx[:, 0:1]
            return (jnp.concatenate([after[:r0], total], axis=0) if r0 else total), a.astype(BF16)

        items = [(g, b) for b in reversed(range(n_sub)) for g in range(group)]
        after = [carry[g][0] for g in range(group)]
        acc = [carry[g][1] for g in range(group)]
        staged, ready = {}, {}
        n = len(items)
        for step in range(n + 2):
            if step < n:
                staged[step] = logits(*items[step])
            if 0 <= step - 1 < n:
                g, b = items[step - 1]
                after[g], ready[step - 1] = weights(staged.pop(step - 1), after[g], first_row(b))
            if 0 <= step - 2 < n:
                g, b = items[step - 2]
                r0 = first_row(b)
                rows = pl.ds(k0 + b * ts, ts)
                pv = jnp.dot(ready.pop(step - 2), v_ref[rows, hd(g)], preferred_element_type=F32)
                acc[g] = jnp.concatenate([acc[g][:r0], acc[g][r0:] + pv], axis=0) if r0 else acc[g] + pv
        return tuple((after[g], acc[g]) for g in range(group))

    n_full = q_lo // tk
    n_all = (q_lo + tq - 1 + tk - 1) // tk
    carry = tuple((jnp.zeros((tq, 1), F32), jnp.zeros((tq, HEAD_DIM), F32)) for _ in range(group))
    carry = lax.fori_loop(0, n_all - n_full, lambda i, c: chunk(n_all - 1 - i, c, tk, True), carry)
    n_narrow = n_full
    if wide:
        n_narrow = n_full % 2
    carry = lax.fori_loop(0, n_narrow, lambda i, c: chunk(n_full - 1 - i, c, tk), carry)
    if wide:
        carry = lax.fori_loop(0, n_full // 2, lambda i, c: chunk(n_full // 2 - 1 - i, c, 2 * tk), carry)
    for g in range(group):
        o_ref[:, hd(g)] = _rms(carry[g][1], g_ref[:, hd(g)]).astype(o_ref.dtype)


def _attention(kind, q, k, v, g, cf, *, tq, tk, q_off):
    b, sq, width = q.shape
    sk = k.shape[1]
    n_heads = width // HEAD_DIM
    group = FOX_GROUP if kind == "fox" else SB_GROUP
    assert sq % tq == 0 and sk % tk == 0 and q_off % tq == 0 and n_heads % group == 0
    w = group * HEAD_DIM
    wide = sk % (2 * tk) == 0
    q_spec = pl.BlockSpec((None, tq, w), lambda i, h, j: (i, j, h))
    kv_spec = pl.BlockSpec((None, sk, w), lambda i, h, j: (i, 0, h))
    g_spec = pl.BlockSpec((1, w), lambda i, h, j: (0, h))
    if kind == "fox":
        body = functools.partial(_fox_kernel, tq=tq, tk=tk, q_off=q_off, prep=math.gcd(sk, 512), group=group,
                                 wide=wide)
        q_blk = q_off // tq
        in_specs = [q_spec, kv_spec, kv_spec,
                    pl.BlockSpec((None, tq, n_heads), lambda i, h, j: (i, j + q_blk, 0)),
                    pl.BlockSpec((None, sk, n_heads), lambda i, h, j: (i, 0, 0)), g_spec]
        args = (q, k, v, cf, cf, g)
        scratch = [pltpu.VMEM((group, sk, 2 * HEAD_DIM), BF16), pltpu.VMEM((group, HEAD_DIM, sk), BF16)]
    else:
        body = functools.partial(_sb_kernel, tq=tq, tk=tk, q_off=q_off, group=group, wide=wide)
        in_specs = [q_spec, kv_spec, kv_spec, g_spec]
        args = (q, k, v, g)
        scratch = []
    return pl.pallas_call(
        body,
        grid=(b, n_heads // group, sq // tq),
        in_specs=in_specs,
        out_specs=q_spec,
        out_shape=jax.ShapeDtypeStruct((b, sq, width), BF16),
        scratch_shapes=scratch,
        compiler_params=_params("arbitrary", "arbitrary", "arbitrary"),
        name=kind,
    )(*args)


def _merge_kernel(nfp_ref, nsp_ref, xp_ref, nfs_ref, nss_ref, xs_ref, wo_ref, g_ref, wr_ref, br_ref,
                  h_ref, hn_ref, lg_ref, *, prompt_blocks):
    def body(nf_ref, ns_ref, x_ref):
        width = nf_ref.shape[1]
        h = x_ref[...] + (jnp.dot(nf_ref[...], wo_ref[:width, :], preferred_element_type=F32)
                          + jnp.dot(ns_ref[...], wo_ref[width:, :], preferred_element_type=F32))
        h_ref[...] = h
        hn = _rms(h, g_ref[...])
        hn_ref[...] = hn
        lg_ref[...] = jnp.dot(hn.astype(BF16), wr_ref[...], preferred_element_type=F32) + br_ref[...]

    is_prompt = pl.program_id(0) < prompt_blocks
    pl.when(is_prompt)(lambda: body(nfp_ref, nsp_ref, xp_ref))
    pl.when(jnp.logical_not(is_prompt))(lambda: body(nfs_ref, nss_ref, xs_ref))


def _merge(prompt, sample, wo, g, wr, br):
    n_p, d = prompt[2].shape
    n_s = sample[2].shape[0]
    width = prompt[0].shape[1]
    tm = ROW_TILE
    pb, sb = n_p // tm, n_s // tm
    n_total = n_p + n_s
    p_row = lambda w: pl.BlockSpec((tm, w), lambda i: (jnp.minimum(i, pb - 1), 0))
    s_row = lambda w: pl.BlockSpec((tm, w), lambda i: (jnp.maximum(i - pb, 0), 0))
    out_row = lambda w: pl.BlockSpec((tm, w), lambda i: (i, 0))
    const = lambda a: pl.BlockSpec(a.shape, lambda i: (0, 0), pipeline_mode=pl.Buffered(1))
    return pl.pallas_call(
        functools.partial(_merge_kernel, prompt_blocks=pb),
        grid=(pb + sb,),
        in_specs=[p_row(width), p_row(width), p_row(d), s_row(width), s_row(width), s_row(d),
                  const(wo), const(g), const(wr), const(br)],
        out_specs=[out_row(d), out_row(d), out_row(LANES)],
        out_shape=[jax.ShapeDtypeStruct((n_total, d), F32), jax.ShapeDtypeStruct((n_total, d), F32),
                   jax.ShapeDtypeStruct((n_total, LANES), F32)],
        compiler_params=_params("arbitrary"),
        name="merge",
    )(*prompt, *sample, wo, g, wr, br)


def _route_kernel(lg_ref, idx_ref, gate_ref, cnt_ref, carry_ref, *, n_groups, per_group):
    tr = lg_ref.shape[0]

    @pl.when(pl.program_id(0) == 0)
    def _():
        carry_ref[...] = jnp.zeros_like(carry_ref)

    x = lg_ref[...]
    lane = lax.broadcasted_iota(jnp.int32, x.shape, 1).astype(F32)
    far = float(LANES)
    rmax = lambda a: jnp.max(a, axis=1, keepdims=True)
    first = lambda hit: jnp.min(jnp.where(hit, lane, far), axis=1, keepdims=True)

    gmask = lane < n_groups
    gl = jnp.where(gmask, x, -jnp.inf)
    gmax = rmax(gl)
    gsel = first(gl == gmax)
    p_g = 1.0 / jnp.sum(jnp.where(gmask, jnp.exp(x - gmax), 0.0), axis=1, keepdims=True)
    lo = n_groups + gsel * per_group
    emask = (lane >= lo) & (lane < lo + per_group)
    el = jnp.where(emask, x, -jnp.inf)
    m1 = rmax(el)
    i1 = first(el == m1)
    el2 = jnp.where(lane == i1, -jnp.inf, el)
    m2 = rmax(el2)
    i2 = first(el2 == m2)
    zsum = jnp.sum(jnp.where(emask, jnp.exp(x - m1), 0.0), axis=1, keepdims=True)
    p1 = 1.0 / zsum
    p2 = jnp.exp(m2 - m1) / zsum
    psum = p1 + p2
    gate1 = p_g * p1 / psum
    gate2 = p_g * p2 / psum
    e1 = i1 - n_groups
    e2 = i2 - n_groups
    hit1 = lane == e1
    hit2 = lane == e2
    onehot = jnp.where(hit1 | hit2, 1.0, 0.0)
    r = lax.broadcasted_iota(jnp.int32, (tr, tr), 0)
    c = lax.broadcasted_iota(jnp.int32, (tr, tr), 1)
    before = jnp.where(c < r, 1.0, 0.0).astype(BF16)
    seen = jnp.dot(before, onehot.astype(BF16), preferred_element_type=F32) + carry_ref[...]
    r1 = jnp.sum(jnp.where(hit1, seen, 0.0), axis=1, keepdims=True)
    r2 = jnp.sum(jnp.where(hit2, seen, 0.0), axis=1, keepdims=True)
    carry_ref[...] = carry_ref[...] + jnp.sum(onehot, axis=0, keepdims=True)
    cnt_ref[...] = carry_ref[...]

    col = lax.broadcasted_iota(jnp.int32, idx_ref.shape, 1)
    idx = jnp.where(col == 0, e1, jnp.where(col == 1, e2, jnp.where(col == 2, r1, r2)))
    idx_ref[...] = idx.astype(jnp.int32)
    gate_ref[...] = jnp.where(col == 0, gate1, jnp.where(col == 1, gate2, 0.0))


def _route(logits, n_groups, per_group):
    n = logits.shape[0]
    tr = ROW_TILE
    narrow = pl.BlockSpec((tr, 8), lambda i: (i, 0))
    return pl.pallas_call(
        functools.partial(_route_kernel, n_groups=n_groups, per_group=per_group),
        grid=(n // tr,),
        in_specs=[pl.BlockSpec((tr, LANES), lambda i: (i, 0))],
        out_specs=[narrow, narrow, pl.BlockSpec((1, LANES), lambda i: (0, 0))],
        out_shape=[jax.ShapeDtypeStruct((n, 8), jnp.int32), jax.ShapeDtypeStruct((n, 8), F32),
                   jax.ShapeDtypeStruct((1, LANES), F32)],
        scratch_shapes=[pltpu.VMEM((1, LANES), F32)],
        compiler_params=_params("arbitrary"),
        name="route",
    )(logits)


def _row_copy(src_ref, src_row, dst_ref, dst_row, sem):
    return pltpu.make_async_copy(src_ref.at[pl.ds(src_row, 1)], dst_ref.at[pl.ds(dst_row, 1)], sem)


def _wait_rows(src_ref, dst_ref, sem, n):
    pltpu.make_async_copy(src_ref.at[pl.ds(0, n)], dst_ref.at[pl.ds(0, n)], sem).wait()


def _dispatch_kernel(dest_ref, hn_ref, xs_ref, sem):
    td = hn_ref.shape[0]

    def start(r, _):
        for k in range(TOP_K):
            _row_copy(hn_ref, r, xs_ref, dest_ref[0, 0, TOP_K * r + k], sem).start()
        return 0

    lax.fori_loop(0, td, start, 0, unroll=ISSUE_UNROLL)
    for _ in range(TOP_K):
        _wait_rows(hn_ref, xs_ref, sem, td)


def _dispatch(dest, hn):
    n, d = hn.shape
    td = ROW_TILE
    return pl.pallas_call(
        _dispatch_kernel,
        grid=(n // td,),
        in_specs=[pl.BlockSpec((1, 1, TOP_K * td), lambda i: (i, 0, 0), memory_space=pltpu.SMEM),
                  pl.BlockSpec((td, d), lambda i: (i, 0))],
        out_specs=pl.BlockSpec(memory_space=pl.ANY),
        out_shape=jax.ShapeDtypeStruct((n * TOP_K, d), F32),
        scratch_shapes=[pltpu.SemaphoreType.DMA(())],
        compiler_params=_params("arbitrary"),
        name="dispatch",
    )(dest.reshape(n // td, 1, TOP_K * td), hn)


def _expert_kernel(blk_ref, exp_ref, lo_ref, hi_ref, first_ref, xs_ref, w1_ref, w3_ref, w2_ref, o_ref):
    w = pl.program_id(0)
    lo, hi = lo_ref[w], hi_ref[w]

    def swiglu():
        rows = lax.broadcasted_iota(jnp.int32, xs_ref.shape, 0)
        x = jnp.where((rows >= lo) & (rows < hi), xs_ref[...], 0.0).astype(BF16)
        a = jnp.dot(x, w1_ref[...], preferred_element_type=F32)
        b = jnp.dot(x, w3_ref[...], preferred_element_type=F32)
        act = (a * jax.nn.sigmoid(a) * b).astype(BF16)
        return jnp.dot(act, w2_ref[...], preferred_element_type=F32)

    @pl.when((hi > lo) & (first_ref[w] == 1))
    def _():
        o_ref[...] = swiglu()

    @pl.when((hi > lo) & (first_ref[w] == 0))
    def _():
        o_ref[...] += swiglu()


def _experts(plan, xs, w1, w3, w2):
    m, d = xs.shape
    de = w1.shape[2]
    blk = EXPERT_BLOCK
    n_items = plan[0].shape[0]
    row_blk = lambda i, rb, ex, lo, hi, fs: (rb[i], 0)
    weight = lambda i, rb, ex, lo, hi, fs: (ex[i], 0, 0)
    return pl.pallas_call(
        _expert_kernel,
        grid_spec=pltpu.PrefetchScalarGridSpec(
            num_scalar_prefetch=5,
            grid=(n_items,),
            in_specs=[pl.BlockSpec((blk, d), row_blk),
                      pl.BlockSpec((None, d, de), weight),
                      pl.BlockSpec((None, d, de), weight),
                      pl.BlockSpec((None, de, d), weight)],
            out_specs=pl.BlockSpec((blk, d), row_blk)),
        out_shape=jax.ShapeDtypeStruct((m, d), F32),
        compiler_params=_params("arbitrary"),
        name="experts",
    )(*plan, xs, w1, w3, w2)


def _combine_kernel(dest_ref, next_ref, gate_ref, h_ref, rows_ref, g_ref, y_ref, buf, sem, *, final_norm):
    i = pl.program_id(0)
    tc = h_ref.shape[0]
    slot = i % 2

    def gather(idx_ref, into):
        def start(r, _):
            for k in range(TOP_K):
                _row_copy(rows_ref, idx_ref[0, 0, TOP_K * r + k], buf.at[into, k], r, sem.at[into]).start()
            return 0

        lax.fori_loop(0, tc, start, 0, unroll=ISSUE_UNROLL)

    pl.when(i == 0)(lambda: gather(dest_ref, 0))
    pl.when(i + 1 < pl.num_programs(0))(lambda: gather(next_ref, 1 - slot))
    for k in range(TOP_K):
        _wait_rows(rows_ref, buf.at[slot, k], sem.at[slot], tc)
    gates = gate_ref[...]
    y = gates[:, 0:1] * buf[slot, 0] + gates[:, 1:2] * buf[slot, 1]
    h = h_ref[...] + y
    y_ref[...] = _rms(h, g_ref[...]) if final_norm else h


def _combine(dest, gates, h, rows, g, n, row_off, final_norm):
    d = h.shape[1]
    tc = COMBINE_TILE
    off = row_off // tc
    last = off + n // tc - 1
    idx_spec = lambda f: pl.BlockSpec((1, 1, TOP_K * tc), lambda i: (f(i), 0, 0), memory_space=pltpu.SMEM)
    dest3 = dest.reshape(-1, 1, TOP_K * tc)
    return pl.pallas_call(
        functools.partial(_combine_kernel, final_norm=final_norm),
        grid=(n // tc,),
        in_specs=[idx_spec(lambda i: i + off), idx_spec(lambda i: jnp.minimum(i + off + 1, last)),
                  pl.BlockSpec((tc, 8), lambda i: (i + off, 0)),
                  pl.BlockSpec((tc, d), lambda i: (i + off, 0)),
                  pl.BlockSpec(memory_space=pl.ANY),
                  pl.BlockSpec((1, d), lambda i: (0, 0))],
        out_specs=pl.BlockSpec((tc, d), lambda i: (i, 0)),
        out_shape=jax.ShapeDtypeStruct((n, d), F32),
        scratch_shapes=[pltpu.VMEM((2, TOP_K, tc, d), F32), pltpu.SemaphoreType.DMA((2,))],
        compiler_params=_params("arbitrary"),
        name="combine",
    )(dest3, dest3, gates, h, rows, g)


def _round_up(a, m):
    return -(-a // m) * m


def _moe_plan(idx, counts, n_experts):
    blk = EXPERT_BLOCK
    n = idx.shape[0]
    i32 = lambda a: a.astype(jnp.int32)
    pick = lambda table, i: jnp.sum(jnp.where(i[..., None] == jnp.arange(table.shape[0]), table, 0), axis=-1)
    rank_of = lambda ends, x: jnp.sum(i32(ends[None, :] <= x[:, None]), axis=1)
    counts = i32(counts[0, :n_experts])
    end = jnp.cumsum(counts)
    start = end - counts
    dest = i32(pick(start, idx[:, :TOP_K]) + idx[:, TOP_K:2 * TOP_K])
    n_blocks = n * TOP_K // blk
    block_lo = jnp.arange(n_blocks, dtype=jnp.int32) * blk
    e_first = jnp.minimum(rank_of(end, block_lo), n_experts - 1)
    e_last = jnp.minimum(rank_of(end, block_lo + blk - 1), n_experts - 1)
    item_end = jnp.cumsum(e_last - e_first + 1)
    item_start = item_end - (e_last - e_first + 1)
    n_items = n_blocks + n_experts - 1
    w = jnp.arange(n_items, dtype=jnp.int32)
    live = w < item_end[-1]
    rb = jnp.minimum(rank_of(item_end, w), n_blocks - 1)
    ex = jnp.where(live, pick(e_first, rb) + (w - pick(item_start, rb)), e_last[-1])
    lo = jnp.where(live, jnp.clip(pick(start, ex) - rb * blk, 0, blk), 0)
    hi = jnp.where(live, jnp.clip(pick(end, ex) - rb * blk, 0, blk), 0)
    first = i32(w == pick(item_start, rb))
    return dest, (i32(rb), i32(ex), i32(lo), i32(hi), first)


def _split_w_in(w_in, width, n_heads):
    wm = jnp.concatenate([w_in[:, :3 * width], w_in[:, 3 * width + n_heads:]], axis=1).astype(BF16)
    wf = jnp.pad(w_in[:, 3 * width:3 * width + n_heads], ((0, 0), (0, LANES - n_heads))).astype(BF16)
    return wm, wf


def kernel(x_prompt, x_sample, cache_fox_k, cache_fox_v, cache_fox_logf, cache_sb_k, cache_sb_v,
           w_in, b_f, g_attn, g_out_fox, g_out_sb, w_out, g_ffn, w_group, b_group, w_router, b_router,
           w1, w3, w2, g_final):
    depth = w_in.shape[0]
    bp, sp, d = x_prompt.shape
    bs, ss, _ = x_sample.shape
    past = cache_fox_logf.shape[2]
    n_heads = cache_fox_logf.shape[3]
    width = n_heads * HEAD_DIM
    n_groups = w_group.shape[2]
    n_experts = w_router.shape[2]
    per_group = n_experts // n_groups
    n_p, n_s = bp * sp, bs * ss
    sk_s = _round_up(past + ss, LANES)
    pad_s = sk_s - past - ss
    row2 = lambda a: a.reshape(1, -1)

    hp = x_prompt.reshape(n_p, d)
    hs = x_sample.reshape(n_s, d)
    outs_p, outs_s = [], []
    for l in range(depth):
        wm, wf = _split_w_in(w_in[l], width, n_heads)
        bfp = jnp.pad(b_f[l], (0, LANES - n_heads)).reshape(1, LANES)
        wo = w_out[l].astype(BF16)
        wr = jnp.pad(jnp.concatenate([w_group[l], w_router[l]], axis=1),
                     ((0, 0), (0, LANES - n_groups - n_experts))).astype(BF16)
        br = jnp.pad(jnp.concatenate([b_group[l], b_router[l]]), (0, LANES - n_groups - n_experts)).reshape(1, LANES)
        g_a, g_f = row2(g_attn[l]), row2(g_ffn[l])
        g_of, g_os = row2(g_out_fox[l]), row2(g_out_sb[l])

        lf, qf, kfb, kf, vfb, vf, qs, ksb, ks, vsb, vs = _in_proj(hp, g_a, wm, wf, bfp, n_heads)
        b3 = lambda a: a.reshape(bp, sp, -1)
        cf = _cumsum(b3(lf), min(sp, 512))
        tq = min(sp, ATTN_TILE)
        nf = _attention("fox", b3(qf), b3(kfb), b3(vfb), g_of, cf, tq=tq, tk=tq, q_off=0)
        ns = _attention("sb", b3(qs), b3(ksb), b3(vsb), g_os, None, tq=tq, tk=tq, q_off=0)
        outs_p.append((kf, vf, lf, ks, vs))

        lf2, qf2, kfb2, kf2, vfb2, vf2, qs2, ksb2, ks2, vsb2, vs2 = _in_proj(hs, g_a, wm, wf, bfp, n_heads)
        c3 = lambda a: a.reshape(bs, ss, -1)
        keys = lambda cache, new: jnp.concatenate(
            [cache.reshape(bs, past, width).astype(BF16), c3(new), jnp.zeros((bs, pad_s, width), BF16)], axis=1)
        cf_all = _cumsum(jnp.concatenate([cache_fox_logf[l].astype(F32), c3(lf2)], axis=1), past + ss)
        cf_all = jnp.pad(cf_all, ((0, 0), (0, pad_s), (0, 0)))
        nf2 = _attention("fox", c3(qf2), keys(cache_fox_k[l], kfb2), keys(cache_fox_v[l], vfb2), g_of, cf_all,
                         tq=ss, tk=sk_s, q_off=past)
        ns2 = _attention("sb", c3(qs2), keys(cache_sb_k[l], ksb2), keys(cache_sb_v[l], vsb2), g_os, None,
                         tq=ss, tk=sk_s, q_off=past)
        outs_s.append((kf2, vf2, lf2, ks2, vs2))

        h_all, hn_all, logits = _merge((nf.reshape(n_p, width), ns.reshape(n_p, width), hp),
                                       (nf2.reshape(n_s, width), ns2.reshape(n_s, width), hs), wo, g_f, wr, br)
        idx, gates, counts = _route(logits, n_groups, per_group)
        dest, plan = _moe_plan(idx, counts, n_experts)
        xs = _dispatch(dest, hn_all)
        rows = _experts(plan, xs, w1[l].astype(BF16), w3[l].astype(BF16), w2[l].astype(BF16))
        last = l == depth - 1
        g_last = row2(g_final)
        hp = _combine(dest, gates, h_all, rows, g_last, n_p, 0, last)
        hs = _combine(dest, gates, h_all, rows, g_last, n_s, n_p, last)

    heads = lambda a, b, s: a.reshape(b, s, n_heads, HEAD_DIM)
    stack = lambda outs, i, f: jnp.stack([f(o[i]) for o in outs])
    hp_f = lambda a: heads(a, bp, sp)
    hs_f = lambda a: heads(a, bs, ss)
    return (hp.reshape(bp, sp, d), hs.reshape(bs, ss, d),
            stack(outs_p, 0, hp_f), stack(outs_p, 1, hp_f), stack(outs_p, 2, lambda a: a.reshape(bp, sp, n_heads)),
            stack(outs_p, 3, hp_f), stack(outs_p, 4, hp_f),
            stack(outs_s, 0, hs_f), stack(outs_s, 1, hs_f), stack(outs_s, 2, lambda a: a.reshape(bs, ss, n_heads)),
            stack(outs_s, 3, hs_f), stack(outs_s, 4, hs_f))
```

```python
import functools
import math

import jax
import jax.numpy as jnp
from jax import lax
from jax.experimental import pallas as pl
from jax.experimental.pallas import tpu as pltpu

F32 = jnp.float32
BF16 = jnp.bfloat16

HEAD_DIM = 128
TOP_K = 2
EPS = 1e-6
LANES = 128
MXU_DIM = 256
NEG_BIG = -1e30
VMEM_LIMIT = 56 * 1024 * 1024
LOG2E = math.log2(math.e)
SCALE = HEAD_DIM ** -0.5
SCALE_LOG2 = SCALE * LOG2E

ROW_TILE = 256
ATTN_TILE = 512
FOX_GROUP = 4
SB_GROUP = 2
EXPERT_BLOCK = 256
COMBINE_TILE = 128
ISSUE_UNROLL = 8


def _params(*sem):
    return pltpu.CompilerParams(dimension_semantics=sem, vmem_limit_bytes=VMEM_LIMIT)


def _log_sigmoid(x):
    return jnp.minimum(x, 0.0) - jnp.log1p(jnp.exp(-jnp.abs(x)))


def _rms(x, g):
    return x * lax.rsqrt(jnp.mean(x * x, axis=-1, keepdims=True) + EPS) * g


def _in_proj_kernel(x_ref, g_ref, wm_ref, wf_ref, bf_ref, lf_ref, *out_refs, width, n_heads):
    hn = _rms(x_ref[...], g_ref[...]).astype(BF16)
    refs = iter(out_refs)
    for c in range(6):
        res = jnp.dot(hn, wm_ref[:, c * width:(c + 1) * width], preferred_element_type=F32)
        next(refs)[...] = (res if c % 3 else res * SCALE_LOG2).astype(BF16)
        if c % 3:
            kv_ref = next(refs)
            kv_ref[...] = res.reshape(kv_ref.shape)
    fl = jnp.dot(hn, wf_ref[...], preferred_element_type=F32) + bf_ref[...]
    lf_ref[...] = _log_sigmoid(fl)[:, :n_heads]


def _in_proj(x, g, wm, wf, bf, n_heads):
    n, d = x.shape
    width = wm.shape[1] // 6
    tm = ROW_TILE
    row = lambda w: pl.BlockSpec((tm, w), lambda i: (i, 0))
    const = lambda a: pl.BlockSpec(a.shape, lambda i: (0, 0), pipeline_mode=pl.Buffered(1))
    per_head = pl.BlockSpec((tm, n_heads, HEAD_DIM), lambda i: (i, 0, 0))
    rows_out = jax.ShapeDtypeStruct((n, width), BF16)
    heads_out = jax.ShapeDtypeStruct((n, n_heads, HEAD_DIM), F32)
    mixer_specs = [row(width)] + [row(width), per_head] * 2
    mixer_shapes = [rows_out] + [rows_out, heads_out] * 2
    return pl.pallas_call(
        functools.partial(_in_proj_kernel, width=width, n_heads=n_heads),
        grid=(n // tm,),
        in_specs=[row(d), const(g), const(wm), const(wf), const(bf)],
        out_specs=[row(n_heads)] + mixer_specs * 2,
        out_shape=[jax.ShapeDtypeStruct((n, n_heads), F32)] + mixer_shapes * 2,
        compiler_params=_params("arbitrary"),
        name="in_proj",
    )(x, g, wm, wf, bf)


def _cumsum_kernel(lf_ref, cf_ref, carry_ref):
    tt = lf_ref.shape[0]

    @pl.when(pl.program_id(1) == 0)
    def _():
        carry_ref[...] = jnp.zeros_like(carry_ref)

    r = lax.broadcasted_iota(jnp.int32, (tt, tt), 0)
    c = lax.broadcasted_iota(jnp.int32, (tt, tt), 1)
    tri = jnp.where(c <= r, 1.0, 0.0).astype(F32)
    cf = jnp.dot(tri, lf_ref[...], precision=lax.Precision.HIGHEST,
                 preferred_element_type=F32) + carry_ref[...]
    cf_ref[...] = cf
    carry_ref[...] = cf[tt - 1:tt, :]


def _cumsum(lf, tt):
    b, t, h = lf.shape
    return pl.pallas_call(
        _cumsum_kernel,
        grid=(b, t // tt),
        in_specs=[pl.BlockSpec((None, tt, h), lambda i, j: (i, j, 0))],
        out_specs=pl.BlockSpec((None, tt, h), lambda i, j: (i, j, 0)),
        out_shape=jax.ShapeDtypeStruct((b, t, h), F32),
        scratch_shapes=[pltpu.VMEM((1, h), F32)],
        compiler_params=_params("arbitrary", "arbitrary"),
        name="cumsum",
    )(lf)


def _split3(x):
    hi = x.astype(BF16).astype(F32)
    r1 = x - hi
    mid = r1.astype(BF16).astype(F32)
    lo = (r1 - mid).astype(BF16).astype(F32)
    return hi, mid, lo


def _bias_block(c_rows, h, as_query):
    n = c_rows.shape[1]
    sub = lax.broadcasted_iota(jnp.int32, c_rows.shape, 0)
    c = jnp.sum(jnp.where(sub == h, c_rows, 0.0), axis=0, keepdims=True) * LOG2E
    hi, mid, lo = _split3(c)
    if as_query:
        parts, base, ones_lo = (hi, mid, lo), 0, 3
    else:
        parts, base, ones_lo = (-hi, -mid, -lo), 3, 0
    row = lax.broadcasted_iota(jnp.int32, (8, n), 0)
    blk = jnp.where((row >= ones_lo) & (row < ones_lo + 3), 1.0, 0.0)
    for i, p in enumerate(parts):
        blk = jnp.where(row == base + i, p, blk)
    full = jnp.concatenate([blk, jnp.zeros((LANES - 8, n), F32)], axis=0)
    return full.T.astype(BF16)


def _fox_kernel(q_ref, k_ref, v_ref, c_ref, gt_ref, o_ref, kaug, vt, *, tq, tk, q_off, prep, group, wide, one_tile):
    hg = pl.program_id(1)
    qi = pl.program_id(2)
    sk = k_ref.shape[0]
    hd = lambda g: slice(g * HEAD_DIM, (g + 1) * HEAD_DIM)

    @pl.when(qi == 0)
    def _():
        for g in range(group):
            for r0 in range(0, sk, prep):
                rows = pl.ds(r0, prep)
                kaug[g, rows, :HEAD_DIM] = k_ref[rows, hd(g)]
                kaug[g, rows, HEAD_DIM:] = _bias_block(c_ref[:, r0:r0 + prep], hg * group + g, False)
                vt[g, :, rows] = v_ref[rows, hd(g)].astype(F32).T.astype(BF16)

    q_lo = q_off if one_tile else pl.multiple_of(q_off + qi * tq, tq)
    c_q = c_ref[:, pl.ds(q_lo, tq)]
    qa = [jnp.concatenate([q_ref[:, hd(g)], _bias_block(c_q, hg * group + g, True)], axis=1) for g in range(group)]
    kiota = lax.broadcasted_iota(jnp.int32, (tk, tq), 0)
    qpos = q_lo + lax.broadcasted_iota(jnp.int32, (tk, tq), 1)

    def chunk(j, carry, width, masked=False):
        k0 = pl.multiple_of(j * width, width)

        def logits(g):
            s = lax.dot_general(kaug[g, pl.ds(k0, width), :], qa[g], (((1,), (1,)), ((), ())),
                                preferred_element_type=F32)
            if masked:
                s = jnp.where(k0 + kiota <= qpos, s, NEG_BIG)
            return s

        def absorb(g, s):
            m, l, acc = carry[g]
            m_new = jnp.maximum(m, jnp.max(s, axis=0, keepdims=True))
            p = jnp.exp2(s - m_new)
            alpha = jnp.exp2(m - m_new)
            l = alpha * l + jnp.sum(p, axis=0, keepdims=True)
            acc = alpha * acc + jnp.dot(vt[g, :, pl.ds(k0, width)], p.astype(BF16), preferred_element_type=F32)
            return m_new, l, acc

        out, s_prev = [], None
        for g in range(group + 1):
            s_next = logits(g) if g < group else None
            if g > 0:
                out.append(absorb(g - 1, s_prev))
            s_prev = s_next
        return tuple(out)

    n_full = (q_lo + 1) // tk
    n_all = (q_lo + tq + tk - 1) // tk
    carry = tuple((jnp.full((1, tq), NEG_BIG, F32), jnp.zeros((1, tq), F32), jnp.zeros((HEAD_DIM, tq), F32))
                  for _ in range(group))
    n_narrow = n_full
    if wide:
        carry = lax.fori_loop(0, n_full // 2, lambda j, c: chunk(j, c, 2 * tk), carry)
        n_narrow = n_full % 2
    carry = lax.fori_loop(n_full - n_narrow, n_full, lambda j, c: chunk(j, c, tk), carry)
    carry = lax.fori_loop(n_full, n_all, lambda j, c: chunk(j, c, tk, True), carry)
    for g in range(group):
        m, l, acc = carry[g]
        o = acc / l
        y = o * lax.rsqrt(jnp.mean(o * o, axis=0, keepdims=True) + EPS) * gt_ref[hd(g), :]
        o_ref[:, hd(g)] = y.T.astype(o_ref.dtype)


def _sb_kernel(q_ref, k_ref, v_ref, g_ref, o_ref, *, tq, tk, q_off, group, wide):
    qi = pl.program_id(2)
    hd = lambda g: slice(g * HEAD_DIM, (g + 1) * HEAD_DIM)
    q = [q_ref[:, hd(g)] for g in range(group)]
    q_lo = q_off + qi * tq
    ts = MXU_DIM if tk % MXU_DIM == 0 else LANES
    r = lax.broadcasted_iota(jnp.int32, (ts, ts), 0)
    c = lax.broadcasted_iota(jnp.int32, (ts, ts), 1)
    neg_from = jnp.where(r >= c, -1.0, 0.0).astype(BF16)
    sign_bit = jnp.uint32(0x80000000)

    def chunk(j, carry, width, masked=False):
        k0 = pl.multiple_of(j * width, width)
        n_sub = width // ts
        first_row = (lambda b: b * ts) if (masked and tq == tk) else (lambda b: 0)

        def logits(g, b):
            r0 = first_row(b)
            z = lax.dot_general(q[g][r0:], k_ref[pl.ds(k0 + b * ts, ts), hd(g)], (((1,), (1,)), ((), ())),
                                preferred_element_type=F32)
            neg_abs = pltpu.bitcast(pltpu.bitcast(z, jnp.uint32) | sign_bit, F32)
            sp = jnp.maximum(z, 0.0) + jnp.log2(1.0 + jnp.exp2(neg_abs))
            valid = None
            if masked:
                qpos = q_lo + r0 + lax.broadcasted_iota(jnp.int32, z.shape, 0)
                valid = k0 + b * ts + lax.broadcasted_iota(jnp.int32, z.shape, 1) < qpos
                sp = jnp.where(valid, sp, 0.0)
            hi = sp.astype(BF16)
            lo = (sp - hi.astype(F32)).astype(BF16)
            return z, hi, lo, valid

        def weights(state, after, r0):
            z, hi, lo, valid = state
            sfx = (jnp.dot(hi, neg_from, preferred_element_type=F32)
                   + jnp.dot(lo, neg_from, preferred_element_type=F32)) + after[r0:]
            a = jnp.exp2(z + sfx)
            if masked:
                a = jnp.where(valid, a, 0.0)
            total = sfx[:, 0:1]
            return (jnp.concatenate([after[:r0], total], axis=0) if r0 else total), a.astype(BF16)

        items = [(g, b) for b in reversed(range(n_sub)) for g in range(group)]
        after = [carry[g][0] for g in range(group)]
        acc = [carry[g][1] for g in range(group)]
        staged, ready = {}, {}
        n = len(items)
        for step in range(n + 2):
            if step < n:
                staged[step] = logits(*items[step])
            if 0 <= step - 1 < n:
                g, b = items[step - 1]
                after[g], ready[step - 1] = weights(staged.pop(step - 1), after[g], first_row(b))
            if 0 <= step - 2 < n:
                g, b = items[step - 2]
                r0 = first_row(b)
                rows = pl.ds(k0 + b * ts, ts)
                pv = jnp.dot(ready.pop(step - 2), v_ref[rows, hd(g)], preferred_element_type=F32)
                acc[g] = jnp.concatenate([acc[g][:r0], acc[g][r0:] + pv], axis=0) if r0 else acc[g] + pv
        return tuple((after[g], acc[g]) for g in range(group))

    n_full = q_lo // tk
    n_all = (q_lo + tq - 1 + tk - 1) // tk
    carry = tuple((jnp.zeros((tq, 1), F32), jnp.zeros((tq, HEAD_DIM), F32)) for _ in range(group))
    carry = lax.fori_loop(0, n_all - n_full, lambda i, c: chunk(n_all - 1 - i, c, tk, True), carry)
    n_narrow = n_full
    if wide:
        n_narrow = n_full % 2
    carry = lax.fori_loop(0, n_narrow, lambda i, c: chunk(n_full - 1 - i, c, tk), carry)
    if wide:
        carry = lax.fori_loop(0, n_full // 2, lambda i, c: chunk(n_full // 2 - 1 - i, c, 2 * tk), carry)
    for g in range(group):
        o_ref[:, hd(g)] = _rms(carry[g][1], g_ref[:, hd(g)]).astype(o_ref.dtype)


def _attention(kind, q, k, v, g, cf, *, tq, tk, q_off):
    b, sq, width = q.shape
    sk = k.shape[1]
    n_heads = width // HEAD_DIM
    group = FOX_GROUP if kind == "fox" else SB_GROUP
    assert sq % tq == 0 and sk % tk == 0 and q_off % tq == 0 and n_heads % group == 0
    w = group * HEAD_DIM
    wide = sk % (2 * tk) == 0
    q_spec = pl.BlockSpec((None, tq, w), lambda i, h, j: (i, j, h))
    kv_spec = pl.BlockSpec((None, sk, w), lambda i, h, j: (i, 0, h))
    g_spec = pl.BlockSpec((1, w), lambda i, h, j: (0, h))
    if kind == "fox":
        body = functools.partial(_fox_kernel, tq=tq, tk=tk, q_off=q_off, prep=math.gcd(sk, 512), group=group,
                                 wide=wide, one_tile=sq == tq)
        in_specs = [q_spec, kv_spec, kv_spec, pl.BlockSpec((None, n_heads, sk), lambda i, h, j: (i, 0, 0)),
                    pl.BlockSpec((w, 1), lambda i, h, j: (h, 0))]
        args = (q, k, v, cf, g.reshape(width, 1))
        scratch = [pltpu.VMEM((group, sk, 2 * HEAD_DIM), BF16), pltpu.VMEM((group, HEAD_DIM, sk), BF16)]
    else:
        body = functools.partial(_sb_kernel, tq=tq, tk=tk, q_off=q_off, group=group, wide=wide)
        in_specs = [q_spec, kv_spec, kv_spec, g_spec]
        args = (q, k, v, g)
        scratch = []
    return pl.pallas_call(
        body,
        grid=(b, n_heads // group, sq // tq),
        in_specs=in_specs,
        out_specs=q_spec,
        out_shape=jax.ShapeDtypeStruct((b, sq, width), BF16),
        scratch_shapes=scratch,
        compiler_params=_params("arbitrary", "arbitrary", "arbitrary"),
        name=kind,
    )(*args)


def _merge_kernel(nfp_ref, nsp_ref, xp_ref, nfs_ref, nss_ref, xs_ref, wo_ref, g_ref, wr_ref, br_ref,
                  h_ref, hn_ref, lg_ref, *, prompt_blocks):
    def body(nf_ref, ns_ref, x_ref):
        width = nf_ref.shape[1]
        h = x_ref[...] + (jnp.dot(nf_ref[...], wo_ref[:width, :], preferred_element_type=F32)
                          + jnp.dot(ns_ref[...], wo_ref[width:, :], preferred_element_type=F32))
        h_ref[...] = h
        hn = _rms(h, g_ref[...])
        hn_ref[...] = hn
        lg_ref[...] = jnp.dot(hn.astype(BF16), wr_ref[...], preferred_element_type=F32) + br_ref[...]

    is_prompt = pl.program_id(0) < prompt_blocks
    pl.when(is_prompt)(lambda: body(nfp_ref, nsp_ref, xp_ref))
    pl.when(jnp.logical_not(is_prompt))(lambda: body(nfs_ref, nss_ref, xs_ref))


def _merge(prompt, sample, wo, g, wr, br):
    n_p, d = prompt[2].shape
    n_s = sample[2].shape[0]
    width = prompt[0].shape[1]
    tm = ROW_TILE
    pb, sb = n_p // tm, n_s // tm
    n_total = n_p + n_s
    p_row = lambda w: pl.BlockSpec((tm, w), lambda i: (jnp.minimum(i, pb - 1), 0))
    s_row = lambda w: pl.BlockSpec((tm, w), lambda i: (jnp.maximum(i - pb, 0), 0))
    out_row = lambda w: pl.BlockSpec((tm, w), lambda i: (i, 0))
    const = lambda a: pl.BlockSpec(a.shape, lambda i: (0, 0), pipeline_mode=pl.Buffered(1))
    return pl.pallas_call(
        functools.partial(_merge_kernel, prompt_blocks=pb),
        grid=(pb + sb,),
        in_specs=[p_row(width), p_row(width), p_row(d), s_row(width), s_row(width), s_row(d),
                  const(wo), const(g), const(wr), const(br)],
        out_specs=[out_row(d), out_row(d), out_row(LANES)],
        out_shape=[jax.ShapeDtypeStruct((n_total, d), F32), jax.ShapeDtypeStruct((n_total, d), F32),
                   jax.ShapeDtypeStruct((n_total, LANES), F32)],
        compiler_params=_params("arbitrary"),
        name="merge",
    )(*prompt, *sample, wo, g, wr, br)


def _route_kernel(lg_ref, idx_ref, gate_ref, cnt_ref, carry_ref, *, n_groups, per_group):
    tr = lg_ref.shape[0]

    @pl.when(pl.program_id(0) == 0)
    def _():
        carry_ref[...] = jnp.zeros_like(carry_ref)

    x = lg_ref[...]
    lane = lax.broadcasted_iota(jnp.int32, x.shape, 1).astype(F32)
    far = float(LANES)
    rmax = lambda a: jnp.max(a, axis=1, keepdims=True)
    first = lambda hit: jnp.min(jnp.where(hit, lane, far), axis=1, keepdims=True)

    gmask = lane < n_groups
    gl = jnp.where(gmask, x, -jnp.inf)
    gmax = rmax(gl)
    gsel = first(gl == gmax)
    p_g = 1.0 / jnp.sum(jnp.where(gmask, jnp.exp(x - gmax), 0.0), axis=1, keepdims=True)
    lo = n_groups + gsel * per_group
    emask = (lane >= lo) & (lane < lo + per_group)
    el = jnp.where(emask, x, -jnp.inf)
    m1 = rmax(el)
    i1 = first(el == m1)
    el2 = jnp.where(lane == i1, -jnp.inf, el)
    m2 = rmax(el2)
    i2 = first(el2 == m2)
    zsum = jnp.sum(jnp.where(emask, jnp.exp(x - m1), 0.0), axis=1, keepdims=True)
    p1 = 1.0 / zsum
    p2 = jnp.exp(m2 - m1) / zsum
    psum = p1 + p2
    gate1 = p_g * p1 / psum
    gate2 = p_g * p2 / psum
    e1 = i1 - n_groups
    e2 = i2 - n_groups
    hit1 = lane == e1
    hit2 = lane == e2
    onehot = jnp.where(hit1 | hit2, 1.0, 0.0)
    r = lax.broadcasted_iota(jnp.int32, (tr, tr), 0)
    c = lax.broadcasted_iota(jnp.int32, (tr, tr), 1)
    before = jnp.where(c < r, 1.0, 0.0).astype(BF16)
    seen = jnp.dot(before, onehot.astype(BF16), preferred_element_type=F32) + carry_ref[...]
    r1 = jnp.sum(jnp.where(hit1, seen, 0.0), axis=1, keepdims=True)
    r2 = jnp.sum(jnp.where(hit2, seen, 0.0), axis=1, keepdims=True)
    carry_ref[...] = carry_ref[...] + jnp.sum(onehot, axis=0, keepdims=True)
    cnt_ref[...] = carry_ref[...]

    col = lax.broadcasted_iota(jnp.int32, idx_ref.shape, 1)
    idx = jnp.where(col == 0, e1, jnp.where(col == 1, e2, jnp.where(col == 2, r1, r2)))
    idx_ref[...] = idx.astype(jnp.int32)
    gate_ref[...] = jnp.where(col == 0, gate1, jnp.where(col == 1, gate2, 0.0))


def _route(logits, n_groups, per_group):
    n = logits.shape[0]
    tr = ROW_TILE
    narrow = pl.BlockSpec((tr, 8), lambda i: (i, 0))
    return pl.pallas_call(
        functools.partial(_route_kernel, n_groups=n_groups, per_group=per_group),
        grid=(n // tr,),
        in_specs=[pl.BlockSpec((tr, LANES), lambda i: (i, 0))],
        out_specs=[narrow, narrow, pl.BlockSpec((1, LANES), lambda i: (0, 0))],
        out_shape=[jax.ShapeDtypeStruct((n, 8), jnp.int32), jax.ShapeDtypeStruct((n, 8), F32),
                   jax.ShapeDtypeStruct((1, LANES), F32)],
        scratch_shapes=[pltpu.VMEM((1, LANES), F32)],
        compiler_params=_params("arbitrary"),
        name="route",
    )(logits)


def _row_copy(src_ref, src_row, dst_ref, dst_row, sem):
    return pltpu.make_async_copy(src_ref.at[pl.ds(src_row, 1)], dst_ref.at[pl.ds(dst_row, 1)], sem)


def _wait_rows(src_ref, dst_ref, sem, n):
    pltpu.make_async_copy(src_ref.at[pl.ds(0, n)], dst_ref.at[pl.ds(0, n)], sem).wait()


def _dispatch_kernel(dest_ref, hn_ref, xs_ref, sem):
    td = hn_ref.shape[0]

    def start(r, _):
        for k in range(TOP_K):
            _row_copy(hn_ref, r, xs_ref, dest_ref[0, 0, TOP_K * r + k], sem).start()
        return 0

    lax.fori_loop(0, td, start, 0, unroll=ISSUE_UNROLL)
    for _ in range(TOP_K):
        _wait_rows(hn_ref, xs_ref, sem, td)


def _dispatch(dest, hn):
    n, d = hn.shape
    td = ROW_TILE
    return pl.pallas_call(
        _dispatch_kernel,
        grid=(n // td,),
        in_specs=[pl.BlockSpec((1, 1, TOP_K * td), lambda i: (i, 0, 0), memory_space=pltpu.SMEM),
                  pl.BlockSpec((td, d), lambda i: (i, 0))],
        out_specs=pl.BlockSpec(memory_space=pl.ANY),
        out_shape=jax.ShapeDtypeStruct((n * TOP_K, d), F32),
        scratch_shapes=[pltpu.SemaphoreType.DMA(())],
        compiler_params=_params("arbitrary"),
        name="dispatch",
    )(dest.reshape(n // td, 1, TOP_K * td), hn)


def _expert_kernel(blk_ref, exp_ref, lo_ref, hi_ref, first_ref, xs_ref, w1_ref, w3_ref, w2_ref, o_ref):
    w = pl.program_id(0)
    lo, hi = lo_ref[w], hi_ref[w]

    def swiglu():
        rows = lax.broadcasted_iota(jnp.int32, xs_ref.shape, 0)
        x = jnp.where((rows >= lo) & (rows < hi), xs_ref[...], 0.0).astype(BF16)
        a = jnp.dot(x, w1_ref[...], preferred_element_type=F32)
        b = jnp.dot(x, w3_ref[...], preferred_element_type=F32)
        act = (a * jax.nn.sigmoid(a) * b).astype(BF16)
        return jnp.dot(act, w2_ref[...], preferred_element_type=F32)

    @pl.when((hi > lo) & (first_ref[w] == 1))
    def _():
        o_ref[...] = swiglu()

    @pl.when((hi > lo) & (first_ref[w] == 0))
    def _():
        o_ref[...] += swiglu()


def _experts(plan, xs, w1, w3, w2):
    m, d = xs.shape
    de = w1.shape[2]
    blk = EXPERT_BLOCK
    n_items = plan[0].shape[0]
    row_blk = lambda i, rb, ex, lo, hi, fs: (rb[i], 0)
    weight = lambda i, rb, ex, lo, hi, fs: (ex[i], 0, 0)
    return pl.pallas_call(
        _expert_kernel,
        grid_spec=pltpu.PrefetchScalarGridSpec(
            num_scalar_prefetch=5,
            grid=(n_items,),
            in_specs=[pl.BlockSpec((blk, d), row_blk),
                      pl.BlockSpec((None, d, de), weight),
                      pl.BlockSpec((None, d, de), weight),
                      pl.BlockSpec((None, de, d), weight)],
            out_specs=pl.BlockSpec((blk, d), row_blk)),
        out_shape=jax.ShapeDtypeStruct((m, d), F32),
        compiler_params=_params("arbitrary"),
        name="experts",
    )(*plan, xs, w1, w3, w2)


def _combine_kernel(dest_ref, next_ref, gate_ref, h_ref, rows_ref, g_ref, y_ref, buf, sem, *, final_norm):
    i = pl.program_id(0)
    tc = h_ref.shape[0]
    slot = i % 2

    def gather(idx_ref, into):
        def start(r, _):
            for k in range(TOP_K):
                _row_copy(rows_ref, idx_ref[0, 0, TOP_K * r + k], buf.at[into, k], r, sem.at[into]).start()
            return 0

        lax.fori_loop(0, tc, start, 0, unroll=ISSUE_UNROLL)

    pl.when(i == 0)(lambda: gather(dest_ref, 0))
    pl.when(i + 1 < pl.num_programs(0))(lambda: gather(next_ref, 1 - slot))
    for k in range(TOP_K):
        _wait_rows(rows_ref, buf.at[slot, k], sem.at[slot], tc)
    gates = gate_ref[...]
    y = gates[:, 0:1] * buf[slot, 0] + gates[:, 1:2] * buf[slot, 1]
    h = h_ref[...] + y
    y_ref[...] = _rms(h, g_ref[...]) if final_norm else h


def _combine(dest, gates, h, rows, g, n, row_off, final_norm):
    d = h.shape[1]
    tc = COMBINE_TILE
    off = row_off // tc
    last = off + n // tc - 1
    idx_spec = lambda f: pl.BlockSpec((1, 1, TOP_K * tc), lambda i: (f(i), 0, 0), memory_space=pltpu.SMEM)
    dest3 = dest.reshape(-1, 1, TOP_K * tc)
    return pl.pallas_call(
        functools.partial(_combine_kernel, final_norm=final_norm),
        grid=(n // tc,),
        in_specs=[idx_spec(lambda i: i + off), idx_spec(lambda i: jnp.minimum(i + off + 1, last)),
                  pl.BlockSpec((tc, 8), lambda i: (i + off, 0)),
                  pl.BlockSpec((tc, d), lambda i: (i + off, 0)),
                  pl.BlockSpec(memory_space=pl.ANY),
                  pl.BlockSpec((1, d), lambda i: (0, 0))],
        out_specs=pl.BlockSpec((tc, d), lambda i: (i, 0)),
        out_shape=jax.ShapeDtypeStruct((n, d), F32),
        scratch_shapes=[pltpu.VMEM((2, TOP_K, tc, d), F32), pltpu.SemaphoreType.DMA((2,))],
        compiler_params=_params("arbitrary"),
        name="combine",
    )(dest3, dest3, gates, h, rows, g)


def _round_up(a, m):
    return -(-a // m) * m


def _moe_plan(idx, counts, n_experts):
    blk = EXPERT_BLOCK
    n = idx.shape[0]
    i32 = lambda a: a.astype(jnp.int32)
    pick = lambda table, i: jnp.sum(jnp.where(i[..., None] == jnp.arange(table.shape[0]), table, 0), axis=-1)
    rank_of = lambda ends, x: jnp.sum(i32(ends[None, :] <= x[:, None]), axis=1)
    counts = i32(counts[0, :n_experts])
    end = jnp.cumsum(counts)
    start = end - counts
    dest = i32(pick(start, idx[:, :TOP_K]) + idx[:, TOP_K:2 * TOP_K])
    n_blocks = n * TOP_K // blk
    block_lo = jnp.arange(n_blocks, dtype=jnp.int32) * blk
    e_first = jnp.minimum(rank_of(end, block_lo), n_experts - 1)
    e_last = jnp.minimum(rank_of(end, block_lo + blk - 1), n_experts - 1)
    item_end = jnp.cumsum(e_last - e_first + 1)
    item_start = item_end - (e_last - e_first + 1)
    n_items = n_blocks + n_experts - 1
    w = jnp.arange(n_items, dtype=jnp.int32)
    live = w < item_end[-1]
    rb = jnp.minimum(rank_of(item_end, w), n_blocks - 1)
    ex = jnp.where(live, pick(e_first, rb) + (w - pick(item_start, rb)), e_last[-1])
    lo = jnp.where(live, jnp.clip(pick(start, ex) - rb * blk, 0, blk), 0)
    hi = jnp.where(live, jnp.clip(pick(end, ex) - rb * blk, 0, blk), 0)
    first = i32(w == pick(item_start, rb))
    return dest, (i32(rb), i32(ex), i32(lo), i32(hi), first)


def _split_w_in(w_in, width, n_heads):
    wm = jnp.concatenate([w_in[:, :3 * width], w_in[:, 3 * width + n_heads:]], axis=1).astype(BF16)
    wf = jnp.pad(w_in[:, 3 * width:3 * width + n_heads], ((0, 0), (0, LANES - n_heads))).astype(BF16)
    return wm, wf


def kernel(x_prompt, x_sample, cache_fox_k, cache_fox_v, cache_fox_logf, cache_sb_k, cache_sb_v,
           w_in, b_f, g_attn, g_out_fox, g_out_sb, w_out, g_ffn, w_group, b_group, w_router, b_router,
           w1, w3, w2, g_final):
    depth = w_in.shape[0]
    bp, sp, d = x_prompt.shape
    bs, ss, _ = x_sample.shape
    past = cache_fox_logf.shape[2]
    n_heads = cache_fox_logf.shape[3]
    width = n_heads * HEAD_DIM
    n_groups = w_group.shape[2]
    n_experts = w_router.shape[2]
    per_group = n_experts // n_groups
    n_p, n_s = bp * sp, bs * ss
    sk_s = _round_up(past + ss, LANES)
    pad_s = sk_s - past - ss
    row2 = lambda a: a.reshape(1, -1)

    hp = x_prompt.reshape(n_p, d)
    hs = x_sample.reshape(n_s, d)
    outs_p, outs_s = [], []
    for l in range(depth):
        wm, wf = _split_w_in(w_in[l], width, n_heads)
        bfp = jnp.pad(b_f[l], (0, LANES - n_heads)).reshape(1, LANES)
        wo = w_out[l].astype(BF16)
        wr = jnp.pad(jnp.concatenate([w_group[l], w_router[l]], axis=1),
                     ((0, 0), (0, LANES - n_groups - n_experts))).astype(BF16)
        br = jnp.pad(jnp.concatenate([b_group[l], b_router[l]]), (0, LANES - n_groups - n_experts)).reshape(1, LANES)
        g_a, g_f = row2(g_attn[l]), row2(g_ffn[l])
        g_of, g_os = row2(g_out_fox[l]), row2(g_out_sb[l])

        lf, qf, kfb, kf, vfb, vf, qs, ksb, ks, vsb, vs = _in_proj(hp, g_a, wm, wf, bfp, n_heads)
        b3 = lambda a: a.reshape(bp, sp, -1)
        cf = _cumsum(b3(lf), min(sp, 512))
        tq = min(sp, ATTN_TILE)
        nf = _attention("fox", b3(qf), b3(kfb), b3(vfb), g_of, jnp.swapaxes(cf, 1, 2), tq=tq, tk=tq, q_off=0)
        ns = _attention("sb", b3(qs), b3(ksb), b3(vsb), g_os, None, tq=tq, tk=tq, q_off=0)
        outs_p.append((kf, vf, lf, ks, vs))

        lf2, qf2, kfb2, kf2, vfb2, vf2, qs2, ksb2, ks2, vsb2, vs2 = _in_proj(hs, g_a, wm, wf, bfp, n_heads)
        c3 = lambda a: a.reshape(bs, ss, -1)
        keys = lambda cache, new: jnp.concatenate(
            [cache.reshape(bs, past, width).astype(BF16), c3(new), jnp.zeros((bs, pad_s, width), BF16)], axis=1)
        cf_all = _cumsum(jnp.concatenate([cache_fox_logf[l].astype(F32), c3(lf2)], axis=1), past + ss)
        cf_all = jnp.swapaxes(jnp.pad(cf_all, ((0, 0), (0, pad_s), (0, 0))), 1, 2)
        nf2 = _attention("fox", c3(qf2), keys(cache_fox_k[l], kfb2), keys(cache_fox_v[l], vfb2), g_of, cf_all,
                         tq=ss, tk=sk_s, q_off=past)
        ns2 = _attention("sb", c3(qs2), keys(cache_sb_k[l], ksb2), keys(cache_sb_v[l], vsb2), g_os, None,
                         tq=ss, tk=sk_s, q_off=past)
        outs_s.append((kf2, vf2, lf2, ks2, vs2))

        h_all, hn_all, logits = _merge((nf.reshape(n_p, width), ns.reshape(n_p, width), hp),
                                       (nf2.reshape(n_s, width), ns2.reshape(n_s, width), hs), wo, g_f, wr, br)
        idx, gates, counts = _route(logits, n_groups, per_group)
        dest, plan = _moe_plan(idx, counts, n_experts)
        xs = _dispatch(dest, hn_all)
        rows = _experts(plan, xs, w1[l].astype(BF16), w3[l].astype(BF16), w2[l].astype(BF16))
        last = l == depth - 1
        g_last = row2(g_final)
        hp = _combine(dest, gates, h_all, rows, g_last, n_p, 0, last)
        hs = _combine(dest, gates, h_all, rows, g_last, n_s, n_p, last)

    heads = lambda a, b, s: a.reshape(b, s, n_heads, HEAD_DIM)
    stack = lambda outs, i, f: jnp.stack([f(o[i]) for o in outs])
    hp_f = lambda a: heads(a, bp, sp)
    hs_f = lambda a: heads(a, bs, ss)
    return (hp.reshape(bp, sp, d), hs.reshape(bs, ss, d),
            stack(outs_p, 0, hp_f), stack(outs_p, 1, hp_f), stack(outs_p, 2, lambda a: a.reshape(bp, sp, n_heads)),
            stack(outs_p, 3, hp_f), stack(outs_p, 4, hp_f),
            stack(outs_s, 0, hs_f), stack(outs_s, 1, hs_f), stack(outs_s, 2, lambda a: a.reshape(bs, ss, n_heads)),
            stack(outs_s, 3, hs_f), stack(outs_s, 4, hs_f))
```

```python
import functools
import math

import jax
import jax.numpy as jnp
from jax import lax
from jax.experimental import pallas as pl
from jax.experimental.pallas import tpu as pltpu

F32 = jnp.float32
BF16 = jnp.bfloat16

HEAD_DIM = 128
TOP_K = 2
EPS = 1e-6
LANES = 128
MXU_DIM = 256
NEG_BIG = -1e30
VMEM_LIMIT = 56 * 1024 * 1024
LOG2E = math.log2(math.e)
SCALE = HEAD_DIM ** -0.5
SCALE_LOG2 = SCALE * LOG2E

ROW_TILE = 256
ATTN_TILE = 512
FOX_GROUP = 4
SB_GROUP = 4
LOOKAHEAD = 2
EXPERT_BLOCK = 256
COMBINE_TILE = 128
ISSUE_UNROLL = 8


def _params(*sem):
    return pltpu.CompilerParams(dimension_semantics=sem, vmem_limit_bytes=VMEM_LIMIT)


def _log_sigmoid(x):
    return jnp.minimum(x, 0.0) - jnp.log1p(jnp.exp(-jnp.abs(x)))


def _rms(x, g):
    return x * lax.rsqrt(jnp.mean(x * x, axis=-1, keepdims=True) + EPS) * g


def _in_proj_kernel(x_ref, g_ref, wm_ref, wf_ref, bf_ref, lf_ref, *out_refs, width, n_heads):
    hn = _rms(x_ref[...], g_ref[...]).astype(BF16)
    refs = iter(out_refs)
    for c in range(6):
        res = jnp.dot(hn, wm_ref[:, c * width:(c + 1) * width], preferred_element_type=F32)
        next(refs)[...] = (res if c % 3 else res * SCALE_LOG2).astype(BF16)
        if c % 3:
            kv_ref = next(refs)
            kv_ref[...] = res.reshape(kv_ref.shape)
    fl = jnp.dot(hn, wf_ref[...], preferred_element_type=F32) + bf_ref[...]
    lf_ref[...] = _log_sigmoid(fl)[:, :n_heads]


def _in_proj(x, g, wm, wf, bf, n_heads):
    n, d = x.shape
    width = wm.shape[1] // 6
    tm = ROW_TILE
    row = lambda w: pl.BlockSpec((tm, w), lambda i: (i, 0))
    const = lambda a: pl.BlockSpec(a.shape, lambda i: (0, 0), pipeline_mode=pl.Buffered(1))
    per_head = pl.BlockSpec((tm, n_heads, HEAD_DIM), lambda i: (i, 0, 0))
    rows_out = jax.ShapeDtypeStruct((n, width), BF16)
    heads_out = jax.ShapeDtypeStruct((n, n_heads, HEAD_DIM), F32)
    mixer_specs = [row(width)] + [row(width), per_head] * 2
    mixer_shapes = [rows_out] + [rows_out, heads_out] * 2
    return pl.pallas_call(
        functools.partial(_in_proj_kernel, width=width, n_heads=n_heads),
        grid=(n // tm,),
        in_specs=[row(d), const(g), const(wm), const(wf), const(bf)],
        out_specs=[row(n_heads)] + mixer_specs * 2,
        out_shape=[jax.ShapeDtypeStruct((n, n_heads), F32)] + mixer_shapes * 2,
        compiler_params=_params("arbitrary"),
        name="in_proj",
    )(x, g, wm, wf, bf)


def _cumsum_kernel(lf_ref, cf_ref, carry_ref):
    tt = lf_ref.shape[0]

    @pl.when(pl.program_id(1) == 0)
    def _():
        carry_ref[...] = jnp.zeros_like(carry_ref)

    r = lax.broadcasted_iota(jnp.int32, (tt, tt), 0)
    c = lax.broadcasted_iota(jnp.int32, (tt, tt), 1)
    tri = jnp.where(c <= r, 1.0, 0.0).astype(F32)
    cf = jnp.dot(tri, lf_ref[...], precision=lax.Precision.HIGHEST,
                 preferred_element_type=F32) + carry_ref[...]
    cf_ref[...] = cf
    carry_ref[...] = cf[tt - 1:tt, :]


def _cumsum(lf, tt):
    b, t, h = lf.shape
    return pl.pallas_call(
        _cumsum_kernel,
        grid=(b, t // tt),
        in_specs=[pl.BlockSpec((None, tt, h), lambda i, j: (i, j, 0))],
        out_specs=pl.BlockSpec((None, tt, h), lambda i, j: (i, j, 0)),
        out_shape=jax.ShapeDtypeStruct((b, t, h), F32),
        scratch_shapes=[pltpu.VMEM((1, h), F32)],
        compiler_params=_params("arbitrary", "arbitrary"),
        name="cumsum",
    )(lf)


def _split3(x):
    hi = x.astype(BF16).astype(F32)
    r1 = x - hi
    mid = r1.astype(BF16).astype(F32)
    lo = (r1 - mid).astype(BF16).astype(F32)
    return hi, mid, lo


def _bias_block(c_rows, h, as_query):
    n = c_rows.shape[1]
    sub = lax.broadcasted_iota(jnp.int32, c_rows.shape, 0)
    c = jnp.sum(jnp.where(sub == h, c_rows, 0.0), axis=0, keepdims=True) * LOG2E
    hi, mid, lo = _split3(c)
    if as_query:
        parts, base, ones_lo = (hi, mid, lo), 0, 3
    else:
        parts, base, ones_lo = (-hi, -mid, -lo), 3, 0
    row = lax.broadcasted_iota(jnp.int32, (8, n), 0)
    blk = jnp.where((row >= ones_lo) & (row < ones_lo + 3), 1.0, 0.0)
    for i, p in enumerate(parts):
        blk = jnp.where(row == base + i, p, blk)
    full = jnp.concatenate([blk, jnp.zeros((LANES - 8, n), F32)], axis=0)
    return full.T.astype(BF16)


def _fox_kernel(q_ref, k_ref, v_ref, c_ref, gt_ref, o_ref, kaug, vt, *, tq, tk, q_off, prep, group, wide, one_tile):
    hg = pl.program_id(1)
    qi = pl.program_id(2)
    sk = k_ref.shape[0]
    hd = lambda g: slice(g * HEAD_DIM, (g + 1) * HEAD_DIM)

    @pl.when(qi == 0)
    def _():
        for g in range(group):
            for r0 in range(0, sk, prep):
                rows = pl.ds(r0, prep)
                kaug[g, rows, :HEAD_DIM] = k_ref[rows, hd(g)]
                kaug[g, rows, HEAD_DIM:] = _bias_block(c_ref[:, r0:r0 + prep], hg * group + g, False)
                vt[g, :, rows] = v_ref[rows, hd(g)].astype(F32).T.astype(BF16)

    q_lo = q_off if one_tile else pl.multiple_of(q_off + qi * tq, tq)
    c_q = c_ref[:, pl.ds(q_lo, tq)]
    qa = [jnp.concatenate([q_ref[:, hd(g)], _bias_block(c_q, hg * group + g, True)], axis=1) for g in range(group)]
    kiota = lax.broadcasted_iota(jnp.int32, (tk, tq), 0)
    qpos = q_lo + lax.broadcasted_iota(jnp.int32, (tk, tq), 1)

    def chunk(j, carry, width, masked=False):
        k0 = pl.multiple_of(j * width, width)

        def logits(g):
            s = lax.dot_general(kaug[g, pl.ds(k0, width), :], qa[g], (((1,), (1,)), ((), ())),
                                preferred_element_type=F32)
            if masked:
                s = jnp.where(k0 + kiota <= qpos, s, NEG_BIG)
            return s

        def absorb(g, s):
            m, l, acc = carry[g]
            m_new = jnp.maximum(m, jnp.max(s, axis=0, keepdims=True))
            p = jnp.exp2(s - m_new)
            alpha = jnp.exp2(m - m_new)
            l = alpha * l + jnp.sum(p, axis=0, keepdims=True)
            acc = alpha * acc + jnp.dot(vt[g, :, pl.ds(k0, width)], p.astype(BF16), preferred_element_type=F32)
            return m_new, l, acc

        out, pending = [], {}
        for g in range(group + LOOKAHEAD):
            if g < group:
                pending[g] = logits(g)
            if g >= LOOKAHEAD:
                out.append(absorb(g - LOOKAHEAD, pending.pop(g - LOOKAHEAD)))
        return tuple(out)

    n_full = (q_lo + 1) // tk
    n_all = (q_lo + tq + tk - 1) // tk
    carry = tuple((jnp.full((1, tq), NEG_BIG, F32), jnp.zeros((1, tq), F32), jnp.zeros((HEAD_DIM, tq), F32))
                  for _ in range(group))
    n_narrow = n_full
    if wide:
        carry = lax.fori_loop(0, n_full // 2, lambda j, c: chunk(j, c, 2 * tk), carry)
        n_narrow = n_full % 2
    carry = lax.fori_loop(n_full - n_narrow, n_full, lambda j, c: chunk(j, c, tk), carry)
    carry = lax.fori_loop(n_full, n_all, lambda j, c: chunk(j, c, tk, True), carry)
    for g in range(group):
        m, l, acc = carry[g]
        o = acc / l
        y = o * lax.rsqrt(jnp.mean(o * o, axis=0, keepdims=True) + EPS) * gt_ref[hd(g), :]
        o_ref[:, hd(g)] = y.T.astype(o_ref.dtype)


def _sb_kernel(q_ref, k_ref, v_ref, g_ref, o_ref, *, tq, tk, q_off, group, wide):
    qi = pl.program_id(2)
    hd = lambda g: slice(g * HEAD_DIM, (g + 1) * HEAD_DIM)
    q = [q_ref[:, hd(g)] for g in range(group)]
    q_lo = q_off + qi * tq
    ts = MXU_DIM if tk % MXU_DIM == 0 else LANES
    r = lax.broadcasted_iota(jnp.int32, (ts, ts), 0)
    c = lax.broadcasted_iota(jnp.int32, (ts, ts), 1)
    neg_from = jnp.where(r >= c, -1.0, 0.0).astype(BF16)
    sign_bit = jnp.uint32(0x80000000)

    def chunk(j, carry, width, masked=False):
        k0 = pl.multiple_of(j * width, width)
        n_sub = width // ts
        first_row = (lambda b: b * ts) if (masked and tq == tk) else (lambda b: 0)

        def logits(g, b):
            r0 = first_row(b)
            z = lax.dot_general(q[g][r0:], k_ref[pl.ds(k0 + b * ts, ts), hd(g)], (((1,), (1,)), ((), ())),
                                preferred_element_type=F32)
            neg_abs = pltpu.bitcast(pltpu.bitcast(z, jnp.uint32) | sign_bit, F32)
            sp = jnp.maximum(z, 0.0) + jnp.log2(1.0 + jnp.exp2(neg_abs))
            valid = None
            if masked:
                qpos = q_lo + r0 + lax.broadcasted_iota(jnp.int32, z.shape, 0)
                valid = k0 + b * ts + lax.broadcasted_iota(jnp.int32, z.shape, 1) < qpos
                sp = jnp.where(valid, sp, 0.0)
            hi = sp.astype(BF16)
            lo = (sp - hi.astype(F32)).astype(BF16)
            return z, hi, lo, valid

        def weights(state, after, r0):
            z, hi, lo, valid = state
            sfx = (jnp.dot(hi, neg_from, preferred_element_type=F32)
                   + jnp.dot(lo, neg_from, preferred_element_type=F32)) + after[r0:]
            a = jnp.exp2(z + sfx)
            if masked:
                a = jnp.where(valid, a, 0.0)
            total = sfx[:, 0:1]
            return (jnp.concatenate([after[:r0], total], axis=0) if r0 else total), a.astype(BF16)

        items = [(g, b) for b in reversed(range(n_sub)) for g in range(group)]
        after = [carry[g][0] for g in range(group)]
        acc = [carry[g][1] for g in range(group)]
        staged, ready = {}, {}
        n = len(items)
        for step in range(n + 2 * LOOKAHEAD):
            if step < n:
                staged[step] = logits(*items[step])
            if 0 <= step - LOOKAHEAD < n:
                g, b = items[step - LOOKAHEAD]
                after[g], ready[step - LOOKAHEAD] = weights(staged.pop(step - LOOKAHEAD), after[g], first_row(b))
            if 0 <= step - 2 * LOOKAHEAD < n:
                g, b = items[step - 2 * LOOKAHEAD]
                r0 = first_row(b)
                rows = pl.ds(k0 + b * ts, ts)
                pv = jnp.dot(ready.pop(step - 2 * LOOKAHEAD), v_ref[rows, hd(g)], preferred_element_type=F32)
                acc[g] = jnp.concatenate([acc[g][:r0], acc[g][r0:] + pv], axis=0) if r0 else acc[g] + pv
        return tuple((after[g], acc[g]) for g in range(group))

    n_full = q_lo // tk
    n_all = (q_lo + tq - 1 + tk - 1) // tk
    carry = tuple((jnp.zeros((tq, 1), F32), jnp.zeros((tq, HEAD_DIM), F32)) for _ in range(group))
    carry = lax.fori_loop(0, n_all - n_full, lambda i, c: chunk(n_all - 1 - i, c, tk, True), carry)
    n_narrow = n_full
    if wide:
        n_narrow = n_full % 2
    carry = lax.fori_loop(0, n_narrow, lambda i, c: chunk(n_full - 1 - i, c, tk), carry)
    if wide:
        carry = lax.fori_loop(0, n_full // 2, lambda i, c: chunk(n_full // 2 - 1 - i, c, 2 * tk), carry)
    for g in range(group):
        o_ref[:, hd(g)] = _rms(carry[g][1], g_ref[:, hd(g)]).astype(o_ref.dtype)


def _attention(kind, q, k, v, g, cf, *, tq, tk, q_off):
    b, sq, width = q.shape
    sk = k.shape[1]
    n_heads = width // HEAD_DIM
    group = FOX_GROUP if kind == "fox" else SB_GROUP
    assert sq % tq == 0 and sk % tk == 0 and q_off % tq == 0 and n_heads % group == 0
    w = group * HEAD_DIM
    wide = sk % (2 * tk) == 0
    q_spec = pl.BlockSpec((None, tq, w), lambda i, h, j: (i, j, h))
    kv_spec = pl.BlockSpec((None, sk, w), lambda i, h, j: (i, 0, h))
    g_spec = pl.BlockSpec((1, w), lambda i, h, j: (0, h))
    if kind == "fox":
        body = functools.partial(_fox_kernel, tq=tq, tk=tk, q_off=q_off, prep=math.gcd(sk, 512), group=group,
                                 wide=wide, one_tile=sq == tq)
        in_specs = [q_spec, kv_spec, kv_spec, pl.BlockSpec((None, n_heads, sk), lambda i, h, j: (i, 0, 0)),
                    pl.BlockSpec((w, 1), lambda i, h, j: (h, 0))]
        args = (q, k, v, cf, g.reshape(width, 1))
        scratch = [pltpu.VMEM((group, sk, 2 * HEAD_DIM), BF16), pltpu.VMEM((group, HEAD_DIM, sk), BF16)]
    else:
        body = functools.partial(_sb_kernel, tq=tq, tk=tk, q_off=q_off, group=group, wide=wide)
        in_specs = [q_spec, kv_spec, kv_spec, g_spec]
        args = (q, k, v, g)
        scratch = []
    return pl.pallas_call(
        body,
        grid=(b, n_heads // group, sq // tq),
        in_specs=in_specs,
        out_specs=q_spec,
        out_shape=jax.ShapeDtypeStruct((b, sq, width), BF16),
        scratch_shapes=scratch,
        compiler_params=_params("arbitrary", "arbitrary", "arbitrary"),
        name=kind,
    )(*args)


def _merge_kernel(nfp_ref, nsp_ref, xp_ref, nfs_ref, nss_ref, xs_ref, wo_ref, g_ref, wr_ref, br_ref,
                  h_ref, hn_ref, lg_ref, *, prompt_blocks):
    def body(nf_ref, ns_ref, x_ref):
        width = nf_ref.shape[1]
        h = x_ref[...] + (jnp.dot(nf_ref[...], wo_ref[:width, :], preferred_element_type=F32)
                          + jnp.dot(ns_ref[...], wo_ref[width:, :], preferred_element_type=F32))
        h_ref[...] = h
        hn = _rms(h, g_ref[...])
        hn_ref[...] = hn
        lg_ref[...] = jnp.dot(hn.astype(BF16), wr_ref[...], preferred_element_type=F32) + br_ref[...]

    is_prompt = pl.program_id(0) < prompt_blocks
    pl.when(is_prompt)(lambda: body(nfp_ref, nsp_ref, xp_ref))
    pl.when(jnp.logical_not(is_prompt))(lambda: body(nfs_ref, nss_ref, xs_ref))


def _merge(prompt, sample, wo, g, wr, br):
    n_p, d = prompt[2].shape
    n_s = sample[2].shape[0]
    width = prompt[0].shape[1]
    tm = ROW_TILE
    pb, sb = n_p // tm, n_s // tm
    n_total = n_p + n_s
    p_row = lambda w: pl.BlockSpec((tm, w), lambda i: (jnp.minimum(i, pb - 1), 0))
    s_row = lambda w: pl.BlockSpec((tm, w), lambda i: (jnp.maximum(i - pb, 0), 0))
    out_row = lambda w: pl.BlockSpec((tm, w), lambda i: (i, 0))
    const = lambda a: pl.BlockSpec(a.shape, lambda i: (0, 0), pipeline_mode=pl.Buffered(1))
    return pl.pallas_call(
        functools.partial(_merge_kernel, prompt_blocks=pb),
        grid=(pb + sb,),
        in_specs=[p_row(width), p_row(width), p_row(d), s_row(width), s_row(width), s_row(d),
                  const(wo), const(g), const(wr), const(br)],
        out_specs=[out_row(d), out_row(d), out_row(LANES)],
        out_shape=[jax.ShapeDtypeStruct((n_total, d), F32), jax.ShapeDtypeStruct((n_total, d), F32),
                   jax.ShapeDtypeStruct((n_total, LANES), F32)],
        compiler_params=_params("arbitrary"),
        name="merge",
    )(*prompt, *sample, wo, g, wr, br)


def _route_kernel(lg_ref, idx_ref, gate_ref, cnt_ref, carry_ref, *, n_groups, per_group):
    tr = lg_ref.shape[0]

    @pl.when(pl.program_id(0) == 0)
    def _():
        carry_ref[...] = jnp.zeros_like(carry_ref)

    x = lg_ref[...]
    lane = lax.broadcasted_iota(jnp.int32, x.shape, 1).astype(F32)
    far = float(LANES)
    rmax = lambda a: jnp.max(a, axis=1, keepdims=True)
    first = lambda hit: jnp.min(jnp.where(hit, lane, far), axis=1, keepdims=True)

    gmask = lane < n_groups
    gl = jnp.where(gmask, x, -jnp.inf)
    gmax = rmax(gl)
    gsel = first(gl == gmax)
    p_g = 1.0 / jnp.sum(jnp.where(gmask, jnp.exp(x - gmax), 0.0), axis=1, keepdims=True)
    lo = n_groups + gsel * per_group
    emask = (lane >= lo) & (lane < lo + per_group)
    el = jnp.where(emask, x, -jnp.inf)
    m1 = rmax(el)
    i1 = first(el == m1)
    el2 = jnp.where(lane == i1, -jnp.inf, el)
    m2 = rmax(el2)
    i2 = first(el2 == m2)
    zsum = jnp.sum(jnp.where(emask, jnp.exp(x - m1), 0.0), axis=1, keepdims=True)
    p1 = 1.0 / zsum
    p2 = jnp.exp(m2 - m1) / zsum
    psum = p1 + p2
    gate1 = p_g * p1 / psum
    gate2 = p_g * p2 / psum
    e1 = i1 - n_groups
    e2 = i2 - n_groups
    hit1 = lane == e1
    hit2 = lane == e2
    onehot = jnp.where(hit1 | hit2, 1.0, 0.0)
    r = lax.broadcasted_iota(jnp.int32, (tr, tr), 0)
    c = lax.broadcasted_iota(jnp.int32, (tr, tr), 1)
    before = jnp.where(c < r, 1.0, 0.0).astype(BF16)
    seen = jnp.dot(before, onehot.astype(BF16), preferred_element_type=F32) + carry_ref[...]
    r1 = jnp.sum(jnp.where(hit1, seen, 0.0), axis=1, keepdims=True)
    r2 = jnp.sum(jnp.where(hit2, seen, 0.0), axis=1, keepdims=True)
    carry_ref[...] = carry_ref[...] + jnp.sum(onehot, axis=0, keepdims=True)
    cnt_ref[...] = carry_ref[...]

    col = lax.broadcasted_iota(jnp.int32, idx_ref.shape, 1)
    idx = jnp.where(col == 0, e1, jnp.where(col == 1, e2, jnp.where(col == 2, r1, r2)))
    idx_ref[...] = idx.astype(jnp.int32)
    gate_ref[...] = jnp.where(col == 0, gate1, jnp.where(col == 1, gate2, 0.0))


def _route(logits, n_groups, per_group):
    n = logits.shape[0]
    tr = ROW_TILE
    narrow = pl.BlockSpec((tr, 8), lambda i: (i, 0))
    return pl.pallas_call(
        functools.partial(_route_kernel, n_groups=n_groups, per_group=per_group),
        grid=(n // tr,),
        in_specs=[pl.BlockSpec((tr, LANES), lambda i: (i, 0))],
        out_specs=[narrow, narrow, pl.BlockSpec((1, LANES), lambda i: (0, 0))],
        out_shape=[jax.ShapeDtypeStruct((n, 8), jnp.int32), jax.ShapeDtypeStruct((n, 8), F32),
                   jax.ShapeDtypeStruct((1, LANES), F32)],
        scratch_shapes=[pltpu.VMEM((1, LANES), F32)],
        compiler_params=_params("arbitrary"),
        name="route",
    )(logits)


def _row_copy(src_ref, src_row, dst_ref, dst_row, sem):
    return pltpu.make_async_copy(src_ref.at[pl.ds(src_row, 1)], dst_ref.at[pl.ds(dst_row, 1)], sem)


def _wait_rows(src_ref, dst_ref, sem, n):
    pltpu.make_async_copy(src_ref.at[pl.ds(0, n)], dst_ref.at[pl.ds(0, n)], sem).wait()


def _dispatch_kernel(dest_ref, hn_ref, xs_ref, sem):
    td = hn_ref.shape[0]

    def start(r, _):
        for k in range(TOP_K):
            _row_copy(hn_ref, r, xs_ref, dest_ref[0, 0, TOP_K * r + k], sem).start()
        return 0

    lax.fori_loop(0, td, start, 0, unroll=ISSUE_UNROLL)
    for _ in range(TOP_K):
        _wait_rows(hn_ref, xs_ref, sem, td)


def _dispatch(dest, hn):
    n, d = hn.shape
    td = ROW_TILE
    return pl.pallas_call(
        _dispatch_kernel,
        grid=(n // td,),
        in_specs=[pl.BlockSpec((1, 1, TOP_K * td), lambda i: (i, 0, 0), memory_space=pltpu.SMEM),
                  pl.BlockSpec((td, d), lambda i: (i, 0))],
        out_specs=pl.BlockSpec(memory_space=pl.ANY),
        out_shape=jax.ShapeDtypeStruct((n * TOP_K, d), F32),
        scratch_shapes=[pltpu.SemaphoreType.DMA(())],
        compiler_params=_params("arbitrary"),
        name="dispatch",
    )(dest.reshape(n // td, 1, TOP_K * td), hn)


def _expert_kernel(blk_ref, exp_ref, lo_ref, hi_ref, first_ref, xs_ref, w1_ref, w3_ref, w2_ref, o_ref):
    w = pl.program_id(0)
    lo, hi = lo_ref[w], hi_ref[w]

    def swiglu():
        rows = lax.broadcasted_iota(jnp.int32, xs_ref.shape, 0)
        x = jnp.where((rows >= lo) & (rows < hi), xs_ref[...], 0.0).astype(BF16)
        a = jnp.dot(x, w1_ref[...], preferred_element_type=F32)
        b = jnp.dot(x, w3_ref[...], preferred_element_type=F32)
        act = (a * jax.nn.sigmoid(a) * b).astype(BF16)
        return jnp.dot(act, w2_ref[...], preferred_element_type=F32)

    @pl.when((hi > lo) & (first_ref[w] == 1))
    def _():
        o_ref[...] = swiglu()

    @pl.when((hi > lo) & (first_ref[w] == 0))
    def _():
        o_ref[...] += swiglu()


def _experts(plan, xs, w1, w3, w2):
    m, d = xs.shape
    de = w1.shape[2]
    blk = EXPERT_BLOCK
    n_items = plan[0].shape[0]
    row_blk = lambda i, rb, ex, lo, hi, fs: (rb[i], 0)
    weight = lambda i, rb, ex, lo, hi, fs: (ex[i], 0, 0)
    return pl.pallas_call(
        _expert_kernel,
        grid_spec=pltpu.PrefetchScalarGridSpec(
            num_scalar_prefetch=5,
            grid=(n_items,),
            in_specs=[pl.BlockSpec((blk, d), row_blk),
                      pl.BlockSpec((None, d, de), weight),
                      pl.BlockSpec((None, d, de), weight),
                      pl.BlockSpec((None, de, d), weight)],
            out_specs=pl.BlockSpec((blk, d), row_blk)),
        out_shape=jax.ShapeDtypeStruct((m, d), F32),
        compiler_params=_params("arbitrary"),
        name="experts",
    )(*plan, xs, w1, w3, w2)


def _combine_kernel(dest_ref, next_ref, gate_ref, h_ref, rows_ref, g_ref, y_ref, buf, sem, *, final_norm):
    i = pl.program_id(0)
    tc = h_ref.shape[0]
    slot = i % 2

    def gather(idx_ref, into):
        def start(r, _):
            for k in range(TOP_K):
                _row_copy(rows_ref, idx_ref[0, 0, TOP_K * r + k], buf.at[into, k], r, sem.at[into]).start()
            return 0

        lax.fori_loop(0, tc, start, 0, unroll=ISSUE_UNROLL)

    pl.when(i == 0)(lambda: gather(dest_ref, 0))
    pl.when(i + 1 < pl.num_programs(0))(lambda: gather(next_ref, 1 - slot))
    for k in range(TOP_K):
        _wait_rows(rows_ref, buf.at[slot, k], sem.at[slot], tc)
    gates = gate_ref[...]
    y = gates[:, 0:1] * buf[slot, 0] + gates[:, 1:2] * buf[slot, 1]
    h = h_ref[...] + y
    y_ref[...] = _rms(h, g_ref[...]) if final_norm else h


def _combine(dest, gates, h, rows, g, n, row_off, final_norm):
    d = h.shape[1]
    tc = COMBINE_TILE
    off = row_off // tc
    last = off + n // tc - 1
    idx_spec = lambda f: pl.BlockSpec((1, 1, TOP_K * tc), lambda i: (f(i), 0, 0), memory_space=pltpu.SMEM)
    dest3 = dest.reshape(-1, 1, TOP_K * tc)
    return pl.pallas_call(
        functools.partial(_combine_kernel, final_norm=final_norm),
        grid=(n // tc,),
        in_specs=[idx_spec(lambda i: i + off), idx_spec(lambda i: jnp.minimum(i + off + 1, last)),
                  pl.BlockSpec((tc, 8), lambda i: (i + off, 0)),
                  pl.BlockSpec((tc, d), lambda i: (i + off, 0)),
                  pl.BlockSpec(memory_space=pl.ANY),
                  pl.BlockSpec((1, d), lambda i: (0, 0))],
        out_specs=pl.BlockSpec((tc, d), lambda i: (i, 0)),
        out_shape=jax.ShapeDtypeStruct((n, d), F32),
        scratch_shapes=[pltpu.VMEM((2, TOP_K, tc, d), F32), pltpu.SemaphoreType.DMA((2,))],
        compiler_params=_params("arbitrary"),
        name="combine",
    )(dest3, dest3, gates, h, rows, g)


def _round_up(a, m):
    return -(-a // m) * m


def _moe_plan(idx, counts, n_experts):
    blk = EXPERT_BLOCK
    n = idx.shape[0]
    i32 = lambda a: a.astype(jnp.int32)
    pick = lambda table, i: jnp.sum(jnp.where(i[..., None] == jnp.arange(table.shape[0]), table, 0), axis=-1)
    rank_of = lambda ends, x: jnp.sum(i32(ends[None, :] <= x[:, None]), axis=1)
    counts = i32(counts[0, :n_experts])
    end = jnp.cumsum(counts)
    start = end - counts
    dest = i32(pick(start, idx[:, :TOP_K]) + idx[:, TOP_K:2 * TOP_K])
    n_blocks = n * TOP_K // blk
    block_lo = jnp.arange(n_blocks, dtype=jnp.int32) * blk
    e_first = jnp.minimum(rank_of(end, block_lo), n_experts - 1)
    e_last = jnp.minimum(rank_of(end, block_lo + blk - 1), n_experts - 1)
    item_end = jnp.cumsum(e_last - e_first + 1)
    item_start = item_end - (e_last - e_first + 1)
    n_items = n_blocks + n_experts - 1
    w = jnp.arange(n_items, dtype=jnp.int32)
    live = w < item_end[-1]
    rb = jnp.minimum(rank_of(item_end, w), n_blocks - 1)
    ex = jnp.where(live, pick(e_first, rb) + (w - pick(item_start, rb)), e_last[-1])
    lo = jnp.where(live, jnp.clip(pick(start, ex) - rb * blk, 0, blk), 0)
    hi = jnp.where(live, jnp.clip(pick(end, ex) - rb * blk, 0, blk), 0)
    first = i32(w == pick(item_start, rb))
    return dest, (i32(rb), i32(ex), i32(lo), i32(hi), first)


def _split_w_in(w_in, width, n_heads):
    wm = jnp.concatenate([w_in[:, :3 * width], w_in[:, 3 * width + n_heads:]], axis=1).astype(BF16)
    wf = jnp.pad(w_in[:, 3 * width:3 * width + n_heads], ((0, 0), (0, LANES - n_heads))).astype(BF16)
    return wm, wf


def kernel(x_prompt, x_sample, cache_fox_k, cache_fox_v, cache_fox_logf, cache_sb_k, cache_sb_v,
           w_in, b_f, g_attn, g_out_fox, g_out_sb, w_out, g_ffn, w_group, b_group, w_router, b_router,
           w1, w3, w2, g_final):
    depth = w_in.shape[0]
    bp, sp, d = x_prompt.shape
    bs, ss, _ = x_sample.shape
    past = cache_fox_logf.shape[2]
    n_heads = cache_fox_logf.shape[3]
    width = n_heads * HEAD_DIM
    n_groups = w_group.shape[2]
    n_experts = w_router.shape[2]
    per_group = n_experts // n_groups
    n_p, n_s = bp * sp, bs * ss
    sk_s = _round_up(past + ss, LANES)
    pad_s = sk_s - past - ss
    row2 = lambda a: a.reshape(1, -1)

    hp = x_prompt.reshape(n_p, d)
    hs = x_sample.reshape(n_s, d)
    outs_p, outs_s = [], []
    for l in range(depth):
        wm, wf = _split_w_in(w_in[l], width, n_heads)
        bfp = jnp.pad(b_f[l], (0, LANES - n_heads)).reshape(1, LANES)
        wo = w_out[l].astype(BF16)
        wr = jnp.pad(jnp.concatenate([w_group[l], w_router[l]], axis=1),
                     ((0, 0), (0, LANES - n_groups - n_experts))).astype(BF16)
        br = jnp.pad(jnp.concatenate([b_group[l], b_router[l]]), (0, LANES - n_groups - n_experts)).reshape(1, LANES)
        g_a, g_f = row2(g_attn[l]), row2(g_ffn[l])
        g_of, g_os = row2(g_out_fox[l]), row2(g_out_sb[l])

        lf, qf, kfb, kf, vfb, vf, qs, ksb, ks, vsb, vs = _in_proj(hp, g_a, wm, wf, bfp, n_heads)
        b3 = lambda a: a.reshape(bp, sp, -1)
        cf = _cumsum(b3(lf), min(sp, 512))
        tq = min(sp, ATTN_TILE)
        nf = _attention("fox", b3(qf), b3(kfb), b3(vfb), g_of, jnp.swapaxes(cf, 1, 2), tq=tq, tk=tq, q_off=0)
        ns = _attention("sb", b3(qs), b3(ksb), b3(vsb), g_os, None, tq=tq, tk=tq, q_off=0)
        outs_p.append((kf, vf, lf, ks, vs))

        lf2, qf2, kfb2, kf2, vfb2, vf2, qs2, ksb2, ks2, vsb2, vs2 = _in_proj(hs, g_a, wm, wf, bfp, n_heads)
        c3 = lambda a: a.reshape(bs, ss, -1)
        keys = lambda cache, new: jnp.concatenate(
            [cache.reshape(bs, past, width).astype(BF16), c3(new), jnp.zeros((bs, pad_s, width), BF16)], axis=1)
        cf_all = _cumsum(jnp.concatenate([cache_fox_logf[l].astype(F32), c3(lf2)], axis=1), past + ss)
        cf_all = jnp.swapaxes(jnp.pad(cf_all, ((0, 0), (0, pad_s), (0, 0))), 1, 2)
        nf2 = _attention("fox", c3(qf2), keys(cache_fox_k[l], kfb2), keys(cache_fox_v[l], vfb2), g_of, cf_all,
                         tq=ss, tk=sk_s, q_off=past)
        ns2 = _attention("sb", c3(qs2), keys(cache_sb_k[l], ksb2), keys(cache_sb_v[l], vsb2), g_os, None,
                         tq=ss, tk=sk_s, q_off=past)
        outs_s.append((kf2, vf2, lf2, ks2, vs2))

        h_all, hn_all, logits = _merge((nf.reshape(n_p, width), ns.reshape(n_p, width), hp),
                                       (nf2.reshape(n_s, width), ns2.reshape(n_s, width), hs), wo, g_f, wr, br)
        idx, gates, counts = _route(logits, n_groups, per_group)
        dest, plan = _moe_plan(idx, counts, n_experts)
        xs = _dispatch(dest, hn_all)
        rows = _experts(plan, xs, w1[l].astype(BF16), w3[l].astype(BF16), w2[l].astype(BF16))
        last = l == depth - 1
        g_last = row2(g_final)
        hp = _combine(dest, gates, h_all, rows, g_last, n_p, 0, last)
        hs = _combine(dest, gates, h_all, rows, g_last, n_s, n_p, last)

    heads = lambda a, b, s: a.reshape(b, s, n_heads, HEAD_DIM)
    stack = lambda outs, i, f: jnp.stack([f(o[i]) for o in outs])
    hp_f = lambda a: heads(a, bp, sp)
    hs_f = lambda a: heads(a, bs, ss)
    return (hp.reshape(bp, sp, d), hs.reshape(bs, ss, d),
            stack(outs_p, 0, hp_f), stack(outs_p, 1, hp_f), stack(outs_p, 2, lambda a: a.reshape(bp, sp, n_heads)),
            stack(outs_p, 3, hp_f), stack(outs_p, 4, hp_f),
            stack(outs_s, 0, hs_f), stack(outs_s, 1, hs_f), stack(outs_s, 2, lambda a: a.reshape(bs, ss, n_heads)),
            stack(outs_s, 3, hs_f), stack(outs_s, 4, hs_f))
```

```python
import functools
import math

import jax
import jax.numpy as jnp
from jax import lax
from jax.experimental import pallas as pl
from jax.experimental.pallas import tpu as pltpu

F32 = jnp.float32
BF16 = jnp.bfloat16

HEAD_DIM = 128
TOP_K = 2
EPS = 1e-6
LANES = 128
MXU_DIM = 256
NEG_BIG = -1e30
VMEM_LIMIT = 56 * 1024 * 1024
LOG2E = math.log2(math.e)
SCALE = HEAD_DIM ** -0.5
SCALE_LOG2 = SCALE * LOG2E

ROW_TILE = 256
ATTN_TILE = 512
FOX_GROUP = 4
SB_GROUP = 4
LOOKAHEAD = 2
EXPERT_BLOCK = 256
COMBINE_TILE = 128
ISSUE_UNROLL = 8


def _params(*sem):
    return pltpu.CompilerParams(dimension_semantics=sem, vmem_limit_bytes=VMEM_LIMIT)


def _log_sigmoid(x):
    return jnp.minimum(x, 0.0) - jnp.log1p(jnp.exp(-jnp.abs(x)))


def _rms(x, g):
    return x * lax.rsqrt(jnp.mean(x * x, axis=-1, keepdims=True) + EPS) * g


def _in_proj_kernel(x_ref, g_ref, wm_ref, wf_ref, bf_ref, lf_ref, *out_refs, width, n_heads):
    hn = _rms(x_ref[...], g_ref[...]).astype(BF16)
    refs = iter(out_refs)
    for c in range(6):
        res = jnp.dot(hn, wm_ref[:, c * width:(c + 1) * width], preferred_element_type=F32)
        next(refs)[...] = (res if c % 3 else res * SCALE_LOG2).astype(BF16)
        if c % 3:
            kv_ref = next(refs)
            kv_ref[...] = res.reshape(kv_ref.shape)
    fl = jnp.dot(hn, wf_ref[...], preferred_element_type=F32) + bf_ref[...]
    lf_ref[...] = _log_sigmoid(fl)[:, :n_heads]


def _in_proj(x, g, wm, wf, bf, n_heads):
    n, d = x.shape
    width = wm.shape[1] // 6
    tm = ROW_TILE
    row = lambda w: pl.BlockSpec((tm, w), lambda i: (i, 0))
    const = lambda a: pl.BlockSpec(a.shape, lambda i: (0, 0), pipeline_mode=pl.Buffered(1))
    per_head = pl.BlockSpec((tm, n_heads, HEAD_DIM), lambda i: (i, 0, 0))
    rows_out = jax.ShapeDtypeStruct((n, width), BF16)
    heads_out = jax.ShapeDtypeStruct((n, n_heads, HEAD_DIM), F32)
    mixer_specs = [row(width)] + [row(width), per_head] * 2
    mixer_shapes = [rows_out] + [rows_out, heads_out] * 2
    return pl.pallas_call(
        functools.partial(_in_proj_kernel, width=width, n_heads=n_heads),
        grid=(n // tm,),
        in_specs=[row(d), const(g), const(wm), const(wf), const(bf)],
        out_specs=[row(n_heads)] + mixer_specs * 2,
        out_shape=[jax.ShapeDtypeStruct((n, n_heads), F32)] + mixer_shapes * 2,
        compiler_params=_params("arbitrary"),
        name="in_proj",
    )(x, g, wm, wf, bf)


def _cumsum_kernel(lf_ref, cf_ref, carry_ref):
    tt = lf_ref.shape[0]

    @pl.when(pl.program_id(1) == 0)
    def _():
        carry_ref[...] = jnp.zeros_like(carry_ref)

    r = lax.broadcasted_iota(jnp.int32, (tt, tt), 0)
    c = lax.broadcasted_iota(jnp.int32, (tt, tt), 1)
    tri = jnp.where(c <= r, 1.0, 0.0).astype(F32)
    cf = jnp.dot(tri, lf_ref[...], precision=lax.Precision.HIGHEST,
                 preferred_element_type=F32) + carry_ref[...]
    cf_ref[...] = cf
    carry_ref[...] = cf[tt - 1:tt, :]


def _cumsum(lf, tt):
    b, t, h = lf.shape
    return pl.pallas_call(
        _cumsum_kernel,
        grid=(b, t // tt),
        in_specs=[pl.BlockSpec((None, tt, h), lambda i, j: (i, j, 0))],
        out_specs=pl.BlockSpec((None, tt, h), lambda i, j: (i, j, 0)),
        out_shape=jax.ShapeDtypeStruct((b, t, h), F32),
        scratch_shapes=[pltpu.VMEM((1, h), F32)],
        compiler_params=_params("arbitrary", "arbitrary"),
        name="cumsum",
    )(lf)


def _split3(x):
    hi = x.astype(BF16).astype(F32)
    r1 = x - hi
    mid = r1.astype(BF16).astype(F32)
    lo = (r1 - mid).astype(BF16).astype(F32)
    return hi, mid, lo


def _bias_block(c_rows, h, as_query):
    n = c_rows.shape[1]
    sub = lax.broadcasted_iota(jnp.int32, c_rows.shape, 0)
    c = jnp.sum(jnp.where(sub == h, c_rows, 0.0), axis=0, keepdims=True) * LOG2E
    hi, mid, lo = _split3(c)
    if as_query:
        parts, base, ones_lo = (hi, mid, lo), 0, 3
    else:
        parts, base, ones_lo = (-hi, -mid, -lo), 3, 0
    row = lax.broadcasted_iota(jnp.int32, (8, n), 0)
    blk = jnp.where((row >= ones_lo) & (row < ones_lo + 3), 1.0, 0.0)
    for i, p in enumerate(parts):
        blk = jnp.where(row == base + i, p, blk)
    full = jnp.concatenate([blk, jnp.zeros((LANES - 8, n), F32)], axis=0)
    return full.T.astype(BF16)


def _fox_kernel(q_ref, k_ref, v_ref, c_ref, gt_ref, o_ref, kaug, vt, *, tq, tk, q_off, prep, group, wide, one_tile):
    hg = pl.program_id(1)
    qi = pl.program_id(2)
    sk = k_ref.shape[0]
    hd = lambda g: slice(g * HEAD_DIM, (g + 1) * HEAD_DIM)

    @pl.when(qi == 0)
    def _():
        for g in range(group):
            for r0 in range(0, sk, prep):
                rows = pl.ds(r0, prep)
                kaug[g, rows, :HEAD_DIM] = k_ref[rows, hd(g)]
                kaug[g, rows, HEAD_DIM:] = _bias_block(c_ref[:, r0:r0 + prep], hg * group + g, False)
                vt[g, :, rows] = v_ref[rows, hd(g)].astype(F32).T.astype(BF16)

    q_lo = q_off if one_tile else pl.multiple_of(q_off + qi * tq, tq)
    c_q = c_ref[:, pl.ds(q_lo, tq)]
    qa = [jnp.concatenate([q_ref[:, hd(g)], _bias_block(c_q, hg * group + g, True)], axis=1) for g in range(group)]
    kiota = lax.broadcasted_iota(jnp.int32, (tk, tq), 0)
    qpos = q_lo + lax.broadcasted_iota(jnp.int32, (tk, tq), 1)

    def chunk(j, carry, width, masked=False):
        k0 = pl.multiple_of(j * width, width)

        def logits(g):
            s = lax.dot_general(kaug[g, pl.ds(k0, width), :], qa[g], (((1,), (1,)), ((), ())),
                                preferred_element_type=F32)
            if masked:
                s = jnp.where(k0 + kiota <= qpos, s, NEG_BIG)
            return s

        def absorb(g, s):
            m, l, acc = carry[g]
            m_new = jnp.maximum(m, jnp.max(s, axis=0, keepdims=True))
            p = jnp.exp2(s - m_new)
            alpha = jnp.exp2(m - m_new)
            l = alpha * l + jnp.sum(p, axis=0, keepdims=True)
            acc = alpha * acc + jnp.dot(vt[g, :, pl.ds(k0, width)], p.astype(BF16), preferred_element_type=F32)
            return m_new, l, acc

        out, pending = [], {}
        for g in range(group + LOOKAHEAD):
            if g < group:
                pending[g] = logits(g)
            if g >= LOOKAHEAD:
                out.append(absorb(g - LOOKAHEAD, pending.pop(g - LOOKAHEAD)))
        return tuple(out)

    n_full = (q_lo + 1) // tk
    n_all = (q_lo + tq + tk - 1) // tk
    carry = tuple((jnp.full((1, tq), NEG_BIG, F32), jnp.zeros((1, tq), F32), jnp.zeros((HEAD_DIM, tq), F32))
                  for _ in range(group))
    n_narrow = n_full
    if wide:
        carry = lax.fori_loop(0, n_full // 2, lambda j, c: chunk(j, c, 2 * tk), carry)
        n_narrow = n_full % 2
    carry = lax.fori_loop(n_full - n_narrow, n_full, lambda j, c: chunk(j, c, tk), carry)
    carry = lax.fori_loop(n_full, n_all, lambda j, c: chunk(j, c, tk, True), carry)
    for g in range(group):
        m, l, acc = carry[g]
        o = acc / l
        y = o * lax.rsqrt(jnp.mean(o * o, axis=0, keepdims=True) + EPS) * gt_ref[hd(g), :]
        o_ref[:, hd(g)] = y.T.astype(o_ref.dtype)


def _sb_kernel(q_ref, k_ref, v_ref, g_ref, o_ref, *, tq, tk, q_off, group, wide):
    qi = pl.program_id(2)
    hd = lambda g: slice(g * HEAD_DIM, (g + 1) * HEAD_DIM)
    q = [q_ref[:, hd(g)] for g in range(group)]
    q_lo = q_off + qi * tq
    ts = MXU_DIM if tk % MXU_DIM == 0 else LANES
    r = lax.broadcasted_iota(jnp.int32, (ts, ts), 0)
    c = lax.broadcasted_iota(jnp.int32, (ts, ts), 1)
    neg_from = jnp.where(r >= c, -1.0, 0.0).astype(BF16)
    sign_bit = jnp.uint32(0x80000000)

    def chunk(j, carry, width, masked=False):
        k0 = pl.multiple_of(j * width, width)
        n_sub = width // ts
        first_row = (lambda b: b * ts) if (masked and tq == tk) else (lambda b: 0)

        def logits(g, b):
            r0 = first_row(b)
            z = lax.dot_general(q[g][r0:], k_ref[pl.ds(k0 + b * ts, ts), hd(g)], (((1,), (1,)), ((), ())),
                                preferred_element_type=F32)
            neg_abs = pltpu.bitcast(pltpu.bitcast(z, jnp.uint32) | sign_bit, F32)
            sp = jnp.maximum(z, 0.0) + jnp.log2(1.0 + jnp.exp2(neg_abs))
            valid = None
            if masked:
                qpos = q_lo + r0 + lax.broadcasted_iota(jnp.int32, z.shape, 0)
                valid = k0 + b * ts + lax.broadcasted_iota(jnp.int32, z.shape, 1) < qpos
                sp = jnp.where(valid, sp, 0.0)
            hi = sp.astype(BF16)
            lo = (sp - hi.astype(F32)).astype(BF16)
            return z, hi, lo, valid

        def weights(state, after, r0):
            z, hi, lo, valid = state
            sfx = (jnp.dot(hi, neg_from, preferred_element_type=F32)
                   + jnp.dot(lo, neg_from, preferred_element_type=F32)) + after[r0:]
            a = jnp.exp2(z + sfx)
            if masked:
                a = jnp.where(valid, a, 0.0)
            total = sfx[:, 0:1]
            return (jnp.concatenate([after[:r0], total], axis=0) if r0 else total), a.astype(BF16)

        items = [(g, b) for b in reversed(range(n_sub)) for g in range(group)]
        after = [carry[g][0] for g in range(group)]
        acc = [carry[g][1] for g in range(group)]
        staged, ready = {}, {}
        n = len(items)
        for step in range(n + 2 * LOOKAHEAD):
            if step < n:
                staged[step] = logits(*items[step])
            if 0 <= step - LOOKAHEAD < n:
                g, b = items[step - LOOKAHEAD]
                after[g], ready[step - LOOKAHEAD] = weights(staged.pop(step - LOOKAHEAD), after[g], first_row(b))
            if 0 <= step - 2 * LOOKAHEAD < n:
                g, b = items[step - 2 * LOOKAHEAD]
                r0 = first_row(b)
                rows = pl.ds(k0 + b * ts, ts)
                pv = jnp.dot(ready.pop(step - 2 * LOOKAHEAD), v_ref[rows, hd(g)], preferred_element_type=F32)
                acc[g] = jnp.concatenate([acc[g][:r0], acc[g][r0:] + pv], axis=0) if r0 else acc[g] + pv
        return tuple((after[g], acc[g]) for g in range(group))

    n_full = q_lo // tk
    n_all = (q_lo + tq - 1 + tk - 1) // tk
    carry = tuple((jnp.zeros((tq, 1), F32), jnp.zeros((tq, HEAD_DIM), F32)) for _ in range(group))
    carry = lax.fori_loop(0, n_all - n_full, lambda i, c: chunk(n_all - 1 - i, c, tk, True), carry)
    n_narrow = n_full
    if wide:
        n_narrow = n_full % 2
    carry = lax.fori_loop(0, n_narrow, lambda i, c: chunk(n_full - 1 - i, c, tk), carry)
    if wide:
        carry = lax.fori_loop(0, n_full // 2, lambda i, c: chunk(n_full // 2 - 1 - i, c, 2 * tk), carry)
    for g in range(group):
        o_ref[:, hd(g)] = _rms(carry[g][1], g_ref[:, hd(g)]).astype(o_ref.dtype)


def _attention(kind, q, k, v, g, cf, *, tq, tk, q_off):
    b, sq, width = q.shape
    sk = k.shape[1]
    n_heads = width // HEAD_DIM
    group = FOX_GROUP if kind == "fox" else SB_GROUP
    assert sq % tq == 0 and sk % tk == 0 and q_off % tq == 0 and n_heads % group == 0
    w = group * HEAD_DIM
    wide = sk % (2 * tk) == 0
    q_spec = pl.BlockSpec((None, tq, w), lambda i, h, j: (i, j, h))
    kv_spec = pl.BlockSpec((None, sk, w), lambda i, h, j: (i, 0, h))
    g_spec = pl.BlockSpec((1, w), lambda i, h, j: (0, h))
    if kind == "fox":
        body = functools.partial(_fox_kernel, tq=tq, tk=tk, q_off=q_off, prep=math.gcd(sk, 512), group=group,
                                 wide=wide, one_tile=sq == tq)
        in_specs = [q_spec, kv_spec, kv_spec, pl.BlockSpec((None, n_heads, sk), lambda i, h, j: (i, 0, 0)),
                    pl.BlockSpec((w, 1), lambda i, h, j: (h, 0))]
        args = (q, k, v, cf, g.reshape(width, 1))
        scratch = [pltpu.VMEM((group, sk, 2 * HEAD_DIM), BF16), pltpu.VMEM((group, HEAD_DIM, sk), BF16)]
    else:
        body = functools.partial(_sb_kernel, tq=tq, tk=tk, q_off=q_off, group=group, wide=wide)
        in_specs = [q_spec, kv_spec, kv_spec, g_spec]
        args = (q, k, v, g)
        scratch = []
    return pl.pallas_call(
        body,
        grid=(b, n_heads // group, sq // tq),
        in_specs=in_specs,
        out_specs=q_spec,
        out_shape=jax.ShapeDtypeStruct((b, sq, width), BF16),
        scratch_shapes=scratch,
        compiler_params=_params("arbitrary", "arbitrary", "arbitrary"),
        name=kind,
    )(*args)


def _merge_kernel(nfp_ref, nsp_ref, xp_ref, nfs_ref, nss_ref, xs_ref, wo_ref, g_ref, wr_ref, br_ref,
                  h_ref, hn_ref, lg_ref, *, prompt_blocks):
    def body(nf_ref, ns_ref, x_ref):
        width = nf_ref.shape[1]
        h = x_ref[...] + (jnp.dot(nf_ref[...], wo_ref[:width, :], preferred_element_type=F32)
                          + jnp.dot(ns_ref[...], wo_ref[width:, :], preferred_element_type=F32))
        h_ref[...] = h
        hn = _rms(h, g_ref[...])
        hn_ref[...] = hn
        lg_ref[...] = jnp.dot(hn.astype(BF16), wr_ref[...], preferred_element_type=F32) + br_ref[...]

    is_prompt = pl.program_id(0) < prompt_blocks
    pl.when(is_prompt)(lambda: body(nfp_ref, nsp_ref, xp_ref))
    pl.when(jnp.logical_not(is_prompt))(lambda: body(nfs_ref, nss_ref, xs_ref))


def _merge(prompt, sample, wo, g, wr, br):
    n_p, d = prompt[2].shape
    n_s = sample[2].shape[0]
    width = prompt[0].shape[1]
    tm = ROW_TILE
    pb, sb = n_p // tm, n_s // tm
    n_total = n_p + n_s
    p_row = lambda w: pl.BlockSpec((tm, w), lambda i: (jnp.minimum(i, pb - 1), 0))
    s_row = lambda w: pl.BlockSpec((tm, w), lambda i: (jnp.maximum(i - pb, 0), 0))
    out_row = lambda w: pl.BlockSpec((tm, w), lambda i: (i, 0))
    const = lambda a: pl.BlockSpec(a.shape, lambda i: (0, 0), pipeline_mode=pl.Buffered(1))
    return pl.pallas_call(
        functools.partial(_merge_kernel, prompt_blocks=pb),
        grid=(pb + sb,),
        in_specs=[p_row(width), p_row(width), p_row(d), s_row(width), s_row(width), s_row(d),
                  const(wo), const(g), const(wr), const(br)],
        out_specs=[out_row(d), out_row(d), out_row(LANES)],
        out_shape=[jax.ShapeDtypeStruct((n_total, d), F32), jax.ShapeDtypeStruct((n_total, d), F32),
                   jax.ShapeDtypeStruct((n_total, LANES), F32)],
        compiler_params=_params("arbitrary"),
        name="merge",
    )(*prompt, *sample, wo, g, wr, br)


def _route_kernel(lg_ref, idx_ref, gate_ref, cnt_ref, carry_ref, *, n_groups, per_group):
    tr = lg_ref.shape[0]

    @pl.when(pl.program_id(0) == 0)
    def _():
        carry_ref[...] = jnp.zeros_like(carry_ref)

    x = lg_ref[...]
    lane = lax.broadcasted_iota(jnp.int32, x.shape, 1).astype(F32)
    far = float(LANES)
    rmax = lambda a: jnp.max(a, axis=1, keepdims=True)
    first = lambda hit: jnp.min(jnp.where(hit, lane, far), axis=1, keepdims=True)

    gmask = lane < n_groups
    gl = jnp.where(gmask, x, -jnp.inf)
    gmax = rmax(gl)
    gsel = first(gl == gmax)
    p_g = 1.0 / jnp.sum(jnp.where(gmask, jnp.exp(x - gmax), 0.0), axis=1, keepdims=True)
    lo = n_groups + gsel * per_group
    emask = (lane >= lo) & (lane < lo + per_group)
    el = jnp.where(emask, x, -jnp.inf)
    m1 = rmax(el)
    i1 = first(el == m1)
    el2 = jnp.where(lane == i1, -jnp.inf, el)
    m2 = rmax(el2)
    i2 = first(el2 == m2)
    zsum = jnp.sum(jnp.where(emask, jnp.exp(x - m1), 0.0), axis=1, keepdims=True)
    p1 = 1.0 / zsum
    p2 = jnp.exp(m2 - m1) / zsum
    psum = p1 + p2
    gate1 = p_g * p1 / psum
    gate2 = p_g * p2 / psum
    e1 = i1 - n_groups
    e2 = i2 - n_groups
    hit1 = lane == e1
    hit2 = lane == e2
    onehot = jnp.where(hit1 | hit2, 1.0, 0.0)
    r = lax.broadcasted_iota(jnp.int32, (tr, tr), 0)
    c = lax.broadcasted_iota(jnp.int32, (tr, tr), 1)
    before = jnp.where(c < r, 1.0, 0.0).astype(BF16)
    seen = jnp.dot(before, onehot.astype(BF16), preferred_element_type=F32) + carry_ref[...]
    r1 = jnp.sum(jnp.where(hit1, seen, 0.0), axis=1, keepdims=True)
    r2 = jnp.sum(jnp.where(hit2, seen, 0.0), axis=1, keepdims=True)
    carry_ref[...] = carry_ref[...] + jnp.sum(onehot, axis=0, keepdims=True)
    cnt_ref[...] = carry_ref[...]

    col = lax.broadcasted_iota(jnp.int32, idx_ref.shape, 1)
    idx = jnp.where(col == 0, e1, jnp.where(col == 1, e2, jnp.where(col == 2, r1, r2)))
    idx_ref[...] = idx.astype(jnp.int32)
    gate_ref[...] = jnp.where(col == 0, gate1, jnp.where(col == 1, gate2, 0.0))


def _route(logits, n_groups, per_group):
    n = logits.shape[0]
    tr = ROW_TILE
    narrow = pl.BlockSpec((tr, 8), lambda i: (i, 0))
    return pl.pallas_call(
        functools.partial(_route_kernel, n_groups=n_groups, per_group=per_group),
        grid=(n // tr,),
        in_specs=[pl.BlockSpec((tr, LANES), lambda i: (i, 0))],
        out_specs=[narrow, narrow, pl.BlockSpec((1, LANES), lambda i: (0, 0))],
        out_shape=[jax.ShapeDtypeStruct((n, 8), jnp.int32), jax.ShapeDtypeStruct((n, 8), F32),
                   jax.ShapeDtypeStruct((1, LANES), F32)],
        scratch_shapes=[pltpu.VMEM((1, LANES), F32)],
        compiler_params=_params("arbitrary"),
        name="route",
    )(logits)


def _row_copy(src_ref, src_row, dst_ref, dst_row, sem):
    return pltpu.make_async_copy(src_ref.at[pl.ds(src_row, 1)], dst_ref.at[pl.ds(dst_row, 1)], sem)


def _wait_rows(src_ref, dst_ref, sem, n):
    pltpu.make_async_copy(src_ref.at[pl.ds(0, n)], dst_ref.at[pl.ds(0, n)], sem).wait()


def _dispatch_kernel(dest_ref, hn_ref, xs_ref, sem):
    td = hn_ref.shape[0]

    def start(r, _):
        for k in range(TOP_K):
            _row_copy(hn_ref, r, xs_ref, dest_ref[0, 0, TOP_K * r + k], sem).start()
        return 0

    for r in range(td):
        start(r, 0)
    for _ in range(TOP_K):
        _wait_rows(hn_ref, xs_ref, sem, td)


def _dispatch(dest, hn):
    n, d = hn.shape
    td = ROW_TILE
    return pl.pallas_call(
        _dispatch_kernel,
        grid=(n // td,),
        in_specs=[pl.BlockSpec((1, 1, TOP_K * td), lambda i: (i, 0, 0), memory_space=pltpu.SMEM),
                  pl.BlockSpec((td, d), lambda i: (i, 0))],
        out_specs=pl.BlockSpec(memory_space=pl.ANY),
        out_shape=jax.ShapeDtypeStruct((n * TOP_K, d), F32),
        scratch_shapes=[pltpu.SemaphoreType.DMA(())],
        compiler_params=_params("arbitrary"),
        name="dispatch",
    )(dest.reshape(n // td, 1, TOP_K * td), hn)


def _expert_kernel(blk_ref, exp_ref, lo_ref, hi_ref, first_ref, xs_ref, w1_ref, w3_ref, w2_ref, o_ref):
    w = pl.program_id(0)
    lo, hi = lo_ref[w], hi_ref[w]

    def swiglu():
        rows = lax.broadcasted_iota(jnp.int32, xs_ref.shape, 0)
        x = jnp.where((rows >= lo) & (rows < hi), xs_ref[...], 0.0).astype(BF16)
        a = jnp.dot(x, w1_ref[...], preferred_element_type=F32)
        b = jnp.dot(x, w3_ref[...], preferred_element_type=F32)
        act = (a * jax.nn.sigmoid(a) * b).astype(BF16)
        return jnp.dot(act, w2_ref[...], preferred_element_type=F32)

    @pl.when((hi > lo) & (first_ref[w] == 1))
    def _():
        o_ref[...] = swiglu()

    @pl.when((hi > lo) & (first_ref[w] == 0))
    def _():
        o_ref[...] += swiglu()


def _experts(plan, xs, w1, w3, w2):
    m, d = xs.shape
    de = w1.shape[2]
    blk = EXPERT_BLOCK
    n_items = plan[0].shape[0]
    row_blk = lambda i, rb, ex, lo, hi, fs: (rb[i], 0)
    weight = lambda i, rb, ex, lo, hi, fs: (ex[i], 0, 0)
    return pl.pallas_call(
        _expert_kernel,
        grid_spec=pltpu.PrefetchScalarGridSpec(
            num_scalar_prefetch=5,
            grid=(n_items,),
            in_specs=[pl.BlockSpec((blk, d), row_blk),
                      pl.BlockSpec((None, d, de), weight),
                      pl.BlockSpec((None, d, de), weight),
                      pl.BlockSpec((None, de, d), weight)],
            out_specs=pl.BlockSpec((blk, d), row_blk)),
        out_shape=jax.ShapeDtypeStruct((m, d), F32),
        compiler_params=_params("arbitrary"),
        name="experts",
    )(*plan, xs, w1, w3, w2)


def _combine_kernel(dest_ref, next_ref, gate_ref, h_ref, rows_ref, g_ref, y_ref, buf, sem, *, final_norm):
    i = pl.program_id(0)
    last = pl.num_programs(0) - 1
    tc = h_ref.shape[0]
    slot = i % 2

    def gather(idx_ref, into, unrolled):
        def start(r, _):
            for k in range(TOP_K):
                _row_copy(rows_ref, idx_ref[0, 0, TOP_K * r + k], buf.at[into, k], r, sem.at[into]).start()
            return 0

        if unrolled:
            for r in range(tc):
                start(r, 0)
        else:
            lax.fori_loop(0, tc, start, 0, unroll=ISSUE_UNROLL)

    def drain(into):
        for k in range(TOP_K):
            _wait_rows(rows_ref, buf.at[into, k], sem.at[into], tc)

    pl.when(i == 0)(lambda: gather(dest_ref, 0, False))
    gather(next_ref, 1 - slot, True)
    drain(slot)
    gates = gate_ref[...]
    y = gates[:, 0:1] * buf[slot, 0] + gates[:, 1:2] * buf[slot, 1]
    h = h_ref[...] + y
    y_ref[...] = _rms(h, g_ref[...]) if final_norm else h
    pl.when(i == last)(lambda: drain(1 - slot))


def _combine(dest, gates, h, rows, g, n, row_off, final_norm):
    d = h.shape[1]
    tc = COMBINE_TILE
    off = row_off // tc
    last = off + n // tc - 1
    idx_spec = lambda f: pl.BlockSpec((1, 1, TOP_K * tc), lambda i: (f(i), 0, 0), memory_space=pltpu.SMEM)
    dest3 = dest.reshape(-1, 1, TOP_K * tc)
    return pl.pallas_call(
        functools.partial(_combine_kernel, final_norm=final_norm),
        grid=(n // tc,),
        in_specs=[idx_spec(lambda i: i + off), idx_spec(lambda i: jnp.minimum(i + off + 1, last)),
                  pl.BlockSpec((tc, 8), lambda i: (i + off, 0)),
                  pl.BlockSpec((tc, d), lambda i: (i + off, 0)),
                  pl.BlockSpec(memory_space=pl.ANY),
                  pl.BlockSpec((1, d), lambda i: (0, 0))],
        out_specs=pl.BlockSpec((tc, d), lambda i: (i, 0)),
        out_shape=jax.ShapeDtypeStruct((n, d), F32),
        scratch_shapes=[pltpu.VMEM((2, TOP_K, tc, d), F32), pltpu.SemaphoreType.DMA((2,))],
        compiler_params=_params("arbitrary"),
        name="combine",
    )(dest3, dest3, gates, h, rows, g)


def _round_up(a, m):
    return -(-a // m) * m


def _moe_plan(idx, counts, n_experts):
    blk = EXPERT_BLOCK
    n = idx.shape[0]
    i32 = lambda a: a.astype(jnp.int32)
    pick = lambda table, i: jnp.sum(jnp.where(i[..., None] == jnp.arange(table.shape[0]), table, 0), axis=-1)
    rank_of = lambda ends, x: jnp.sum(i32(ends[None, :] <= x[:, None]), axis=1)
    counts = i32(counts[0, :n_experts])
    end = jnp.cumsum(counts)
    start = end - counts
    dest = i32(pick(start, idx[:, :TOP_K]) + idx[:, TOP_K:2 * TOP_K])
    n_blocks = n * TOP_K // blk
    block_lo = jnp.arange(n_blocks, dtype=jnp.int32) * blk
    e_first = jnp.minimum(rank_of(end, block_lo), n_experts - 1)
    e_last = jnp.minimum(rank_of(end, block_lo + blk - 1), n_experts - 1)
    item_end = jnp.cumsum(e_last - e_first + 1)
    item_start = item_end - (e_last - e_first + 1)
    n_items = n_blocks + n_experts - 1
    w = jnp.arange(n_items, dtype=jnp.int32)
    live = w < item_end[-1]
    rb = jnp.minimum(rank_of(item_end, w), n_blocks - 1)
    ex = jnp.where(live, pick(e_first, rb) + (w - pick(item_start, rb)), e_last[-1])
    lo = jnp.where(live, jnp.clip(pick(start, ex) - rb * blk, 0, blk), 0)
    hi = jnp.where(live, jnp.clip(pick(end, ex) - rb * blk, 0, blk), 0)
    first = i32(w == pick(item_start, rb))
    return dest, (i32(rb), i32(ex), i32(lo), i32(hi), first)


def _split_w_in(w_in, width, n_heads):
    wm = jnp.concatenate([w_in[:, :3 * width], w_in[:, 3 * width + n_heads:]], axis=1).astype(BF16)
    wf = jnp.pad(w_in[:, 3 * width:3 * width + n_heads], ((0, 0), (0, LANES - n_heads))).astype(BF16)
    return wm, wf


def kernel(x_prompt, x_sample, cache_fox_k, cache_fox_v, cache_fox_logf, cache_sb_k, cache_sb_v,
           w_in, b_f, g_attn, g_out_fox, g_out_sb, w_out, g_ffn, w_group, b_group, w_router, b_router,
           w1, w3, w2, g_final):
    depth = w_in.shape[0]
    bp, sp, d = x_prompt.shape
    bs, ss, _ = x_sample.shape
    past = cache_fox_logf.shape[2]
    n_heads = cache_fox_logf.shape[3]
    width = n_heads * HEAD_DIM
    n_groups = w_group.shape[2]
    n_experts = w_router.shape[2]
    per_group = n_experts // n_groups
    n_p, n_s = bp * sp, bs * ss
    sk_s = _round_up(past + ss, LANES)
    pad_s = sk_s - past - ss
    row2 = lambda a: a.reshape(1, -1)

    hp = x_prompt.reshape(n_p, d)
    hs = x_sample.reshape(n_s, d)
    outs_p, outs_s = [], []
    for l in range(depth):
        wm, wf = _split_w_in(w_in[l], width, n_heads)
        bfp = jnp.pad(b_f[l], (0, LANES - n_heads)).reshape(1, LANES)
        wo = w_out[l].astype(BF16)
        wr = jnp.pad(jnp.concatenate([w_group[l], w_router[l]], axis=1),
                     ((0, 0), (0, LANES - n_groups - n_experts))).astype(BF16)
        br = jnp.pad(jnp.concatenate([b_group[l], b_router[l]]), (0, LANES - n_groups - n_experts)).reshape(1, LANES)
        g_a, g_f = row2(g_attn[l]), row2(g_ffn[l])
        g_of, g_os = row2(g_out_fox[l]), row2(g_out_sb[l])

        lf, qf, kfb, kf, vfb, vf, qs, ksb, ks, vsb, vs = _in_proj(hp, g_a, wm, wf, bfp, n_heads)
        b3 = lambda a: a.reshape(bp, sp, -1)
        cf = _cumsum(b3(lf), min(sp, 512))
        tq = min(sp, ATTN_TILE)
        nf = _attention("fox", b3(qf), b3(kfb), b3(vfb), g_of, jnp.swapaxes(cf, 1, 2), tq=tq, tk=tq, q_off=0)
        ns = _attention("sb", b3(qs), b3(ksb), b3(vsb), g_os, None, tq=tq, tk=tq, q_off=0)
        outs_p.append((kf, vf, lf, ks, vs))

        lf2, qf2, kfb2, kf2, vfb2, vf2, qs2, ksb2, ks2, vsb2, vs2 = _in_proj(hs, g_a, wm, wf, bfp, n_heads)
        c3 = lambda a: a.reshape(bs, ss, -1)
        keys = lambda cache, new: jnp.concatenate(
            [cache.reshape(bs, past, width).astype(BF16), c3(new), jnp.zeros((bs, pad_s, width), BF16)], axis=1)
        cf_all = _cumsum(jnp.concatenate([cache_fox_logf[l].astype(F32), c3(lf2)], axis=1), past + ss)
        cf_all = jnp.swapaxes(jnp.pad(cf_all, ((0, 0), (0, pad_s), (0, 0))), 1, 2)
        nf2 = _attention("fox", c3(qf2), keys(cache_fox_k[l], kfb2), keys(cache_fox_v[l], vfb2), g_of, cf_all,
                         tq=ss, tk=sk_s, q_off=past)
        ns2 = _attention("sb", c3(qs2), keys(cache_sb_k[l], ksb2), keys(cache_sb_v[l], vsb2), g_os, None,
                         tq=ss, tk=sk_s, q_off=past)
        outs_s.append((kf2, vf2, lf2, ks2, vs2))

        h_all, hn_all, logits = _merge((nf.reshape(n_p, width), ns.reshape(n_p, width), hp),
                                       (nf2.reshape(n_s, width), ns2.reshape(n_s, width), hs), wo, g_f, wr, br)
        idx, gates, counts = _route(logits, n_groups, per_group)
        dest, plan = _moe_plan(idx, counts, n_experts)
        xs = _dispatch(dest, hn_all)
        rows = _experts(plan, xs, w1[l].astype(BF16), w3[l].astype(BF16), w2[l].astype(BF16))
        last = l == depth - 1
        g_last = row2(g_final)
        hp = _combine(dest, gates, h_all, rows, g_last, n_p, 0, last)
        hs = _combine(dest, gates, h_all, rows, g_last, n_s, n_p, last)

    heads = lambda a, b, s: a.reshape(b, s, n_heads, HEAD_DIM)
    stack = lambda outs, i, f: jnp.stack([f(o[i]) for o in outs])
    hp_f = lambda a: heads(a, bp, sp)
    hs_f = lambda a: heads(a, bs, ss)
    return (hp.reshape(bp, sp, d), hs.reshape(bs, ss, d),
            stack(outs_p, 0, hp_f), stack(outs_p, 1, hp_f), stack(outs_p, 2, lambda a: a.reshape(bp, sp, n_heads)),
            stack(outs_p, 3, hp_f), stack(outs_p, 4, hp_f),
            stack(outs_s, 0, hs_f), stack(outs_s, 1, hs_f), stack(outs_s, 2, lambda a: a.reshape(bs, ss, n_heads)),
            stack(outs_s, 3, hs_f), stack(outs_s, 4, hs_f))
```

```python
import functools
import math

import jax
import jax.numpy as jnp
from jax import lax
from jax.experimental import pallas as pl
from jax.experimental.pallas import tpu as pltpu

F32 = jnp.float32
BF16 = jnp.bfloat16

HEAD_DIM = 128
TOP_K = 2
EPS = 1e-6
LANES = 128
MXU_DIM = 256
NEG_BIG = -1e30
VMEM_LIMIT = 56 * 1024 * 1024
LOG2E = math.log2(math.e)
SCALE = HEAD_DIM ** -0.5
SCALE_LOG2 = SCALE * LOG2E

ROW_TILE = 256
ATTN_TILE = 512
FOX_GROUP = 4
SB_GROUP = 4
LOOKAHEAD = 2
EXPERT_BLOCK = 256
COMBINE_TILE = 128
ISSUE_UNROLL = 8


def _params(*sem):
    return pltpu.CompilerParams(dimension_semantics=sem, vmem_limit_bytes=VMEM_LIMIT)


def _log_sigmoid(x):
    return jnp.minimum(x, 0.0) - jnp.log1p(jnp.exp(-jnp.abs(x)))


def _rms(x, g):
    return x * lax.rsqrt(jnp.mean(x * x, axis=-1, keepdims=True) + EPS) * g


def _in_proj_kernel(x_ref, g_ref, wm_ref, wf_ref, bf_ref, lf_ref, *out_refs, width, n_heads):
    hn = _rms(x_ref[...], g_ref[...]).astype(BF16)
    refs = iter(out_refs)
    for c in range(6):
        res = jnp.dot(hn, wm_ref[:, c * width:(c + 1) * width], preferred_element_type=F32)
        next(refs)[...] = (res if c % 3 else res * SCALE_LOG2).astype(BF16)
        if c % 3:
            kv_ref = next(refs)
            kv_ref[...] = res.reshape(kv_ref.shape)
    fl = jnp.dot(hn, wf_ref[...], preferred_element_type=F32) + bf_ref[...]
    lf_ref[...] = _log_sigmoid(fl)[:, :n_heads]


def _in_proj(x, g, wm, wf, bf, n_heads):
    n, d = x.shape
    width = wm.shape[1] // 6
    tm = ROW_TILE
    row = lambda w: pl.BlockSpec((tm, w), lambda i: (i, 0))
    const = lambda a: pl.BlockSpec(a.shape, lambda i: (0, 0), pipeline_mode=pl.Buffered(1))
    per_head = pl.BlockSpec((tm, n_heads, HEAD_DIM), lambda i: (i, 0, 0))
    rows_out = jax.ShapeDtypeStruct((n, width), BF16)
    heads_out = jax.ShapeDtypeStruct((n, n_heads, HEAD_DIM), F32)
    mixer_specs = [row(width)] + [row(width), per_head] * 2
    mixer_shapes = [rows_out] + [rows_out, heads_out] * 2
    return pl.pallas_call(
        functools.partial(_in_proj_kernel, width=width, n_heads=n_heads),
        grid=(n // tm,),
        in_specs=[row(d), const(g), const(wm), const(wf), const(bf)],
        out_specs=[row(n_heads)] + mixer_specs * 2,
        out_shape=[jax.ShapeDtypeStruct((n, n_heads), F32)] + mixer_shapes * 2,
        compiler_params=_params("arbitrary"),
        name="in_proj",
    )(x, g, wm, wf, bf)


def _cumsum_kernel(lf_ref, cf_ref, carry_ref):
    tt = lf_ref.shape[0]

    @pl.when(pl.program_id(1) == 0)
    def _():
        carry_ref[...] = jnp.zeros_like(carry_ref)

    r = lax.broadcasted_iota(jnp.int32, (tt, tt), 0)
    c = lax.broadcasted_iota(jnp.int32, (tt, tt), 1)
    tri = jnp.where(c <= r, 1.0, 0.0).astype(F32)
    cf = jnp.dot(tri, lf_ref[...], precision=lax.Precision.HIGHEST,
                 preferred_element_type=F32) + carry_ref[...]
    cf_ref[...] = cf
    carry_ref[...] = cf[tt - 1:tt, :]


def _cumsum(lf, tt):
    b, t, h = lf.shape
    return pl.pallas_call(
        _cumsum_kernel,
        grid=(b, t // tt),
        in_specs=[pl.BlockSpec((None, tt, h), lambda i, j: (i, j, 0))],
        out_specs=pl.BlockSpec((None, tt, h), lambda i, j: (i, j, 0)),
        out_shape=jax.ShapeDtypeStruct((b, t, h), F32),
        scratch_shapes=[pltpu.VMEM((1, h), F32)],
        compiler_params=_params("arbitrary", "arbitrary"),
        name="cumsum",
    )(lf)


def _split3(x):
    hi = x.astype(BF16).astype(F32)
    r1 = x - hi
    mid = r1.astype(BF16).astype(F32)
    lo = (r1 - mid).astype(BF16).astype(F32)
    return hi, mid, lo


def _bias_block(c_rows, h, as_query):
    n = c_rows.shape[1]
    sub = lax.broadcasted_iota(jnp.int32, c_rows.shape, 0)
    c = jnp.sum(jnp.where(sub == h, c_rows, 0.0), axis=0, keepdims=True) * LOG2E
    hi, mid, lo = _split3(c)
    if as_query:
        parts, base, ones_lo = (hi, mid, lo), 0, 3
    else:
        parts, base, ones_lo = (-hi, -mid, -lo), 3, 0
    row = lax.broadcasted_iota(jnp.int32, (8, n), 0)
    blk = jnp.where((row >= ones_lo) & (row < ones_lo + 3), 1.0, 0.0)
    for i, p in enumerate(parts):
        blk = jnp.where(row == base + i, p, blk)
    full = jnp.concatenate([blk, jnp.zeros((LANES - 8, n), F32)], axis=0)
    return full.T.astype(BF16)


def _fox_kernel(q_ref, k_ref, v_ref, c_ref, gt_ref, o_ref, kaug, vt, *, tq, tk, q_off, prep, group, wide, one_tile):
    hg = pl.program_id(1)
    qi = pl.program_id(2)
    sk = k_ref.shape[0]
    hd = lambda g: slice(g * HEAD_DIM, (g + 1) * HEAD_DIM)

    @pl.when(qi == 0)
    def _():
        for g in range(group):
            for r0 in range(0, sk, prep):
                rows = pl.ds(r0, prep)
                kaug[g, rows, :HEAD_DIM] = k_ref[rows, hd(g)]
                kaug[g, rows, HEAD_DIM:] = _bias_block(c_ref[:, r0:r0 + prep], hg * group + g, False)
                vt[g, :, rows] = v_ref[rows, hd(g)].astype(F32).T.astype(BF16)

    q_lo = q_off if one_tile else pl.multiple_of(q_off + qi * tq, tq)
    c_q = c_ref[:, pl.ds(q_lo, tq)]
    qa = [jnp.concatenate([q_ref[:, hd(g)], _bias_block(c_q, hg * group + g, True)], axis=1) for g in range(group)]
    kiota = lax.broadcasted_iota(jnp.int32, (tk, tq), 0)
    qpos = q_lo + lax.broadcasted_iota(jnp.int32, (tk, tq), 1)

    def chunk(j, carry, width, masked=False):
        k0 = pl.multiple_of(j * width, width)

        def logits(g):
            s = lax.dot_general(kaug[g, pl.ds(k0, width), :], qa[g], (((1,), (1,)), ((), ())),
                                preferred_element_type=F32)
            if masked:
                s = jnp.where(k0 + kiota <= qpos, s, NEG_BIG)
            return s

        def absorb(g, s):
            m, l, acc = carry[g]
            m_new = jnp.maximum(m, jnp.max(s, axis=0, keepdims=True))
            p = jnp.exp2(s - m_new)
            alpha = jnp.exp2(m - m_new)
            l = alpha * l + jnp.sum(p, axis=0, keepdims=True)
            acc = alpha * acc + jnp.dot(vt[g, :, pl.ds(k0, width)], p.astype(BF16), preferred_element_type=F32)
            return m_new, l, acc

        out, pending = [], {}
        for g in range(group + LOOKAHEAD):
            if g < group:
                pending[g] = logits(g)
            if g >= LOOKAHEAD:
                out.append(absorb(g - LOOKAHEAD, pending.pop(g - LOOKAHEAD)))
        return tuple(out)

    n_full = (q_lo + 1) // tk
    n_all = (q_lo + tq + tk - 1) // tk
    carry = tuple((jnp.full((1, tq), NEG_BIG, F32), jnp.zeros((1, tq), F32), jnp.zeros((HEAD_DIM, tq), F32))
                  for _ in range(group))
    n_narrow = n_full
    if wide:
        carry = lax.fori_loop(0, n_full // 2, lambda j, c: chunk(j, c, 2 * tk), carry)
        n_narrow = n_full % 2
    carry = lax.fori_loop(n_full - n_narrow, n_full, lambda j, c: chunk(j, c, tk), carry)
    carry = lax.fori_loop(n_full, n_all, lambda j, c: chunk(j, c, tk, True), carry)
    for g in range(group):
        m, l, acc = carry[g]
        o = acc / l
        y = o * lax.rsqrt(jnp.mean(o * o, axis=0, keepdims=True) + EPS) * gt_ref[hd(g), :]
        o_ref[:, hd(g)] = y.T.astype(o_ref.dtype)


def _sb_kernel(q_ref, k_ref, v_ref, g_ref, o_ref, *, tq, tk, q_off, group, wide):
    qi = pl.program_id(2)
    hd = lambda g: slice(g * HEAD_DIM, (g + 1) * HEAD_DIM)
    q = [q_ref[:, hd(g)] for g in range(group)]
    q_lo = q_off + qi * tq
    ts = MXU_DIM if tk % MXU_DIM == 0 else LANES
    r = lax.broadcasted_iota(jnp.int32, (ts, ts), 0)
    c = lax.broadcasted_iota(jnp.int32, (ts, ts), 1)
    neg_from = jnp.where(r >= c, -1.0, 0.0).astype(BF16)
    sign_bit = jnp.uint32(0x80000000)

    def chunk(j, carry, width, masked=False):
        k0 = pl.multiple_of(j * width, width)
        n_sub = width // ts
        first_row = (lambda b: b * ts) if (masked and tq == tk) else (lambda b: 0)

        def logits(g, b):
            r0 = first_row(b)
            z = lax.dot_general(q[g][r0:], k_ref[pl.ds(k0 + b * ts, ts), hd(g)], (((1,), (1,)), ((), ())),
                                preferred_element_type=F32)
            neg_abs = pltpu.bitcast(pltpu.bitcast(z, jnp.uint32) | sign_bit, F32)
            sp = jnp.maximum(z, 0.0) + jnp.log2(1.0 + jnp.exp2(neg_abs))
            valid = None
            if masked:
                qpos = q_lo + r0 + lax.broadcasted_iota(jnp.int32, z.shape, 0)
                valid = k0 + b * ts + lax.broadcasted_iota(jnp.int32, z.shape, 1) < qpos
                sp = jnp.where(valid, sp, 0.0)
            hi = sp.astype(BF16)
            lo = (sp - hi.astype(F32)).astype(BF16)
            return z, hi, lo, valid

        def weights(state, after, r0):
            z, hi, lo, valid = state
            sfx = (jnp.dot(hi, neg_from, preferred_element_type=F32)
                   + jnp.dot(lo, neg_from, preferred_element_type=F32)) + after[r0:]
            a = jnp.exp2(z + sfx)
            if masked:
                a = jnp.where(valid, a, 0.0)
            total = sfx[:, 0:1]
            return (jnp.concatenate([after[:r0], total], axis=0) if r0 else total), a.astype(BF16)

        items = [(g, b) for b in reversed(range(n_sub)) for g in range(group)]
        after = [carry[g][0] for g in range(group)]
        acc = [carry[g][1] for g in range(group)]
        staged, ready = {}, {}
        n = len(items)
        for step in range(n + 2 * LOOKAHEAD):
            if step < n:
                staged[step] = logits(*items[step])
            if 0 <= step - LOOKAHEAD < n:
                g, b = items[step - LOOKAHEAD]
                after[g], ready[step - LOOKAHEAD] = weights(staged.pop(step - LOOKAHEAD), after[g], first_row(b))
            if 0 <= step - 2 * LOOKAHEAD < n:
                g, b = items[step - 2 * LOOKAHEAD]
                r0 = first_row(b)
                rows = pl.ds(k0 + b * ts, ts)
                pv = jnp.dot(ready.pop(step - 2 * LOOKAHEAD), v_ref[rows, hd(g)], preferred_element_type=F32)
                acc[g] = jnp.concatenate([acc[g][:r0], acc[g][r0:] + pv], axis=0) if r0 else acc[g] + pv
        return tuple((after[g], acc[g]) for g in range(group))

    n_full = q_lo // tk
    n_all = (q_lo + tq - 1 + tk - 1) // tk
    carry = tuple((jnp.zeros((tq, 1), F32), jnp.zeros((tq, HEAD_DIM), F32)) for _ in range(group))
    carry = lax.fori_loop(0, n_all - n_full, lambda i, c: chunk(n_all - 1 - i, c, tk, True), carry)
    n_narrow = n_full
    if wide:
        n_narrow = n_full % 2
    carry = lax.fori_loop(0, n_narrow, lambda i, c: chunk(n_full - 1 - i, c, tk), carry)
    if wide:
        carry = lax.fori_loop(0, n_full // 2, lambda i, c: chunk(n_full // 2 - 1 - i, c, 2 * tk), carry)
    for g in range(group):
        o_ref[:, hd(g)] = _rms(carry[g][1], g_ref[:, hd(g)]).astype(o_ref.dtype)


def _attention(kind, q, k, v, g, cf, *, tq, tk, q_off):
    b, sq, width = q.shape
    sk = k.shape[1]
    n_heads = width // HEAD_DIM
    group = FOX_GROUP if kind == "fox" else SB_GROUP
    assert sq % tq == 0 and sk % tk == 0 and q_off % tq == 0 and n_heads % group == 0
    w = group * HEAD_DIM
    wide = sk % (2 * tk) == 0
    q_spec = pl.BlockSpec((None, tq, w), lambda i, h, j: (i, j, h))
    kv_spec = pl.BlockSpec((None, sk, w), lambda i, h, j: (i, 0, h))
    g_spec = pl.BlockSpec((1, w), lambda i, h, j: (0, h))
    if kind == "fox":
        body = functools.partial(_fox_kernel, tq=tq, tk=tk, q_off=q_off, prep=math.gcd(sk, 512), group=group,
                                 wide=wide, one_tile=sq == tq)
        in_specs = [q_spec, kv_spec, kv_spec, pl.BlockSpec((None, n_heads, sk), lambda i, h, j: (i, 0, 0)),
                    pl.BlockSpec((w, 1), lambda i, h, j: (h, 0))]
        args = (q, k, v, cf, g.reshape(width, 1))
        scratch = [pltpu.VMEM((group, sk, 2 * HEAD_DIM), BF16), pltpu.VMEM((group, HEAD_DIM, sk), BF16)]
    else:
        body = functools.partial(_sb_kernel, tq=tq, tk=tk, q_off=q_off, group=group, wide=wide)
        in_specs = [q_spec, kv_spec, kv_spec, g_spec]
        args = (q, k, v, g)
        scratch = []
    return pl.pallas_call(
        body,
        grid=(b, n_heads // group, sq // tq),
        in_specs=in_specs,
        out_specs=q_spec,
        out_shape=jax.ShapeDtypeStruct((b, sq, width), BF16),
        scratch_shapes=scratch,
        compiler_params=_params("arbitrary", "arbitrary", "arbitrary"),
        name=kind,
    )(*args)


def _with_cache_kernel(cache_ref, new_ref, o_ref):
    past, n_heads, _ = cache_ref.shape
    fresh = new_ref.shape[0]
    o_ref[:past, :] = cache_ref[...].reshape(past, n_heads * HEAD_DIM).astype(BF16)
    o_ref[past:past + fresh, :] = new_ref[...]
    o_ref[past + fresh:, :] = jnp.zeros((o_ref.shape[0] - past - fresh, o_ref.shape[1]), BF16)


def _with_cache(cache, layer, new, rows):
    _, b, past, n_heads, _ = cache.shape
    fresh, width = new.shape[1:]
    return pl.pallas_call(
        _with_cache_kernel,
        grid=(b,),
        in_specs=[pl.BlockSpec((None, None, past, n_heads, HEAD_DIM), lambda i: (layer, i, 0, 0, 0)),
                  pl.BlockSpec((None, fresh, width), lambda i: (i, 0, 0))],
        out_specs=pl.BlockSpec((None, rows, width), lambda i: (i, 0, 0)),
        out_shape=jax.ShapeDtypeStruct((b, rows, width), BF16),
        compiler_params=_params("arbitrary"),
        name="with_cache",
    )(cache, new)


def _merge_kernel(nfp_ref, nsp_ref, xp_ref, nfs_ref, nss_ref, xs_ref, wo_ref, g_ref, wr_ref, br_ref,
                  h_ref, hn_ref, lg_ref, *, prompt_blocks):
    def body(nf_ref, ns_ref, x_ref):
        width = nf_ref.shape[1]
        h = x_ref[...] + (jnp.dot(nf_ref[...], wo_ref[:width, :], preferred_element_type=F32)
                          + jnp.dot(ns_ref[...], wo_ref[width:, :], preferred_element_type=F32))
        h_ref[...] = h
        hn = _rms(h, g_ref[...])
        hn_ref[...] = hn
        lg_ref[...] = jnp.dot(hn.astype(BF16), wr_ref[...], preferred_element_type=F32) + br_ref[...]

    is_prompt = pl.program_id(0) < prompt_blocks
    pl.when(is_prompt)(lambda: body(nfp_ref, nsp_ref, xp_ref))
    pl.when(jnp.logical_not(is_prompt))(lambda: body(nfs_ref, nss_ref, xs_ref))


def _merge(prompt, sample, wo, g, wr, br):
    n_p, d = prompt[2].shape
    n_s = sample[2].shape[0]
    width = prompt[0].shape[1]
    tm = ROW_TILE
    pb, sb = n_p // tm, n_s // tm
    n_total = n_p + n_s
    p_row = lambda w: pl.BlockSpec((tm, w), lambda i: (jnp.minimum(i, pb - 1), 0))
    s_row = lambda w: pl.BlockSpec((tm, w), lambda i: (jnp.maximum(i - pb, 0), 0))
    out_row = lambda w: pl.BlockSpec((tm, w), lambda i: (i, 0))
    const = lambda a: pl.BlockSpec(a.shape, lambda i: (0, 0), pipeline_mode=pl.Buffered(1))
    return pl.pallas_call(
        functools.partial(_merge_kernel, prompt_blocks=pb),
        grid=(pb + sb,),
        in_specs=[p_row(width), p_row(width), p_row(d), s_row(width), s_row(width), s_row(d),
                  const(wo), const(g), const(wr), const(br)],
        out_specs=[out_row(d), out_row(d), out_row(LANES)],
        out_shape=[jax.ShapeDtypeStruct((n_total, d), F32), jax.ShapeDtypeStruct((n_total, d), F32),
                   jax.ShapeDtypeStruct((n_total, LANES), F32)],
        compiler_params=_params("arbitrary"),
        name="merge",
    )(*prompt, *sample, wo, g, wr, br)


def _route_kernel(lg_ref, idx_ref, gate_ref, cnt_ref, carry_ref, *, n_groups, per_group):
    tr = lg_ref.shape[0]

    @pl.when(pl.program_id(0) == 0)
    def _():
        carry_ref[...] = jnp.zeros_like(carry_ref)

    x = lg_ref[...]
    lane = lax.broadcasted_iota(jnp.int32, x.shape, 1).astype(F32)
    far = float(LANES)
    rmax = lambda a: jnp.max(a, axis=1, keepdims=True)
    first = lambda hit: jnp.min(jnp.where(hit, lane, far), axis=1, keepdims=True)

    gmask = lane < n_groups
    gl = jnp.where(gmask, x, -jnp.inf)
    gmax = rmax(gl)
    gsel = first(gl == gmax)
    p_g = 1.0 / jnp.sum(jnp.where(gmask, jnp.exp(x - gmax), 0.0), axis=1, keepdims=True)
    lo = n_groups + gsel * per_group
    emask = (lane >= lo) & (lane < lo + per_group)
    el = jnp.where(emask, x, -jnp.inf)
    m1 = rmax(el)
    i1 = first(el == m1)
    el2 = jnp.where(lane == i1, -jnp.inf, el)
    m2 = rmax(el2)
    i2 = first(el2 == m2)
    zsum = jnp.sum(jnp.where(emask, jnp.exp(x - m1), 0.0), axis=1, keepdims=True)
    p1 = 1.0 / zsum
    p2 = jnp.exp(m2 - m1) / zsum
    psum = p1 + p2
    gate1 = p_g * p1 / psum
    gate2 = p_g * p2 / psum
    e1 = i1 - n_groups
    e2 = i2 - n_groups
    hit1 = lane == e1
    hit2 = lane == e2
    onehot = jnp.where(hit1 | hit2, 1.0, 0.0)
    r = lax.broadcasted_iota(jnp.int32, (tr, tr), 0)
    c = lax.broadcasted_iota(jnp.int32, (tr, tr), 1)
    before = jnp.where(c < r, 1.0, 0.0).astype(BF16)
    seen = jnp.dot(before, onehot.astype(BF16), preferred_element_type=F32) + carry_ref[...]
    r1 = jnp.sum(jnp.where(hit1, seen, 0.0), axis=1, keepdims=True)
    r2 = jnp.sum(jnp.where(hit2, seen, 0.0), axis=1, keepdims=True)
    carry_ref[...] = carry_ref[...] + jnp.sum(onehot, axis=0, keepdims=True)
    cnt_ref[...] = carry_ref[...]

    col = lax.broadcasted_iota(jnp.int32, idx_ref.shape, 1)
    idx = jnp.where(col == 0, e1, jnp.where(col == 1, e2, jnp.where(col == 2, r1, r2)))
    idx_ref[...] = idx.astype(jnp.int32)
    gate_ref[...] = jnp.where(col == 0, gate1, jnp.where(col == 1, gate2, 0.0))


def _route(logits, n_groups, per_group):
    n = logits.shape[0]
    tr = ROW_TILE
    narrow = pl.BlockSpec((tr, 8), lambda i: (i, 0))
    return pl.pallas_call(
        functools.partial(_route_kernel, n_groups=n_groups, per_group=per_group),
        grid=(n // tr,),
        in_specs=[pl.BlockSpec((tr, LANES), lambda i: (i, 0))],
        out_specs=[narrow, narrow, pl.BlockSpec((1, LANES), lambda i: (0, 0))],
        out_shape=[jax.ShapeDtypeStruct((n, 8), jnp.int32), jax.ShapeDtypeStruct((n, 8), F32),
                   jax.ShapeDtypeStruct((1, LANES), F32)],
        scratch_shapes=[pltpu.VMEM((1, LANES), F32)],
        compiler_params=_params("arbitrary"),
        name="route",
    )(logits)


def _row_copy(src_ref, src_row, dst_ref, dst_row, sem):
    return pltpu.make_async_copy(src_ref.at[pl.ds(src_row, 1)], dst_ref.at[pl.ds(dst_row, 1)], sem)


def _wait_rows(src_ref, dst_ref, sem, n):
    pltpu.make_async_copy(src_ref.at[pl.ds(0, n)], dst_ref.at[pl.ds(0, n)], sem).wait()


def _dispatch_kernel(dest_ref, hn_ref, xs_ref, sem):
    td = hn_ref.shape[0]

    def start(r, _):
        for k in range(TOP_K):
            _row_copy(hn_ref, r, xs_ref, dest_ref[0, 0, TOP_K * r + k], sem).start()
        return 0

    for r in range(td):
        start(r, 0)
    for _ in range(TOP_K):
        _wait_rows(hn_ref, xs_ref, sem, td)


def _dispatch(dest, hn):
    n, d = hn.shape
    td = ROW_TILE
    return pl.pallas_call(
        _dispatch_kernel,
        grid=(n // td,),
        in_specs=[pl.BlockSpec((1, 1, TOP_K * td), lambda i: (i, 0, 0), memory_space=pltpu.SMEM),
                  pl.BlockSpec((td, d), lambda i: (i, 0))],
        out_specs=pl.BlockSpec(memory_space=pl.ANY),
        out_shape=jax.ShapeDtypeStruct((n * TOP_K, d), F32),
        scratch_shapes=[pltpu.SemaphoreType.DMA(())],
        compiler_params=_params("arbitrary"),
        name="dispatch",
    )(dest.reshape(n // td, 1, TOP_K * td), hn)


def _expert_kernel(blk_ref, exp_ref, lo_ref, hi_ref, first_ref, xs_ref, w1_ref, w3_ref, w2_ref, o_ref):
    w = pl.program_id(0)
    lo, hi = lo_ref[w], hi_ref[w]

    def swiglu():
        rows = lax.broadcasted_iota(jnp.int32, xs_ref.shape, 0)
        x = jnp.where((rows >= lo) & (rows < hi), xs_ref[...], 0.0).astype(BF16)
        a = jnp.dot(x, w1_ref[...], preferred_element_type=F32)
        b = jnp.dot(x, w3_ref[...], preferred_element_type=F32)
        act = (a * jax.nn.sigmoid(a) * b).astype(BF16)
        return jnp.dot(act, w2_ref[...], preferred_element_type=F32)

    @pl.when((hi > lo) & (first_ref[w] == 1))
    def _():
        o_ref[...] = swiglu()

    @pl.when((hi > lo) & (first_ref[w] == 0))
    def _():
        o_ref[...] += swiglu()


def _experts(plan, xs, w1, w3, w2):
    m, d = xs.shape
    de = w1.shape[2]
    blk = EXPERT_BLOCK
    n_items = plan[0].shape[0]
    row_blk = lambda i, rb, ex, lo, hi, fs: (rb[i], 0)
    weight = lambda i, rb, ex, lo, hi, fs: (ex[i], 0, 0)
    return pl.pallas_call(
        _expert_kernel,
        grid_spec=pltpu.PrefetchScalarGridSpec(
            num_scalar_prefetch=5,
            grid=(n_items,),
            in_specs=[pl.BlockSpec((blk, d), row_blk),
                      pl.BlockSpec((None, d, de), weight),
                      pl.BlockSpec((None, d, de), weight),
                      pl.BlockSpec((None, de, d), weight)],
            out_specs=pl.BlockSpec((blk, d), row_blk)),
        out_shape=jax.ShapeDtypeStruct((m, d), F32),
        compiler_params=_params("arbitrary"),
        name="experts",
    )(*plan, xs, w1, w3, w2)


def _combine_kernel(dest_ref, next_ref, gate_ref, h_ref, rows_ref, g_ref, y_ref, buf, sem, *, final_norm):
    i = pl.program_id(0)
    last = pl.num_programs(0) - 1
    tc = h_ref.shape[0]
    slot = i % 2

    def gather(idx_ref, into, unrolled):
        def start(r, _):
            for k in range(TOP_K):
                _row_copy(rows_ref, idx_ref[0, 0, TOP_K * r + k], buf.at[into, k], r, sem.at[into]).start()
            return 0

        if unrolled:
            for r in range(tc):
                start(r, 0)
        else:
            lax.fori_loop(0, tc, start, 0, unroll=ISSUE_UNROLL)

    def drain(into):
        for k in range(TOP_K):
            _wait_rows(rows_ref, buf.at[into, k], sem.at[into], tc)

    pl.when(i == 0)(lambda: gather(dest_ref, 0, False))
    gather(next_ref, 1 - slot, True)
    drain(slot)
    gates = gate_ref[...]
    y = gates[:, 0:1] * buf[slot, 0] + gates[:, 1:2] * buf[slot, 1]
    h = h_ref[...] + y
    y_ref[...] = _rms(h, g_ref[...]) if final_norm else h
    pl.when(i == last)(lambda: drain(1 - slot))


def _combine(dest, gates, h, rows, g, n, row_off, final_norm):
    d = h.shape[1]
    tc = COMBINE_TILE
    off = row_off // tc
    last = off + n // tc - 1
    idx_spec = lambda f: pl.BlockSpec((1, 1, TOP_K * tc), lambda i: (f(i), 0, 0), memory_space=pltpu.SMEM)
    dest3 = dest.reshape(-1, 1, TOP_K * tc)
    return pl.pallas_call(
        functools.partial(_combine_kernel, final_norm=final_norm),
        grid=(n // tc,),
        in_specs=[idx_spec(lambda i: i + off), idx_spec(lambda i: jnp.minimum(i + off + 1, last)),
                  pl.BlockSpec((tc, 8), lambda i: (i + off, 0)),
                  pl.BlockSpec((tc, d), lambda i: (i + off, 0)),
                  pl.BlockSpec(memory_space=pl.ANY),
                  pl.BlockSpec((1, d), lambda i: (0, 0))],
        out_specs=pl.BlockSpec((tc, d), lambda i: (i, 0)),
        out_shape=jax.ShapeDtypeStruct((n, d), F32),
        scratch_shapes=[pltpu.VMEM((2, TOP_K, tc, d), F32), pltpu.SemaphoreType.DMA((2,))],
        compiler_params=_params("arbitrary"),
        name="combine",
    )(dest3, dest3, gates, h, rows, g)


def _round_up(a, m):
    return -(-a // m) * m


def _moe_plan(idx, counts, n_experts):
    blk = EXPERT_BLOCK
    n = idx.shape[0]
    i32 = lambda a: a.astype(jnp.int32)
    pick = lambda table, i: jnp.sum(jnp.where(i[..., None] == jnp.arange(table.shape[0]), table, 0), axis=-1)
    rank_of = lambda ends, x: jnp.sum(i32(ends[None, :] <= x[:, None]), axis=1)
    counts = i32(counts[0, :n_experts])
    end = jnp.cumsum(counts)
    start = end - counts
    dest = i32(pick(start, idx[:, :TOP_K]) + idx[:, TOP_K:2 * TOP_K])
    n_blocks = n * TOP_K // blk
    block_lo = jnp.arange(n_blocks, dtype=jnp.int32) * blk
    e_first = jnp.minimum(rank_of(end, block_lo), n_experts - 1)
    e_last = jnp.minimum(rank_of(end, block_lo + blk - 1), n_experts - 1)
    item_end = jnp.cumsum(e_last - e_first + 1)
    item_start = item_end - (e_last - e_first + 1)
    n_items = n_blocks + n_experts - 1
    w = jnp.arange(n_items, dtype=jnp.int32)
    live = w < item_end[-1]
    rb = jnp.minimum(rank_of(item_end, w), n_blocks - 1)
    ex = jnp.where(live, pick(e_first, rb) + (w - pick(item_start, rb)), e_last[-1])
    lo = jnp.where(live, jnp.clip(pick(start, ex) - rb * blk, 0, blk), 0)
    hi = jnp.where(live, jnp.clip(pick(end, ex) - rb * blk, 0, blk), 0)
    first = i32(w == pick(item_start, rb))
    return dest, (i32(rb), i32(ex), i32(lo), i32(hi), first)


def _split_w_in(w_in, width, n_heads):
    wm = jnp.concatenate([w_in[:, :3 * width], w_in[:, 3 * width + n_heads:]], axis=1).astype(BF16)
    wf = jnp.pad(w_in[:, 3 * width:3 * width + n_heads], ((0, 0), (0, LANES - n_heads))).astype(BF16)
    return wm, wf


def kernel(x_prompt, x_sample, cache_fox_k, cache_fox_v, cache_fox_logf, cache_sb_k, cache_sb_v,
           w_in, b_f, g_attn, g_out_fox, g_out_sb, w_out, g_ffn, w_group, b_group, w_router, b_router,
           w1, w3, w2, g_final):
    depth = w_in.shape[0]
    bp, sp, d = x_prompt.shape
    bs, ss, _ = x_sample.shape
    past = cache_fox_logf.shape[2]
    n_heads = cache_fox_logf.shape[3]
    width = n_heads * HEAD_DIM
    n_groups = w_group.shape[2]
    n_experts = w_router.shape[2]
    per_group = n_experts // n_groups
    n_p, n_s = bp * sp, bs * ss
    sk_s = _round_up(past + ss, LANES)
    pad_s = sk_s - past - ss
    row2 = lambda a: a.reshape(1, -1)

    hp = x_prompt.reshape(n_p, d)
    hs = x_sample.reshape(n_s, d)
    outs_p, outs_s = [], []
    for l in range(depth):
        wm, wf = _split_w_in(w_in[l], width, n_heads)
        bfp = jnp.pad(b_f[l], (0, LANES - n_heads)).reshape(1, LANES)
        wo = w_out[l].astype(BF16)
        wr = jnp.pad(jnp.concatenate([w_group[l], w_router[l]], axis=1),
                     ((0, 0), (0, LANES - n_groups - n_experts))).astype(BF16)
        br = jnp.pad(jnp.concatenate([b_group[l], b_router[l]]), (0, LANES - n_groups - n_experts)).reshape(1, LANES)
        g_a, g_f = row2(g_attn[l]), row2(g_ffn[l])
        g_of, g_os = row2(g_out_fox[l]), row2(g_out_sb[l])

        lf, qf, kfb, kf, vfb, vf, qs, ksb, ks, vsb, vs = _in_proj(hp, g_a, wm, wf, bfp, n_heads)
        b3 = lambda a: a.reshape(bp, sp, -1)
        cf = _cumsum(b3(lf), min(sp, 512))
        tq = min(sp, ATTN_TILE)
        nf = _attention("fox", b3(qf), b3(kfb), b3(vfb), g_of, jnp.swapaxes(cf, 1, 2), tq=tq, tk=tq, q_off=0)
        ns = _attention("sb", b3(qs), b3(ksb), b3(vsb), g_os, None, tq=tq, tk=tq, q_off=0)
        outs_p.append((kf, vf, lf, ks, vs))

        lf2, qf2, kfb2, kf2, vfb2, vf2, qs2, ksb2, ks2, vsb2, vs2 = _in_proj(hs, g_a, wm, wf, bfp, n_heads)
        c3 = lambda a: a.reshape(bs, ss, -1)
        keys = lambda cache, new: _with_cache(cache, l, c3(new), sk_s)
        cf_all = _cumsum(jnp.concatenate([cache_fox_logf[l].astype(F32), c3(lf2)], axis=1), past + ss)
        cf_all = jnp.swapaxes(jnp.pad(cf_all, ((0, 0), (0, pad_s), (0, 0))), 1, 2)
        nf2 = _attention("fox", c3(qf2), keys(cache_fox_k, kfb2), keys(cache_fox_v, vfb2), g_of, cf_all,
                         tq=ss, tk=sk_s, q_off=past)
        ns2 = _attention("sb", c3(qs2), keys(cache_sb_k, ksb2), keys(cache_sb_v, vsb2), g_os, None,
                         tq=ss, tk=sk_s, q_off=past)
        outs_s.append((kf2, vf2, lf2, ks2, vs2))

        h_all, hn_all, logits = _merge((nf.reshape(n_p, width), ns.reshape(n_p, width), hp),
                                       (nf2.reshape(n_s, width), ns2.reshape(n_s, width), hs), wo, g_f, wr, br)
        idx, gates, counts = _route(logits, n_groups, per_group)
        dest, plan = _moe_plan(idx, counts, n_experts)
        xs = _dispatch(dest, hn_all)
        rows = _experts(plan, xs, w1[l].astype(BF16), w3[l].astype(BF16), w2[l].astype(BF16))
        last = l == depth - 1
        g_last = row2(g_final)
        hp = _combine(dest, gates, h_all, rows, g_last, n_p, 0, last)
        hs = _combine(dest, gates, h_all, rows, g_last, n_s, n_p, last)

    heads = lambda a, b, s: a.reshape(b, s, n_heads, HEAD_DIM)
    stack = lambda outs, i, f: jnp.stack([f(o[i]) for o in outs])
    hp_f = lambda a: heads(a, bp, sp)
    hs_f = lambda a: heads(a, bs, ss)
    return (hp.reshape(bp, sp, d), hs.reshape(bs, ss, d),
            stack(outs_p, 0, hp_f), stack(outs_p, 1, hp_f), stack(outs_p, 2, lambda a: a.reshape(bp, sp, n_heads)),
            stack(outs_p, 3, hp_f), stack(outs_p, 4, hp_f),
            stack(outs_s, 0, hs_f), stack(outs_s, 1, hs_f), stack(outs_s, 2, lambda a: a.reshape(bs, ss, n_heads)),
            stack(outs_s, 3, hs_f), stack(outs_s, 4, hs_f))
```

```python
import functools
import math

import jax
import jax.numpy as jnp
from jax import lax
from jax.experimental import pallas as pl
from jax.experimental.pallas import tpu as pltpu

F32 = jnp.float32
BF16 = jnp.bfloat16

HEAD_DIM = 128
TOP_K = 2
EPS = 1e-6
LANES = 128
MXU_DIM = 256
NEG_BIG = -1e30
VMEM_LIMIT = 56 * 1024 * 1024
LOG2E = math.log2(math.e)
SCALE = HEAD_DIM ** -0.5
SCALE_LOG2 = SCALE * LOG2E

ROW_TILE = 256
ATTN_TILE = 512
FOX_GROUP = 4
SB_GROUP = 4
LOOKAHEAD = 2
EXPERT_BLOCK = 256
COMBINE_TILE = 128
ISSUE_UNROLL = 8


def _params(*sem):
    return pltpu.CompilerParams(dimension_semantics=sem, vmem_limit_bytes=VMEM_LIMIT)


def _log_sigmoid(x):
    return jnp.minimum(x, 0.0) - jnp.log1p(jnp.exp(-jnp.abs(x)))


def _rms(x, g):
    return x * lax.rsqrt(jnp.mean(x * x, axis=-1, keepdims=True) + EPS) * g


def _split3(x):
    hi = x.astype(BF16).astype(F32)
    r1 = x - hi
    mid = r1.astype(BF16).astype(F32)
    lo = (r1 - mid).astype(BF16).astype(F32)
    return hi, mid, lo


def _running_sum(x):
    n = x.shape[0]
    r = lax.broadcasted_iota(jnp.int32, (n, n), 0)
    c = lax.broadcasted_iota(jnp.int32, (n, n), 1)
    tri = jnp.where(c <= r, 1.0, 0.0).astype(BF16)
    return sum(jnp.dot(tri, p.astype(BF16), preferred_element_type=F32) for p in _split3(x))


def _in_proj_kernel(x_ref, g_ref, wm_ref, wf_ref, bf_ref, lf_ref, *rest, width, n_heads, seq_tiles):
    hn = _rms(x_ref[...], g_ref[...]).astype(BF16)
    refs = iter(rest[1:-1] if seq_tiles else rest)
    for c in range(6):
        res = jnp.dot(hn, wm_ref[:, c * width:(c + 1) * width], preferred_element_type=F32)
        next(refs)[...] = (res if c % 3 else res * SCALE_LOG2).astype(BF16)
        if c % 3:
            kv_ref = next(refs)
            kv_ref[...] = res.reshape(kv_ref.shape)
    lf = _log_sigmoid(jnp.dot(hn, wf_ref[...], preferred_element_type=F32) + bf_ref[...])
    lf_ref[...] = lf[:, :n_heads]
    if seq_tiles:
        cf_ref, carry_ref = rest[0], rest[-1]

        @pl.when(pl.program_id(0) % seq_tiles == 0)
        def _():
            carry_ref[...] = jnp.zeros_like(carry_ref)

        cf = _running_sum(lf) + carry_ref[...]
        cf_ref[...] = cf[:, :n_heads]
        carry_ref[...] = cf[lf.shape[0] - 1:, :]


def _in_proj(x, g, wm, wf, bf, n_heads, seq_tiles=0):
    n, d = x.shape
    width = wm.shape[1] // 6
    tm = ROW_TILE
    row = lambda w: pl.BlockSpec((tm, w), lambda i: (i, 0))
    const = lambda a: pl.BlockSpec(a.shape, lambda i: (0, 0), pipeline_mode=pl.Buffered(1))
    per_head = pl.BlockSpec((tm, n_heads, HEAD_DIM), lambda i: (i, 0, 0))
    rows_out = jax.ShapeDtypeStruct((n, width), BF16)
    heads_out = jax.ShapeDtypeStruct((n, n_heads, HEAD_DIM), F32)
    narrow_out = jax.ShapeDtypeStruct((n, n_heads), F32)
    mixer_specs = [row(width)] + [row(width), per_head] * 2
    mixer_shapes = [rows_out] + [rows_out, heads_out] * 2
    n_narrow = 2 if seq_tiles else 1
    return pl.pallas_call(
        functools.partial(_in_proj_kernel, width=width, n_heads=n_heads, seq_tiles=seq_tiles),
        grid=(n // tm,),
        in_specs=[row(d), const(g), const(wm), const(wf), const(bf)],
        out_specs=[row(n_heads)] * n_narrow + mixer_specs * 2,
        out_shape=[narrow_out] * n_narrow + mixer_shapes * 2,
        scratch_shapes=[pltpu.VMEM((1, LANES), F32)] if seq_tiles else [],
        compiler_params=_params("arbitrary"),
        name="in_proj",
    )(x, g, wm, wf, bf)


def _cumsum_kernel(lf_ref, cf_ref, carry_ref):
    tt = lf_ref.shape[0]

    @pl.when(pl.program_id(1) == 0)
    def _():
        carry_ref[...] = jnp.zeros_like(carry_ref)

    cf = _running_sum(lf_ref[...]) + carry_ref[...]
    cf_ref[...] = cf
    carry_ref[...] = cf[tt - 1:tt, :]


def _cumsum(lf, tt):
    b, t, h = lf.shape
    return pl.pallas_call(
        _cumsum_kernel,
        grid=(b, t // tt),
        in_specs=[pl.BlockSpec((None, tt, h), lambda i, j: (i, j, 0))],
        out_specs=pl.BlockSpec((None, tt, h), lambda i, j: (i, j, 0)),
        out_shape=jax.ShapeDtypeStruct((b, t, h), F32),
        scratch_shapes=[pltpu.VMEM((1, h), F32)],
        compiler_params=_params("arbitrary", "arbitrary"),
        name="cumsum",
    )(lf)


def _bias_block(c_rows, h, as_query):
    n = c_rows.shape[1]
    sub = lax.broadcasted_iota(jnp.int32, c_rows.shape, 0)
    c = jnp.sum(jnp.where(sub == h, c_rows, 0.0), axis=0, keepdims=True) * LOG2E
    hi, mid, lo = _split3(c)
    if as_query:
        parts, base, ones_lo = (hi, mid, lo), 0, 3
    else:
        parts, base, ones_lo = (-hi, -mid, -lo), 3, 0
    row = lax.broadcasted_iota(jnp.int32, (8, n), 0)
    blk = jnp.where((row >= ones_lo) & (row < ones_lo + 3), 1.0, 0.0)
    for i, p in enumerate(parts):
        blk = jnp.where(row == base + i, p, blk)
    full = jnp.concatenate([blk, jnp.zeros((LANES - 8, n), F32)], axis=0)
    return full.T.astype(BF16)


def _fox_kernel(q_ref, k_ref, v_ref, c_ref, gt_ref, o_ref, kaug, vt, *, tq, tk, q_off, prep, group, wide, one_tile):
    hg = pl.program_id(1)
    qi = pl.program_id(2)
    sk = k_ref.shape[0]
    hd = lambda g: slice(g * HEAD_DIM, (g + 1) * HEAD_DIM)

    @pl.when(qi == 0)
    def _():
        for g in range(group):
            for r0 in range(0, sk, prep):
                rows = pl.ds(r0, prep)
                kaug[g, rows, :HEAD_DIM] = k_ref[rows, hd(g)]
                kaug[g, rows, HEAD_DIM:] = _bias_block(c_ref[:, r0:r0 + prep], hg * group + g, False)
                vt[g, :, rows] = v_ref[rows, hd(g)].astype(F32).T.astype(BF16)

    q_lo = q_off if one_tile else pl.multiple_of(q_off + qi * tq, tq)
    c_q = c_ref[:, pl.ds(q_lo, tq)]
    qa = [jnp.concatenate([q_ref[:, hd(g)], _bias_block(c_q, hg * group + g, True)], axis=1) for g in range(group)]
    kiota = lax.broadcasted_iota(jnp.int32, (tk, tq), 0)
    qpos = q_lo + lax.broadcasted_iota(jnp.int32, (tk, tq), 1)

    def chunk(j, carry, width, masked=False):
        k0 = pl.multiple_of(j * width, width)

        def logits(g):
            s = lax.dot_general(kaug[g, pl.ds(k0, width), :], qa[g], (((1,), (1,)), ((), ())),
                                preferred_element_type=F32)
            if masked:
                s = jnp.where(k0 + kiota <= qpos, s, NEG_BIG)
            return s

        def absorb(g, s):
            m, l, acc = carry[g]
            m_new = jnp.maximum(m, jnp.max(s, axis=0, keepdims=True))
            p = jnp.exp2(s - m_new)
            alpha = jnp.exp2(m - m_new)
            l = alpha * l + jnp.sum(p, axis=0, keepdims=True)
            acc = alpha * acc + jnp.dot(vt[g, :, pl.ds(k0, width)], p.astype(BF16), preferred_element_type=F32)
            return m_new, l, acc

        out, pending = [], {}
        for g in range(group + LOOKAHEAD):
            if g < group:
                pending[g] = logits(g)
            if g >= LOOKAHEAD:
                out.append(absorb(g - LOOKAHEAD, pending.pop(g - LOOKAHEAD)))
        return tuple(out)

    n_full = (q_lo + 1) // tk
    n_all = (q_lo + tq + tk - 1) // tk
    carry = tuple((jnp.full((1, tq), NEG_BIG, F32), jnp.zeros((1, tq), F32), jnp.zeros((HEAD_DIM, tq), F32))
                  for _ in range(group))
    n_narrow = n_full
    if wide:
        carry = lax.fori_loop(0, n_full // 2, lambda j, c: chunk(j, c, 2 * tk), carry)
        n_narrow = n_full % 2
    carry = lax.fori_loop(n_full - n_narrow, n_full, lambda j, c: chunk(j, c, tk), carry)
    carry = lax.fori_loop(n_full, n_all, lambda j, c: chunk(j, c, tk, True), carry)
    for g in range(group):
        m, l, acc = carry[g]
        o = acc / l
        y = o * lax.rsqrt(jnp.mean(o * o, axis=0, keepdims=True) + EPS) * gt_ref[hd(g), :]
        o_ref[:, hd(g)] = y.T.astype(o_ref.dtype)


def _sb_kernel(q_ref, k_ref, v_ref, g_ref, o_ref, *, tq, tk, q_off, group, wide):
    qi = pl.program_id(2)
    hd = lambda g: slice(g * HEAD_DIM, (g + 1) * HEAD_DIM)
    q = [q_ref[:, hd(g)] for g in range(group)]
    q_lo = q_off + qi * tq
    ts = MXU_DIM if tk % MXU_DIM == 0 else LANES
    r = lax.broadcasted_iota(jnp.int32, (ts, ts), 0)
    c = lax.broadcasted_iota(jnp.int32, (ts, ts), 1)
    neg_from = jnp.where(r >= c, -1.0, 0.0).astype(BF16)
    sign_bit = jnp.uint32(0x80000000)

    def chunk(j, carry, width, masked=False):
        k0 = pl.multiple_of(j * width, width)
        n_sub = width // ts
        first_row = (lambda b: b * ts) if (masked and tq == tk) else (lambda b: 0)

        def logits(g, b):
            r0 = first_row(b)
            z = lax.dot_general(q[g][r0:], k_ref[pl.ds(k0 + b * ts, ts), hd(g)], (((1,), (1,)), ((), ())),
                                preferred_element_type=F32)
            neg_abs = pltpu.bitcast(pltpu.bitcast(z, jnp.uint32) | sign_bit, F32)
            sp = jnp.maximum(z, 0.0) + jnp.log2(1.0 + jnp.exp2(neg_abs))
            valid = None
            if masked:
                qpos = q_lo + r0 + lax.broadcasted_iota(jnp.int32, z.shape, 0)
                valid = k0 + b * ts + lax.broadcasted_iota(jnp.int32, z.shape, 1) < qpos
                sp = jnp.where(valid, sp, 0.0)
            hi = sp.astype(BF16)
            lo = (sp - hi.astype(F32)).astype(BF16)
            return z, hi, lo, valid

        def weights(state, after, r0):
            z, hi, lo, valid = state
            sfx = (jnp.dot(hi, neg_from, preferred_element_type=F32)
                   + jnp.dot(lo, neg_from, preferred_element_type=F32)) + after[r0:]
            a = jnp.exp2(z + sfx)
            if masked:
                a = jnp.where(valid, a, 0.0)
            total = sfx[:, 0:1]
            return (jnp.concatenate([after[:r0], total], axis=0) if r0 else total), a.astype(BF16)

        items = [(g, b) for b in reversed(range(n_sub)) for g in range(group)]
        after = [carry[g][0] for g in range(group)]
        acc = [carry[g][1] for g in range(group)]
        staged, ready = {}, {}
        n = len(items)
        for step in range(n + 2 * LOOKAHEAD):
            if step < n:
                staged[step] = logits(*items[step])
            if 0 <= step - LOOKAHEAD < n:
                g, b = items[step - LOOKAHEAD]
                after[g], ready[step - LOOKAHEAD] = weights(staged.pop(step - LOOKAHEAD), after[g], first_row(b))
            if 0 <= step - 2 * LOOKAHEAD < n:
                g, b = items[step - 2 * LOOKAHEAD]
                r0 = first_row(b)
                rows = pl.ds(k0 + b * ts, ts)
                pv = jnp.dot(ready.pop(step - 2 * LOOKAHEAD), v_ref[rows, hd(g)], preferred_element_type=F32)
                acc[g] = jnp.concatenate([acc[g][:r0], acc[g][r0:] + pv], axis=0) if r0 else acc[g] + pv
        return tuple((after[g], acc[g]) for g in range(group))

    n_full = q_lo // tk
    n_all = (q_lo + tq - 1 + tk - 1) // tk
    carry = tuple((jnp.zeros((tq, 1), F32), jnp.zeros((tq, HEAD_DIM), F32)) for _ in range(group))
    carry = lax.fori_loop(0, n_all - n_full, lambda i, c: chunk(n_all - 1 - i, c, tk, True), carry)
    n_narrow = n_full
    if wide:
        n_narrow = n_full % 2
    carry = lax.fori_loop(0, n_narrow, lambda i, c: chunk(n_full - 1 - i, c, tk), carry)
    if wide:
        carry = lax.fori_loop(0, n_full // 2, lambda i, c: chunk(n_full // 2 - 1 - i, c, 2 * tk), carry)
    for g in range(group):
        o_ref[:, hd(g)] = _rms(carry[g][1], g_ref[:, hd(g)]).astype(o_ref.dtype)


def _attention(kind, q, k, v, g, cf, *, tq, tk, q_off):
    b, sq, width = q.shape
    sk = k.shape[1]
    n_heads = width // HEAD_DIM
    group = FOX_GROUP if kind == "fox" else SB_GROUP
    assert sq % tq == 0 and sk % tk == 0 and q_off % tq == 0 and n_heads % group == 0
    w = group * HEAD_DIM
    wide = sk % (2 * tk) == 0
    q_spec = pl.BlockSpec((None, tq, w), lambda i, h, j: (i, j, h))
    kv_spec = pl.BlockSpec((None, sk, w), lambda i, h, j: (i, 0, h))
    g_spec = pl.BlockSpec((1, w), lambda i, h, j: (0, h))
    if kind == "fox":
        body = functools.partial(_fox_kernel, tq=tq, tk=tk, q_off=q_off, prep=math.gcd(sk, 512), group=group,
                                 wide=wide, one_tile=sq == tq)
        in_specs = [q_spec, kv_spec, kv_spec, pl.BlockSpec((None, n_heads, sk), lambda i, h, j: (i, 0, 0)),
                    pl.BlockSpec((w, 1), lambda i, h, j: (h, 0))]
        args = (q, k, v, cf, g.reshape(width, 1))
        scratch = [pltpu.VMEM((group, sk, 2 * HEAD_DIM), BF16), pltpu.VMEM((group, HEAD_DIM, sk), BF16)]
    else:
        body = functools.partial(_sb_kernel, tq=tq, tk=tk, q_off=q_off, group=group, wide=wide)
        in_specs = [q_spec, kv_spec, kv_spec, g_spec]
        args = (q, k, v, g)
        scratch = []
    return pl.pallas_call(
        body,
        grid=(b, n_heads // group, sq // tq),
        in_specs=in_specs,
        out_specs=q_spec,
        out_shape=jax.ShapeDtypeStruct((b, sq, width), BF16),
        scratch_shapes=scratch,
        compiler_params=_params("arbitrary", "arbitrary", "arbitrary"),
        name=kind,
    )(*args)


def _with_cache_kernel(cache_ref, new_ref, o_ref):
    past, n_heads, _ = cache_ref.shape
    fresh = new_ref.shape[0]
    o_ref[:past, :] = cache_ref[...].reshape(past, n_heads * HEAD_DIM).astype(BF16)
    o_ref[past:past + fresh, :] = new_ref[...]
    o_ref[past + fresh:, :] = jnp.zeros((o_ref.shape[0] - past - fresh, o_ref.shape[1]), BF16)


def _with_cache(cache, layer, new, rows):
    _, b, past, n_heads, _ = cache.shape
    fresh, width = new.shape[1:]
    return pl.pallas_call(
        _with_cache_kernel,
        grid=(b,),
        in_specs=[pl.BlockSpec((None, None, past, n_heads, HEAD_DIM), lambda i: (layer, i, 0, 0, 0)),
                  pl.BlockSpec((None, fresh, width), lambda i: (i, 0, 0))],
        out_specs=pl.BlockSpec((None, rows, width), lambda i: (i, 0, 0)),
        out_shape=jax.ShapeDtypeStruct((b, rows, width), BF16),
        compiler_params=_params("arbitrary"),
        name="with_cache",
    )(cache, new)


def _merge_kernel(nfp_ref, nsp_ref, xp_ref, nfs_ref, nss_ref, xs_ref, wo_ref, g_ref, wr_ref, br_ref,
                  h_ref, hn_ref, lg_ref, *, prompt_blocks):
    def body(nf_ref, ns_ref, x_ref):
        width = nf_ref.shape[1]
        h = x_ref[...] + (jnp.dot(nf_ref[...], wo_ref[:width, :], preferred_element_type=F32)
                          + jnp.dot(ns_ref[...], wo_ref[width:, :], preferred_element_type=F32))
        h_ref[...] = h
        hn = _rms(h, g_ref[...])
        hn_ref[...] = hn
        lg_ref[...] = jnp.dot(hn.astype(BF16), wr_ref[...], preferred_element_type=F32) + br_ref[...]

    is_prompt = pl.program_id(0) < prompt_blocks
    pl.when(is_prompt)(lambda: body(nfp_ref, nsp_ref, xp_ref))
    pl.when(jnp.logical_not(is_prompt))(lambda: body(nfs_ref, nss_ref, xs_ref))


def _merge(prompt, sample, wo, g, wr, br):
    n_p, d = prompt[2].shape
    n_s = sample[2].shape[0]
    width = prompt[0].shape[1]
    tm = ROW_TILE
    pb, sb = n_p // tm, n_s // tm
    n_total = n_p + n_s
    p_row = lambda w: pl.BlockSpec((tm, w), lambda i: (jnp.minimum(i, pb - 1), 0))
    s_row = lambda w: pl.BlockSpec((tm, w), lambda i: (jnp.maximum(i - pb, 0), 0))
    out_row = lambda w: pl.BlockSpec((tm, w), lambda i: (i, 0))
    const = lambda a: pl.BlockSpec(a.shape, lambda i: (0, 0), pipeline_mode=pl.Buffered(1))
    return pl.pallas_call(
        functools.partial(_merge_kernel, prompt_blocks=pb),
        grid=(pb + sb,),
        in_specs=[p_row(width), p_row(width), p_row(d), s_row(width), s_row(width), s_row(d),
                  const(wo), const(g), const(wr), const(br)],
        out_specs=[out_row(d), out_row(d), out_row(LANES)],
        out_shape=[jax.ShapeDtypeStruct((n_total, d), F32), jax.ShapeDtypeStruct((n_total, d), F32),
                   jax.ShapeDtypeStruct((n_total, LANES), F32)],
        compiler_params=_params("arbitrary"),
        name="merge",
    )(*prompt, *sample, wo, g, wr, br)


def _route_kernel(lg_ref, idx_ref, gate_ref, cnt_ref, carry_ref, *, n_groups, per_group):
    tr = lg_ref.shape[0]

    @pl.when(pl.program_id(0) == 0)
    def _():
        carry_ref[...] = jnp.zeros_like(carry_ref)

    x = lg_ref[...]
    lane = lax.broadcasted_iota(jnp.int32, x.shape, 1).astype(F32)
    far = float(LANES)
    rmax = lambda a: jnp.max(a, axis=1, keepdims=True)
    first = lambda hit: jnp.min(jnp.where(hit, lane, far), axis=1, keepdims=True)

    gmask = lane < n_groups
    gl = jnp.where(gmask, x, -jnp.inf)
    gmax = rmax(gl)
    gsel = first(gl == gmax)
    p_g = 1.0 / jnp.sum(jnp.where(gmask, jnp.exp(x - gmax), 0.0), axis=1, keepdims=True)
    lo = n_groups + gsel * per_group
    emask = (lane >= lo) & (lane < lo + per_group)
    el = jnp.where(emask, x, -jnp.inf)
    m1 = rmax(el)
    i1 = first(el == m1)
    el2 = jnp.where(lane == i1, -jnp.inf, el)
    m2 = rmax(el2)
    i2 = first(el2 == m2)
    zsum = jnp.sum(jnp.where(emask, jnp.exp(x - m1), 0.0), axis=1, keepdims=True)
    p1 = 1.0 / zsum
    p2 = jnp.exp(m2 - m1) / zsum
    psum = p1 + p2
    gate1 = p_g * p1 / psum
    gate2 = p_g * p2 / psum
    e1 = i1 - n_groups
    e2 = i2 - n_groups
    hit1 = lane == e1
    hit2 = lane == e2
    onehot = jnp.where(hit1 | hit2, 1.0, 0.0)
    r = lax.broadcasted_iota(jnp.int32, (tr, tr), 0)
    c = lax.broadcasted_iota(jnp.int32, (tr, tr), 1)
    before = jnp.where(c < r, 1.0, 0.0).astype(BF16)
    seen = jnp.dot(before, onehot.astype(BF16), preferred_element_type=F32) + carry_ref[...]
    r1 = jnp.sum(jnp.where(hit1, seen, 0.0), axis=1, keepdims=True)
    r2 = jnp.sum(jnp.where(hit2, seen, 0.0), axis=1, keepdims=True)
    carry_ref[...] = carry_ref[...] + jnp.sum(onehot, axis=0, keepdims=True)
    cnt_ref[...] = carry_ref[...]

    col = lax.broadcasted_iota(jnp.int32, idx_ref.shape, 1)
    idx = jnp.where(col == 0, e1, jnp.where(col == 1, e2, jnp.where(col == 2, r1, r2)))
    idx_ref[...] = idx.astype(jnp.int32)
    gate_ref[...] = jnp.where(col == 0, gate1, jnp.where(col == 1, gate2, 0.0))


def _route(logits, n_groups, per_group):
    n = logits.shape[0]
    tr = ROW_TILE
    narrow = pl.BlockSpec((tr, 8), lambda i: (i, 0))
    return pl.pallas_call(
        functools.partial(_route_kernel, n_groups=n_groups, per_group=per_group),
        grid=(n // tr,),
        in_specs=[pl.BlockSpec((tr, LANES), lambda i: (i, 0))],
        out_specs=[narrow, narrow, pl.BlockSpec((1, LANES), lambda i: (0, 0))],
        out_shape=[jax.ShapeDtypeStruct((n, 8), jnp.int32), jax.ShapeDtypeStruct((n, 8), F32),
                   jax.ShapeDtypeStruct((1, LANES), F32)],
        scratch_shapes=[pltpu.VMEM((1, LANES), F32)],
        compiler_params=_params("arbitrary"),
        name="route",
    )(logits)


def _row_copy(src_ref, src_row, dst_ref, dst_row, sem):
    return pltpu.make_async_copy(src_ref.at[pl.ds(src_row, 1)], dst_ref.at[pl.ds(dst_row, 1)], sem)


def _wait_rows(src_ref, dst_ref, sem, n):
    pltpu.make_async_copy(src_ref.at[pl.ds(0, n)], dst_ref.at[pl.ds(0, n)], sem).wait()


def _dispatch_kernel(dest_ref, hn_ref, xs_ref, sem):
    td = hn_ref.shape[0]

    def start(r, _):
        for k in range(TOP_K):
            _row_copy(hn_ref, r, xs_ref, dest_ref[0, 0, TOP_K * r + k], sem).start()
        return 0

    for r in range(td):
        start(r, 0)
    for _ in range(TOP_K):
        _wait_rows(hn_ref, xs_ref, sem, td)


def _dispatch(dest, hn):
    n, d = hn.shape
    td = ROW_TILE
    return pl.pallas_call(
        _dispatch_kernel,
        grid=(n // td,),
        in_specs=[pl.BlockSpec((1, 1, TOP_K * td), lambda i: (i, 0, 0), memory_space=pltpu.SMEM),
                  pl.BlockSpec((td, d), lambda i: (i, 0))],
        out_specs=pl.BlockSpec(memory_space=pl.ANY),
        out_shape=jax.ShapeDtypeStruct((n * TOP_K, d), F32),
        scratch_shapes=[pltpu.SemaphoreType.DMA(())],
        compiler_params=_params("arbitrary"),
        name="dispatch",
    )(dest.reshape(n // td, 1, TOP_K * td), hn)


def _expert_kernel(blk_ref, exp_ref, lo_ref, hi_ref, first_ref, xs_ref, w1_ref, w3_ref, w2_ref, o_ref):
    w = pl.program_id(0)
    lo, hi = lo_ref[w], hi_ref[w]

    def swiglu():
        rows = lax.broadcasted_iota(jnp.int32, xs_ref.shape, 0)
        x = jnp.where((rows >= lo) & (rows < hi), xs_ref[...], 0.0).astype(BF16)
        a = jnp.dot(x, w1_ref[...], preferred_element_type=F32)
        b = jnp.dot(x, w3_ref[...], preferred_element_type=F32)
        act = (a * jax.nn.sigmoid(a) * b).astype(BF16)
        return jnp.dot(act, w2_ref[...], preferred_element_type=F32)

    @pl.when((hi > lo) & (first_ref[w] == 1))
    def _():
        o_ref[...] = swiglu()

    @pl.when((hi > lo) & (first_ref[w] == 0))
    def _():
        o_ref[...] += swiglu()


def _experts(plan, xs, w1, w3, w2):
    m, d = xs.shape
    de = w1.shape[2]
    blk = EXPERT_BLOCK
    n_items = plan[0].shape[0]
    row_blk = lambda i, rb, ex, lo, hi, fs: (rb[i], 0)
    weight = lambda i, rb, ex, lo, hi, fs: (ex[i], 0, 0)
    return pl.pallas_call(
        _expert_kernel,
        grid_spec=pltpu.PrefetchScalarGridSpec(
            num_scalar_prefetch=5,
            grid=(n_items,),
            in_specs=[pl.BlockSpec((blk, d), row_blk),
                      pl.BlockSpec((None, d, de), weight),
                      pl.BlockSpec((None, d, de), weight),
                      pl.BlockSpec((None, de, d), weight)],
            out_specs=pl.BlockSpec((blk, d), row_blk)),
        out_shape=jax.ShapeDtypeStruct((m, d), F32),
        compiler_params=_params("arbitrary"),
        name="experts",
    )(*plan, xs, w1, w3, w2)


def _combine_kernel(dest_ref, next_ref, gate_ref, h_ref, rows_ref, g_ref, y_ref, buf, sem, *, final_norm):
    i = pl.program_id(0)
    last = pl.num_programs(0) - 1
    tc = h_ref.shape[0]
    slot = i % 2

    def gather(idx_ref, into, unrolled):
        def start(r, _):
            for k in range(TOP_K):
                _row_copy(rows_ref, idx_ref[0, 0, TOP_K * r + k], buf.at[into, k], r, sem.at[into]).start()
            return 0

        if unrolled:
            for r in range(tc):
                start(r, 0)
        else:
            lax.fori_loop(0, tc, start, 0, unroll=ISSUE_UNROLL)

    def drain(into):
        for k in range(TOP_K):
            _wait_rows(rows_ref, buf.at[into, k], sem.at[into], tc)

    pl.when(i == 0)(lambda: gather(dest_ref, 0, False))
    gather(next_ref, 1 - slot, True)
    drain(slot)
    gates = gate_ref[...]
    y = gates[:, 0:1] * buf[slot, 0] + gates[:, 1:2] * buf[slot, 1]
    h = h_ref[...] + y
    y_ref[...] = _rms(h, g_ref[...]) if final_norm else h
    pl.when(i == last)(lambda: drain(1 - slot))


def _combine(dest, gates, h, rows, g, n, row_off, final_norm):
    d = h.shape[1]
    tc = COMBINE_TILE
    off = row_off // tc
    last = off + n // tc - 1
    idx_spec = lambda f: pl.BlockSpec((1, 1, TOP_K * tc), lambda i: (f(i), 0, 0), memory_space=pltpu.SMEM)
    dest3 = dest.reshape(-1, 1, TOP_K * tc)
    return pl.pallas_call(
        functools.partial(_combine_kernel, final_norm=final_norm),
        grid=(n // tc,),
        in_specs=[idx_spec(lambda i: i + off), idx_spec(lambda i: jnp.minimum(i + off + 1, last)),
                  pl.BlockSpec((tc, 8), lambda i: (i + off, 0)),
                  pl.BlockSpec((tc, d), lambda i: (i + off, 0)),
                  pl.BlockSpec(memory_space=pl.ANY),
                  pl.BlockSpec((1, d), lambda i: (0, 0))],
        out_specs=pl.BlockSpec((tc, d), lambda i: (i, 0)),
        out_shape=jax.ShapeDtypeStruct((n, d), F32),
        scratch_shapes=[pltpu.VMEM((2, TOP_K, tc, d), F32), pltpu.SemaphoreType.DMA((2,))],
        compiler_params=_params("arbitrary"),
        name="combine",
    )(dest3, dest3, gates, h, rows, g)


def _round_up(a, m):
    return -(-a // m) * m


def _moe_plan(idx, counts, n_experts):
    blk = EXPERT_BLOCK
    n = idx.shape[0]
    i32 = lambda a: a.astype(jnp.int32)
    pick = lambda table, i: jnp.sum(jnp.where(i[..., None] == jnp.arange(table.shape[0]), table, 0), axis=-1)
    rank_of = lambda ends, x: jnp.sum(i32(ends[None, :] <= x[:, None]), axis=1)
    counts = i32(counts[0, :n_experts])
    end = jnp.cumsum(counts)
    start = end - counts
    dest = i32(pick(start, idx[:, :TOP_K]) + idx[:, TOP_K:2 * TOP_K])
    n_blocks = n * TOP_K // blk
    block_lo = jnp.arange(n_blocks, dtype=jnp.int32) * blk
    e_first = jnp.minimum(rank_of(end, block_lo), n_experts - 1)
    e_last = jnp.minimum(rank_of(end, block_lo + blk - 1), n_experts - 1)
    item_end = jnp.cumsum(e_last - e_first + 1)
    item_start = item_end - (e_last - e_first + 1)
    n_items = n_blocks + n_experts - 1
    w = jnp.arange(n_items, dtype=jnp.int32)
    live = w < item_end[-1]
    rb = jnp.minimum(rank_of(item_end, w), n_blocks - 1)
    ex = jnp.where(live, pick(e_first, rb) + (w - pick(item_start, rb)), e_last[-1])
    lo = jnp.where(live, jnp.clip(pick(start, ex) - rb * blk, 0, blk), 0)
    hi = jnp.where(live, jnp.clip(pick(end, ex) - rb * blk, 0, blk), 0)
    first = i32(w == pick(item_start, rb))
    return dest, (i32(rb), i32(ex), i32(lo), i32(hi), first)


def _split_w_in(w_in, width, n_heads):
    wm = jnp.concatenate([w_in[:, :3 * width], w_in[:, 3 * width + n_heads:]], axis=1).astype(BF16)
    wf = jnp.pad(w_in[:, 3 * width:3 * width + n_heads], ((0, 0), (0, LANES - n_heads))).astype(BF16)
    return wm, wf


def kernel(x_prompt, x_sample, cache_fox_k, cache_fox_v, cache_fox_logf, cache_sb_k, cache_sb_v,
           w_in, b_f, g_attn, g_out_fox, g_out_sb, w_out, g_ffn, w_group, b_group, w_router, b_router,
           w1, w3, w2, g_final):
    depth = w_in.shape[0]
    bp, sp, d = x_prompt.shape
    bs, ss, _ = x_sample.shape
    past = cache_fox_logf.shape[2]
    n_heads = cache_fox_logf.shape[3]
    width = n_heads * HEAD_DIM
    n_groups = w_group.shape[2]
    n_experts = w_router.shape[2]
    per_group = n_experts // n_groups
    n_p, n_s = bp * sp, bs * ss
    sk_s = _round_up(past + ss, LANES)
    pad_s = sk_s - past - ss
    row2 = lambda a: a.reshape(1, -1)

    hp = x_prompt.reshape(n_p, d)
    hs = x_sample.reshape(n_s, d)
    outs_p, outs_s = [], []
    for l in range(depth):
        wm, wf = _split_w_in(w_in[l], width, n_heads)
        bfp = jnp.pad(b_f[l], (0, LANES - n_heads)).reshape(1, LANES)
        wo = w_out[l].astype(BF16)
        wr = jnp.pad(jnp.concatenate([w_group[l], w_router[l]], axis=1),
                     ((0, 0), (0, LANES - n_groups - n_experts))).astype(BF16)
        br = jnp.pad(jnp.concatenate([b_group[l], b_router[l]]), (0, LANES - n_groups - n_experts)).reshape(1, LANES)
        g_a, g_f = row2(g_attn[l]), row2(g_ffn[l])
        g_of, g_os = row2(g_out_fox[l]), row2(g_out_sb[l])

        lf, cf, qf, kfb, kf, vfb, vf, qs, ksb, ks, vsb, vs = _in_proj(hp, g_a, wm, wf, bfp, n_heads, sp // ROW_TILE)
        b3 = lambda a: a.reshape(bp, sp, -1)
        tq = min(sp, ATTN_TILE)
        nf = _attention("fox", b3(qf), b3(kfb), b3(vfb), g_of, jnp.swapaxes(b3(cf), 1, 2), tq=tq, tk=tq, q_off=0)
        ns = _attention("sb", b3(qs), b3(ksb), b3(vsb), g_os, None, tq=tq, tk=tq, q_off=0)
        outs_p.append((kf, vf, lf, ks, vs))

        lf2, qf2, kfb2, kf2, vfb2, vf2, qs2, ksb2, ks2, vsb2, vs2 = _in_proj(hs, g_a, wm, wf, bfp, n_heads)
        c3 = lambda a: a.reshape(bs, ss, -1)
        keys = lambda cache, new: _with_cache(cache, l, c3(new), sk_s)
        cf_all = _cumsum(jnp.concatenate([cache_fox_logf[l].astype(F32), c3(lf2)], axis=1), past + ss)
        cf_all = jnp.swapaxes(jnp.pad(cf_all, ((0, 0), (0, pad_s), (0, 0))), 1, 2)
        nf2 = _attention("fox", c3(qf2), keys(cache_fox_k, kfb2), keys(cache_fox_v, vfb2), g_of, cf_all,
                         tq=ss, tk=sk_s, q_off=past)
        ns2 = _attention("sb", c3(qs2), keys(cache_sb_k, ksb2), keys(cache_sb_v, vsb2), g_os, None,
                         tq=ss, tk=sk_s, q_off=past)
        outs_s.append((kf2, vf2, lf2, ks2, vs2))

        h_all, hn_all, logits = _merge((nf.reshape(n_p, width), ns.reshape(n_p, width), hp),
                                       (nf2.reshape(n_s, width), ns2.reshape(n_s, width), hs), wo, g_f, wr, br)
        idx, gates, counts = _route(logits, n_groups, per_group)
        dest, plan = _moe_plan(idx, counts, n_experts)
        xs = _dispatch(dest, hn_all)
        rows = _experts(plan, xs, w1[l].astype(BF16), w3[l].astype(BF16), w2[l].astype(BF16))
        last = l == depth - 1
        g_last = row2(g_final)
        hp = _combine(dest, gates, h_all, rows, g_last, n_p, 0, last)
        hs = _combine(dest, gates, h_all, rows, g_last, n_s, n_p, last)

    heads = lambda a, b, s: a.reshape(b, s, n_heads, HEAD_DIM)
    stack = lambda outs, i, f: jnp.stack([f(o[i]) for o in outs])
    hp_f = lambda a: heads(a, bp, sp)
    hs_f = lambda a: heads(a, bs, ss)
    return (hp.reshape(bp, sp, d), hs.reshape(bs, ss, d),
            stack(outs_p, 0, hp_f), stack(outs_p, 1, hp_f), stack(outs_p, 2, lambda a: a.reshape(bp, sp, n_heads)),
            stack(outs_p, 3, hp_f), stack(outs_p, 4, hp_f),
            stack(outs_s, 0, hs_f), stack(outs_s, 1, hs_f), stack(outs_s, 2, lambda a: a.reshape(bs, ss, n_heads)),
            stack(outs_s, 3, hs_f), stack(outs_s, 4, hs_f))
```

```python
import functools
import math

import jax
import jax.numpy as jnp
from jax import lax
from jax.experimental import pallas as pl
from jax.experimental.pallas import tpu as pltpu

F32 = jnp.float32
BF16 = jnp.bfloat16

HEAD_DIM = 128
TOP_K = 2
EPS = 1e-6
LANES = 128
MXU_DIM = 256
NEG_BIG = -1e30
VMEM_LIMIT = 56 * 1024 * 1024
LOG2E = math.log2(math.e)
SCALE = HEAD_DIM ** -0.5
SCALE_LOG2 = SCALE * LOG2E

ROW_TILE = 256
ATTN_TILE = 512
FOX_GROUP = 4
SB_GROUP = 4
LOOKAHEAD = 2
EXPERT_BLOCK = 256
DISPATCH_TILE = 512
COMBINE_TILE = 256
ISSUE_UNROLL = 8


def _params(*sem):
    return pltpu.CompilerParams(dimension_semantics=sem, vmem_limit_bytes=VMEM_LIMIT)


def _log_sigmoid(x):
    return jnp.minimum(x, 0.0) - jnp.log1p(jnp.exp(-jnp.abs(x)))


def _rms(x, g):
    return x * lax.rsqrt(jnp.mean(x * x, axis=-1, keepdims=True) + EPS) * g


def _split3(x):
    hi = x.astype(BF16).astype(F32)
    r1 = x - hi
    mid = r1.astype(BF16).astype(F32)
    lo = (r1 - mid).astype(BF16).astype(F32)
    return hi, mid, lo


def _running_sum(x):
    n = x.shape[0]
    r = lax.broadcasted_iota(jnp.int32, (n, n), 0)
    c = lax.broadcasted_iota(jnp.int32, (n, n), 1)
    tri = jnp.where(c <= r, 1.0, 0.0).astype(BF16)
    return sum(jnp.dot(tri, p.astype(BF16), preferred_element_type=F32) for p in _split3(x))


def _in_proj_kernel(x_ref, g_ref, wm_ref, wf_ref, bf_ref, lf_ref, *rest, width, n_heads, seq_tiles):
    hn = _rms(x_ref[...], g_ref[...]).astype(BF16)
    refs = iter(rest[1:-1] if seq_tiles else rest)
    for c in range(6):
        res = jnp.dot(hn, wm_ref[:, c * width:(c + 1) * width], preferred_element_type=F32)
        next(refs)[...] = (res if c % 3 else res * SCALE_LOG2).astype(BF16)
        if c % 3:
            kv_ref = next(refs)
            kv_ref[...] = res.reshape(kv_ref.shape)
    lf = _log_sigmoid(jnp.dot(hn, wf_ref[...], preferred_element_type=F32) + bf_ref[...])
    lf_ref[...] = lf[:, :n_heads]
    if seq_tiles:
        cf_ref, carry_ref = rest[0], rest[-1]

        @pl.when(pl.program_id(0) % seq_tiles == 0)
        def _():
            carry_ref[...] = jnp.zeros_like(carry_ref)

        cf = _running_sum(lf) + carry_ref[...]
        cf_ref[...] = cf[:, :n_heads]
        carry_ref[...] = cf[lf.shape[0] - 1:, :]


def _in_proj(x, g, wm, wf, bf, n_heads, seq_tiles=0):
    n, d = x.shape
    width = wm.shape[1] // 6
    tm = ROW_TILE
    row = lambda w: pl.BlockSpec((tm, w), lambda i: (i, 0))
    const = lambda a: pl.BlockSpec(a.shape, lambda i: (0, 0), pipeline_mode=pl.Buffered(1))
    per_head = pl.BlockSpec((tm, n_heads, HEAD_DIM), lambda i: (i, 0, 0))
    rows_out = jax.ShapeDtypeStruct((n, width), BF16)
    heads_out = jax.ShapeDtypeStruct((n, n_heads, HEAD_DIM), F32)
    narrow_out = jax.ShapeDtypeStruct((n, n_heads), F32)
    mixer_specs = [row(width)] + [row(width), per_head] * 2
    mixer_shapes = [rows_out] + [rows_out, heads_out] * 2
    n_narrow = 2 if seq_tiles else 1
    return pl.pallas_call(
        functools.partial(_in_proj_kernel, width=width, n_heads=n_heads, seq_tiles=seq_tiles),
        grid=(n // tm,),
        in_specs=[row(d), const(g), const(wm), const(wf), const(bf)],
        out_specs=[row(n_heads)] * n_narrow + mixer_specs * 2,
        out_shape=[narrow_out] * n_narrow + mixer_shapes * 2,
        scratch_shapes=[pltpu.VMEM((1, LANES), F32)] if seq_tiles else [],
        compiler_params=_params("arbitrary"),
        name="in_proj",
    )(x, g, wm, wf, bf)


def _cumsum_kernel(lf_ref, cf_ref, carry_ref):
    tt = lf_ref.shape[0]

    @pl.when(pl.program_id(1) == 0)
    def _():
        carry_ref[...] = jnp.zeros_like(carry_ref)

    cf = _running_sum(lf_ref[...]) + carry_ref[...]
    cf_ref[...] = cf
    carry_ref[...] = cf[tt - 1:tt, :]


def _cumsum(lf, tt):
    b, t, h = lf.shape
    return pl.pallas_call(
        _cumsum_kernel,
        grid=(b, t // tt),
        in_specs=[pl.BlockSpec((None, tt, h), lambda i, j: (i, j, 0))],
        out_specs=pl.BlockSpec((None, tt, h), lambda i, j: (i, j, 0)),
        out_shape=jax.ShapeDtypeStruct((b, t, h), F32),
        scratch_shapes=[pltpu.VMEM((1, h), F32)],
        compiler_params=_params("arbitrary", "arbitrary"),
        name="cumsum",
    )(lf)


def _bias_rows(c_rows, h, as_query):
    n = c_rows.shape[1]
    sub = lax.broadcasted_iota(jnp.int32, c_rows.shape, 0)
    c = jnp.sum(jnp.where(sub == h, c_rows, 0.0), axis=0, keepdims=True) * LOG2E
    hi, mid, lo = _split3(c)
    if as_query:
        parts, base, ones_lo = (hi, mid, lo), 0, 3
    else:
        parts, base, ones_lo = (-hi, -mid, -lo), 3, 0
    row = lax.broadcasted_iota(jnp.int32, (8, n), 0)
    blk = jnp.where((row >= ones_lo) & (row < ones_lo + 3), 1.0, 0.0)
    for i, p in enumerate(parts):
        blk = jnp.where(row == base + i, p, blk)
    return blk


def _bias_block(groups, n):
    zeros = lambda rows: jnp.zeros((rows, n), F32)
    rows = [zeros(8) if g is None else g for g in groups] + [zeros(LANES - 8 * len(groups))]
    return jnp.concatenate(rows, axis=0).T.astype(BF16)


def _fox_kernel(q_ref, k_ref, v_ref, c_ref, gt_ref, o_ref, kaug, vt, *, tq, tk, q_off, prep, group, wide, one_tile):
    hg = pl.program_id(1)
    qi = pl.program_id(2)
    sk = k_ref.shape[0]
    hd = lambda g: slice(g * HEAD_DIM, (g + 1) * HEAD_DIM)

    @pl.when(qi == 0)
    def _():
        for g in range(group):
            for r0 in range(0, sk, prep):
                rows = pl.ds(r0, prep)
                kaug[g, rows, :HEAD_DIM] = k_ref[rows, hd(g)]
                kaug[g, rows, HEAD_DIM:] = _bias_block(
                    [None] * g + [_bias_rows(c_ref[:, r0:r0 + prep], hg * group + g, False)], prep)
                vt[g, :, rows] = v_ref[rows, hd(g)].astype(F32).T.astype(BF16)

    q_lo = q_off if one_tile else pl.multiple_of(q_off + qi * tq, tq)
    c_q = c_ref[:, pl.ds(q_lo, tq)]
    q_bias = _bias_block([_bias_rows(c_q, hg * group + g, True) for g in range(group)], tq)
    qa = [jnp.concatenate([q_ref[:, hd(g)], q_bias], axis=1) for g in range(group)]
    kiota = lax.broadcasted_iota(jnp.int32, (tk, tq), 0)
    qpos = q_lo + lax.broadcasted_iota(jnp.int32, (tk, tq), 1)

    def chunk(j, carry, width, masked=False):
        k0 = pl.multiple_of(j * width, width)

        def logits(g):
            s = lax.dot_general(kaug[g, pl.ds(k0, width), :], qa[g], (((1,), (1,)), ((), ())),
                                preferred_element_type=F32)
            if masked:
                s = jnp.where(k0 + kiota <= qpos, s, NEG_BIG)
            return s

        def absorb(g, s):
            m, l, acc = carry[g]
            m_new = jnp.maximum(m, jnp.max(s, axis=0, keepdims=True))
            p = jnp.exp2(s - m_new)
            alpha = jnp.exp2(m - m_new)
            l = alpha * l + jnp.sum(p, axis=0, keepdims=True)
            acc = alpha * acc + jnp.dot(vt[g, :, pl.ds(k0, width)], p.astype(BF16), preferred_element_type=F32)
            return m_new, l, acc

        out, pending = [], {}
        for g in range(group + LOOKAHEAD):
            if g < group:
                pending[g] = logits(g)
            if g >= LOOKAHEAD:
                out.append(absorb(g - LOOKAHEAD, pending.pop(g - LOOKAHEAD)))
        return tuple(out)

    n_full = (q_lo + 1) // tk
    n_all = (q_lo + tq + tk - 1) // tk
    carry = tuple((jnp.full((1, tq), NEG_BIG, F32), jnp.zeros((1, tq), F32), jnp.zeros((HEAD_DIM, tq), F32))
                  for _ in range(group))
    n_narrow = n_full
    if wide:
        carry = lax.fori_loop(0, n_full // 2, lambda j, c: chunk(j, c, 2 * tk), carry)
        n_narrow = n_full % 2
    carry = lax.fori_loop(n_full - n_narrow, n_full, lambda j, c: chunk(j, c, tk), carry)
    carry = lax.fori_loop(n_full, n_all, lambda j, c: chunk(j, c, tk, True), carry)
    for g in range(group):
        m, l, acc = carry[g]
        o = acc / l
        y = o * lax.rsqrt(jnp.mean(o * o, axis=0, keepdims=True) + EPS) * gt_ref[hd(g), :]
        o_ref[:, hd(g)] = y.T.astype(o_ref.dtype)


def _sb_kernel(q_ref, k_ref, v_ref, g_ref, o_ref, *, tq, tk, q_off, group, wide):
    qi = pl.program_id(2)
    hd = lambda g: slice(g * HEAD_DIM, (g + 1) * HEAD_DIM)
    q = [q_ref[:, hd(g)] for g in range(group)]
    q_lo = q_off + qi * tq
    ts = MXU_DIM if tk % MXU_DIM == 0 else LANES
    r = lax.broadcasted_iota(jnp.int32, (ts, ts), 0)
    c = lax.broadcasted_iota(jnp.int32, (ts, ts), 1)
    neg_from = jnp.where(r >= c, -1.0, 0.0).astype(BF16)
    sign_bit = jnp.uint32(0x80000000)

    def chunk(j, carry, width, masked=False):
        k0 = pl.multiple_of(j * width, width)
        n_sub = width // ts
        first_row = (lambda b: b * ts) if (masked and tq == tk) else (lambda b: 0)

        def logits(g, b):
            r0 = first_row(b)
            z = lax.dot_general(q[g][r0:], k_ref[pl.ds(k0 + b * ts, ts), hd(g)], (((1,), (1,)), ((), ())),
                                preferred_element_type=F32)
            neg_abs = pltpu.bitcast(pltpu.bitcast(z, jnp.uint32) | sign_bit, F32)
            sp = jnp.maximum(z, 0.0) + jnp.log2(1.0 + jnp.exp2(neg_abs))
            valid = None
            if masked:
                qpos = q_lo + r0 + lax.broadcasted_iota(jnp.int32, z.shape, 0)
                valid = k0 + b * ts + lax.broadcasted_iota(jnp.int32, z.shape, 1) < qpos
                sp = jnp.where(valid, sp, 0.0)
            hi = sp.astype(BF16)
            lo = (sp - hi.astype(F32)).astype(BF16)
            return z, hi, lo, valid

        def weights(state, after, r0):
            z, hi, lo, valid = state
            sfx = (jnp.dot(hi, neg_from, preferred_element_type=F32)
                   + jnp.dot(lo, neg_from, preferred_element_type=F32)) + after[r0:]
            a = jnp.exp2(z + sfx)
            if masked:
                a = jnp.where(valid, a, 0.0)
            total = sfx[:, 0:1]
            return (jnp.concatenate([after[:r0], total], axis=0) if r0 else total), a.astype(BF16)

        items = [(g, b) for b in reversed(range(n_sub)) for g in range(group)]
        after = [carry[g][0] for g in range(group)]
        acc = [carry[g][1] for g in range(group)]
        staged, ready = {}, {}
        n = len(items)
        for step in range(n + 2 * LOOKAHEAD):
            if step < n:
                staged[step] = logits(*items[step])
            if 0 <= step - LOOKAHEAD < n:
                g, b = items[step - LOOKAHEAD]
                after[g], ready[step - LOOKAHEAD] = weights(staged.pop(step - LOOKAHEAD), after[g], first_row(b))
            if 0 <= step - 2 * LOOKAHEAD < n:
                g, b = items[step - 2 * LOOKAHEAD]
                r0 = first_row(b)
                rows = pl.ds(k0 + b * ts, ts)
                pv = jnp.dot(ready.pop(step - 2 * LOOKAHEAD), v_ref[rows, hd(g)], preferred_element_type=F32)
                acc[g] = jnp.concatenate([acc[g][:r0], acc[g][r0:] + pv], axis=0) if r0 else acc[g] + pv
        return tuple((after[g], acc[g]) for g in range(group))

    n_full = q_lo // tk
    n_all = (q_lo + tq - 1 + tk - 1) // tk
    carry = tuple((jnp.zeros((tq, 1), F32), jnp.zeros((tq, HEAD_DIM), F32)) for _ in range(group))
    carry = lax.fori_loop(0, n_all - n_full, lambda i, c: chunk(n_all - 1 - i, c, tk, True), carry)
    n_narrow = n_full
    if wide:
        n_narrow = n_full % 2
    carry = lax.fori_loop(0, n_narrow, lambda i, c: chunk(n_full - 1 - i, c, tk), carry)
    if wide:
        carry = lax.fori_loop(0, n_full // 2, lambda i, c: chunk(n_full // 2 - 1 - i, c, 2 * tk), carry)
    for g in range(group):
        o_ref[:, hd(g)] = _rms(carry[g][1], g_ref[:, hd(g)]).astype(o_ref.dtype)


def _attention(kind, q, k, v, g, cf, *, tq, tk, q_off):
    b, sq, width = q.shape
    sk = k.shape[1]
    n_heads = width // HEAD_DIM
    group = FOX_GROUP if kind == "fox" else SB_GROUP
    assert sq % tq == 0 and sk % tk == 0 and q_off % tq == 0 and n_heads % group == 0
    w = group * HEAD_DIM
    wide = sk % (2 * tk) == 0
    q_spec = pl.BlockSpec((None, tq, w), lambda i, h, j: (i, j, h))
    kv_spec = pl.BlockSpec((None, sk, w), lambda i, h, j: (i, 0, h))
    g_spec = pl.BlockSpec((1, w), lambda i, h, j: (0, h))
    if kind == "fox":
        body = functools.partial(_fox_kernel, tq=tq, tk=tk, q_off=q_off, prep=math.gcd(sk, 512), group=group,
                                 wide=wide, one_tile=sq == tq)
        in_specs = [q_spec, kv_spec, kv_spec, pl.BlockSpec((None, n_heads, sk), lambda i, h, j: (i, 0, 0)),
                    pl.BlockSpec((w, 1), lambda i, h, j: (h, 0))]
        args = (q, k, v, cf, g.reshape(width, 1))
        scratch = [pltpu.VMEM((group, sk, 2 * HEAD_DIM), BF16), pltpu.VMEM((group, HEAD_DIM, sk), BF16)]
    else:
        body = functools.partial(_sb_kernel, tq=tq, tk=tk, q_off=q_off, group=group, wide=wide)
        in_specs = [q_spec, kv_spec, kv_spec, g_spec]
        args = (q, k, v, g)
        scratch = []
    return pl.pallas_call(
        body,
        grid=(b, n_heads // group, sq // tq),
        in_specs=in_specs,
        out_specs=q_spec,
        out_shape=jax.ShapeDtypeStruct((b, sq, width), BF16),
        scratch_shapes=scratch,
        compiler_params=_params("arbitrary", "arbitrary", "arbitrary"),
        name=kind,
    )(*args)


def _with_cache_kernel(cache_ref, new_ref, o_ref):
    past, n_heads, _ = cache_ref.shape
    fresh = new_ref.shape[0]
    o_ref[:past, :] = cache_ref[...].reshape(past, n_heads * HEAD_DIM).astype(BF16)
    o_ref[past:past + fresh, :] = new_ref[...]
    o_ref[past + fresh:, :] = jnp.zeros((o_ref.shape[0] - past - fresh, o_ref.shape[1]), BF16)


def _with_cache(cache, layer, new, rows):
    _, b, past, n_heads, _ = cache.shape
    fresh, width = new.shape[1:]
    return pl.pallas_call(
        _with_cache_kernel,
        grid=(b,),
        in_specs=[pl.BlockSpec((None, None, past, n_heads, HEAD_DIM), lambda i: (layer, i, 0, 0, 0)),
                  pl.BlockSpec((None, fresh, width), lambda i: (i, 0, 0))],
        out_specs=pl.BlockSpec((None, rows, width), lambda i: (i, 0, 0)),
        out_shape=jax.ShapeDtypeStruct((b, rows, width), BF16),
        compiler_params=_params("arbitrary"),
        name="with_cache",
    )(cache, new)


def _merge_kernel(nfp_ref, nsp_ref, xp_ref, nfs_ref, nss_ref, xs_ref, wo_ref, g_ref, wr_ref, br_ref,
                  h_ref, hn_ref, lg_ref, *, prompt_blocks):
    def body(nf_ref, ns_ref, x_ref):
        width = nf_ref.shape[1]
        h = x_ref[...] + (jnp.dot(nf_ref[...], wo_ref[:width, :], preferred_element_type=F32)
                          + jnp.dot(ns_ref[...], wo_ref[width:, :], preferred_element_type=F32))
        h_ref[...] = h
        hn = _rms(h, g_ref[...])
        hn_ref[...] = hn
        lg_ref[...] = jnp.dot(hn.astype(BF16), wr_ref[...], preferred_element_type=F32) + br_ref[...]

    is_prompt = pl.program_id(0) < prompt_blocks
    pl.when(is_prompt)(lambda: body(nfp_ref, nsp_ref, xp_ref))
    pl.when(jnp.logical_not(is_prompt))(lambda: body(nfs_ref, nss_ref, xs_ref))


def _merge(prompt, sample, wo, g, wr, br):
    n_p, d = prompt[2].shape
    n_s = sample[2].shape[0]
    width = prompt[0].shape[1]
    tm = ROW_TILE
    pb, sb = n_p // tm, n_s // tm
    n_total = n_p + n_s
    p_row = lambda w: pl.BlockSpec((tm, w), lambda i: (jnp.minimum(i, pb - 1), 0))
    s_row = lambda w: pl.BlockSpec((tm, w), lambda i: (jnp.maximum(i - pb, 0), 0))
    out_row = lambda w: pl.BlockSpec((tm, w), lambda i: (i, 0))
    const = lambda a: pl.BlockSpec(a.shape, lambda i: (0, 0), pipeline_mode=pl.Buffered(1))
    return pl.pallas_call(
        functools.partial(_merge_kernel, prompt_blocks=pb),
        grid=(pb + sb,),
        in_specs=[p_row(width), p_row(width), p_row(d), s_row(width), s_row(width), s_row(d),
                  const(wo), const(g), const(wr), const(br)],
        out_specs=[out_row(d), out_row(d), out_row(LANES)],
        out_shape=[jax.ShapeDtypeStruct((n_total, d), F32), jax.ShapeDtypeStruct((n_total, d), F32),
                   jax.ShapeDtypeStruct((n_total, LANES), F32)],
        compiler_params=_params("arbitrary"),
        name="merge",
    )(*prompt, *sample, wo, g, wr, br)


def _route_kernel(lg_ref, idx_ref, gate_ref, cnt_ref, carry_ref, *, n_groups, per_group):
    tr = lg_ref.shape[0]

    @pl.when(pl.program_id(0) == 0)
    def _():
        carry_ref[...] = jnp.zeros_like(carry_ref)

    x = lg_ref[...]
    lane = lax.broadcasted_iota(jnp.int32, x.shape, 1).astype(F32)
    far = float(LANES)
    rmax = lambda a: jnp.max(a, axis=1, keepdims=True)
    first = lambda hit: jnp.min(jnp.where(hit, lane, far), axis=1, keepdims=True)

    gmask = lane < n_groups
    gl = jnp.where(gmask, x, -jnp.inf)
    gmax = rmax(gl)
    gsel = first(gl == gmax)
    p_g = 1.0 / jnp.sum(jnp.where(gmask, jnp.exp(x - gmax), 0.0), axis=1, keepdims=True)
    lo = n_groups + gsel * per_group
    emask = (lane >= lo) & (lane < lo + per_group)
    el = jnp.where(emask, x, -jnp.inf)
    m1 = rmax(el)
    i1 = first(el == m1)
    el2 = jnp.where(lane == i1, -jnp.inf, el)
    m2 = rmax(el2)
    i2 = first(el2 == m2)
    zsum = jnp.sum(jnp.where(emask, jnp.exp(x - m1), 0.0), axis=1, keepdims=True)
    p1 = 1.0 / zsum
    p2 = jnp.exp(m2 - m1) / zsum
    psum = p1 + p2
    gate1 = p_g * p1 / psum
    gate2 = p_g * p2 / psum
    e1 = i1 - n_groups
    e2 = i2 - n_groups
    hit1 = lane == e1
    hit2 = lane == e2
    onehot = jnp.where(hit1 | hit2, 1.0, 0.0)
    r = lax.broadcasted_iota(jnp.int32, (tr, tr), 0)
    c = lax.broadcasted_iota(jnp.int32, (tr, tr), 1)
    before = jnp.where(c < r, 1.0, 0.0).astype(BF16)
    seen = jnp.dot(before, onehot.astype(BF16), preferred_element_type=F32) + carry_ref[...]
    r1 = jnp.sum(jnp.where(hit1, seen, 0.0), axis=1, keepdims=True)
    r2 = jnp.sum(jnp.where(hit2, seen, 0.0), axis=1, keepdims=True)
    carry_ref[...] = carry_ref[...] + jnp.sum(onehot, axis=0, keepdims=True)
    cnt_ref[...] = carry_ref[...]

    col = lax.broadcasted_iota(jnp.int32, idx_ref.shape, 1)
    idx = jnp.where(col == 0, e1, jnp.where(col == 1, e2, jnp.where(col == 2, r1, r2)))
    idx_ref[...] = idx.astype(jnp.int32)
    gate_ref[...] = jnp.where(col == 0, gate1, jnp.where(col == 1, gate2, 0.0))


def _route(logits, n_groups, per_group):
    n = logits.shape[0]
    tr = ROW_TILE
    narrow = pl.BlockSpec((tr, 8), lambda i: (i, 0))
    return pl.pallas_call(
        functools.partial(_route_kernel, n_groups=n_groups, per_group=per_group),
        grid=(n // tr,),
        in_specs=[pl.BlockSpec((tr, LANES), lambda i: (i, 0))],
        out_specs=[narrow, narrow, pl.BlockSpec((1, LANES), lambda i: (0, 0))],
        out_shape=[jax.ShapeDtypeStruct((n, 8), jnp.int32), jax.ShapeDtypeStruct((n, 8), F32),
                   jax.ShapeDtypeStruct((1, LANES), F32)],
        scratch_shapes=[pltpu.VMEM((1, LANES), F32)],
        compiler_params=_params("arbitrary"),
        name="route",
    )(logits)


def _row_copy(src_ref, src_row, dst_ref, dst_row, sem):
    return pltpu.make_async_copy(src_ref.at[pl.ds(src_row, 1)], dst_ref.at[pl.ds(dst_row, 1)], sem)


def _wait_rows(src_ref, dst_ref, sem, n):
    pltpu.make_async_copy(src_ref.at[pl.ds(0, n)], dst_ref.at[pl.ds(0, n)], sem).wait()


def _dispatch_kernel(dest_ref, hn_ref, xs_ref, sem):
    td = hn_ref.shape[0]

    def start(r, _):
        for k in range(TOP_K):
            _row_copy(hn_ref, r, xs_ref, dest_ref[0, 0, TOP_K * r + k], sem).start()
        return 0

    for r in range(td):
        start(r, 0)
    for _ in range(TOP_K):
        _wait_rows(hn_ref, xs_ref, sem, td)


def _dispatch(dest, hn):
    n, d = hn.shape
    td = DISPATCH_TILE
    return pl.pallas_call(
        _dispatch_kernel,
        grid=(n // td,),
        in_specs=[pl.BlockSpec((1, 1, TOP_K * td), lambda i: (i, 0, 0), memory_space=pltpu.SMEM),
                  pl.BlockSpec((td, d), lambda i: (i, 0))],
        out_specs=pl.BlockSpec(memory_space=pl.ANY),
        out_shape=jax.ShapeDtypeStruct((n * TOP_K, d), F32),
        scratch_shapes=[pltpu.SemaphoreType.DMA(())],
        compiler_params=_params("arbitrary"),
        name="dispatch",
    )(dest.reshape(n // td, 1, TOP_K * td), hn)


def _expert_kernel(blk_ref, exp_ref, lo_ref, hi_ref, first_ref, xs_ref, w1_ref, w3_ref, w2_ref, o_ref):
    w = pl.program_id(0)
    lo, hi = lo_ref[w], hi_ref[w]

    def swiglu():
        rows = lax.broadcasted_iota(jnp.int32, xs_ref.shape, 0)
        x = jnp.where((rows >= lo) & (rows < hi), xs_ref[...], 0.0).astype(BF16)
        a = jnp.dot(x, w1_ref[...], preferred_element_type=F32)
        b = jnp.dot(x, w3_ref[...], preferred_element_type=F32)
        act = (a * jax.nn.sigmoid(a) * b).astype(BF16)
        return jnp.dot(act, w2_ref[...], preferred_element_type=F32)

    @pl.when((hi > lo) & (first_ref[w] == 1))
    def _():
        o_ref[...] = swiglu()

    @pl.when((hi > lo) & (first_ref[w] == 0))
    def _():
        o_ref[...] += swiglu()


def _experts(plan, xs, w1, w3, w2):
    m, d = xs.shape
    de = w1.shape[2]
    blk = EXPERT_BLOCK
    n_items = plan[0].shape[0]
    row_blk = lambda i, rb, ex, lo, hi, fs: (rb[i], 0)
    weight = lambda i, rb, ex, lo, hi, fs: (ex[i], 0, 0)
    return pl.pallas_call(
        _expert_kernel,
        grid_spec=pltpu.PrefetchScalarGridSpec(
            num_scalar_prefetch=5,
            grid=(n_items,),
            in_specs=[pl.BlockSpec((blk, d), row_blk),
                      pl.BlockSpec((None, d, de), weight),
                      pl.BlockSpec((None, d, de), weight),
                      pl.BlockSpec((None, de, d), weight)],
            out_specs=pl.BlockSpec((blk, d), row_blk)),
        out_shape=jax.ShapeDtypeStruct((m, d), F32),
        compiler_params=_params("arbitrary"),
        name="experts",
    )(*plan, xs, w1, w3, w2)


def _combine_kernel(dest_ref, next_ref, gate_ref, h_ref, rows_ref, g_ref, y_ref, buf, sem, *, final_norm):
    i = pl.program_id(0)
    last = pl.num_programs(0) - 1
    tc = h_ref.shape[0]
    slot = i % 2

    def gather(idx_ref, into, unrolled):
        def start(r, _):
            for k in range(TOP_K):
                _row_copy(rows_ref, idx_ref[0, 0, TOP_K * r + k], buf.at[into, k], r, sem.at[into]).start()
            return 0

        if unrolled:
            for r in range(tc):
                start(r, 0)
        else:
            lax.fori_loop(0, tc, start, 0, unroll=ISSUE_UNROLL)

    def drain(into):
        for k in range(TOP_K):
            _wait_rows(rows_ref, buf.at[into, k], sem.at[into], tc)

    pl.when(i == 0)(lambda: gather(dest_ref, 0, False))
    gather(next_ref, 1 - slot, True)
    drain(slot)
    gates = gate_ref[...]
    y = gates[:, 0:1] * buf[slot, 0] + gates[:, 1:2] * buf[slot, 1]
    h = h_ref[...] + y
    y_ref[...] = _rms(h, g_ref[...]) if final_norm else h
    pl.when(i == last)(lambda: drain(1 - slot))


def _combine(dest, gates, h, rows, g, n, row_off, final_norm):
    d = h.shape[1]
    tc = COMBINE_TILE
    off = row_off // tc
    last = off + n // tc - 1
    idx_spec = lambda f: pl.BlockSpec((1, 1, TOP_K * tc), lambda i: (f(i), 0, 0), memory_space=pltpu.SMEM)
    dest3 = dest.reshape(-1, 1, TOP_K * tc)
    return pl.pallas_call(
        functools.partial(_combine_kernel, final_norm=final_norm),
        grid=(n // tc,),
        in_specs=[idx_spec(lambda i: i + off), idx_spec(lambda i: jnp.minimum(i + off + 1, last)),
                  pl.BlockSpec((tc, 8), lambda i: (i + off, 0)),
                  pl.BlockSpec((tc, d), lambda i: (i + off, 0)),
                  pl.BlockSpec(memory_space=pl.ANY),
                  pl.BlockSpec((1, d), lambda i: (0, 0))],
        out_specs=pl.BlockSpec((tc, d), lambda i: (i, 0)),
        out_shape=jax.ShapeDtypeStruct((n, d), F32),
        scratch_shapes=[pltpu.VMEM((2, TOP_K, tc, d), F32), pltpu.SemaphoreType.DMA((2,))],
        compiler_params=_params("arbitrary"),
        name="combine",
    )(dest3, dest3, gates, h, rows, g)


def _round_up(a, m):
    return -(-a // m) * m


def _moe_plan(idx, counts, n_experts):
    blk = EXPERT_BLOCK
    n = idx.shape[0]
    i32 = lambda a: a.astype(jnp.int32)
    pick = lambda table, i: jnp.sum(jnp.where(i[..., None] == jnp.arange(table.shape[0]), table, 0), axis=-1)
    rank_of = lambda ends, x: jnp.sum(i32(ends[None, :] <= x[:, None]), axis=1)
    counts = i32(counts[0, :n_experts])
    end = jnp.cumsum(counts)
    start = end - counts
    dest = i32(pick(start, idx[:, :TOP_K]) + idx[:, TOP_K:2 * TOP_K])
    n_blocks = n * TOP_K // blk
    block_lo = jnp.arange(n_blocks, dtype=jnp.int32) * blk
    e_first = jnp.minimum(rank_of(end, block_lo), n_experts - 1)
    e_last = jnp.minimum(rank_of(end, block_lo + blk - 1), n_experts - 1)
    item_end = jnp.cumsum(e_last - e_first + 1)
    item_start = item_end - (e_last - e_first + 1)
    n_items = n_blocks + n_experts - 1
    w = jnp.arange(n_items, dtype=jnp.int32)
    live = w < item_end[-1]
    rb = jnp.minimum(rank_of(item_end, w), n_blocks - 1)
    ex = jnp.where(live, pick(e_first, rb) + (w - pick(item_start, rb)), e_last[-1])
    lo = jnp.where(live, jnp.clip(pick(start, ex) - rb * blk, 0, blk), 0)
    hi = jnp.where(live, jnp.clip(pick(end, ex) - rb * blk, 0, blk), 0)
    first = i32(w == pick(item_start, rb))
    return dest, (i32(rb), i32(ex), i32(lo), i32(hi), first)


def _split_w_in(w_in, width, n_heads):
    wm = jnp.concatenate([w_in[:, :3 * width], w_in[:, 3 * width + n_heads:]], axis=1).astype(BF16)
    wf = jnp.pad(w_in[:, 3 * width:3 * width + n_heads], ((0, 0), (0, LANES - n_heads))).astype(BF16)
    return wm, wf


def kernel(x_prompt, x_sample, cache_fox_k, cache_fox_v, cache_fox_logf, cache_sb_k, cache_sb_v,
           w_in, b_f, g_attn, g_out_fox, g_out_sb, w_out, g_ffn, w_group, b_group, w_router, b_router,
           w1, w3, w2, g_final):
    depth = w_in.shape[0]
    bp, sp, d = x_prompt.shape
    bs, ss, _ = x_sample.shape
    past = cache_fox_logf.shape[2]
    n_heads = cache_fox_logf.shape[3]
    width = n_heads * HEAD_DIM
    n_groups = w_group.shape[2]
    n_experts = w_router.shape[2]
    per_group = n_experts // n_groups
    n_p, n_s = bp * sp, bs * ss
    sk_s = _round_up(past + ss, LANES)
    pad_s = sk_s - past - ss
    row2 = lambda a: a.reshape(1, -1)

    hp = x_prompt.reshape(n_p, d)
    hs = x_sample.reshape(n_s, d)
    outs_p, outs_s = [], []
    for l in range(depth):
        wm, wf = _split_w_in(w_in[l], width, n_heads)
        bfp = jnp.pad(b_f[l], (0, LANES - n_heads)).reshape(1, LANES)
        wo = w_out[l].astype(BF16)
        wr = jnp.pad(jnp.concatenate([w_group[l], w_router[l]], axis=1),
                     ((0, 0), (0, LANES - n_groups - n_experts))).astype(BF16)
        br = jnp.pad(jnp.concatenate([b_group[l], b_router[l]]), (0, LANES - n_groups - n_experts)).reshape(1, LANES)
        g_a, g_f = row2(g_attn[l]), row2(g_ffn[l])
        g_of, g_os = row2(g_out_fox[l]), row2(g_out_sb[l])

        lf, cf, qf, kfb, kf, vfb, vf, qs, ksb, ks, vsb, vs = _in_proj(hp, g_a, wm, wf, bfp, n_heads, sp // ROW_TILE)
        b3 = lambda a: a.reshape(bp, sp, -1)
        tq = min(sp, ATTN_TILE)
        nf = _attention("fox", b3(qf), b3(kfb), b3(vfb), g_of, jnp.swapaxes(b3(cf), 1, 2), tq=tq, tk=tq, q_off=0)
        ns = _attention("sb", b3(qs), b3(ksb), b3(vsb), g_os, None, tq=tq, tk=tq, q_off=0)
        outs_p.append((kf, vf, lf, ks, vs))

        lf2, qf2, kfb2, kf2, vfb2, vf2, qs2, ksb2, ks2, vsb2, vs2 = _in_proj(hs, g_a, wm, wf, bfp, n_heads)
        c3 = lambda a: a.reshape(bs, ss, -1)
        keys = lambda cache, new: _with_cache(cache, l, c3(new), sk_s)
        cf_all = _cumsum(jnp.concatenate([cache_fox_logf[l].astype(F32), c3(lf2)], axis=1), past + ss)
        cf_all = jnp.swapaxes(jnp.pad(cf_all, ((0, 0), (0, pad_s), (0, 0))), 1, 2)
        nf2 = _attention("fox", c3(qf2), keys(cache_fox_k, kfb2), keys(cache_fox_v, vfb2), g_of, cf_all,
                         tq=ss, tk=sk_s, q_off=past)
        ns2 = _attention("sb", c3(qs2), keys(cache_sb_k, ksb2), keys(cache_sb_v, vsb2), g_os, None,
                         tq=ss, tk=sk_s, q_off=past)
        outs_s.append((kf2, vf2, lf2, ks2, vs2))

        h_all, hn_all, logits = _merge((nf.reshape(n_p, width), ns.reshape(n_p, width), hp),
                                       (nf2.reshape(n_s, width), ns2.reshape(n_s, width), hs), wo, g_f, wr, br)
        idx, gates, counts = _route(logits, n_groups, per_group)
        dest, plan = _moe_plan(idx, counts, n_experts)
        xs = _dispatch(dest, hn_all)
        rows = _experts(plan, xs, w1[l].astype(BF16), w3[l].astype(BF16), w2[l].astype(BF16))
        last = l == depth - 1
        g_last = row2(g_final)
        hp = _combine(dest, gates, h_all, rows, g_last, n_p, 0, last)
        hs = _combine(dest, gates, h_all, rows, g_last, n_s, n_p, last)

    heads = lambda a, b, s: a.reshape(b, s, n_heads, HEAD_DIM)
    stack = lambda outs, i, f: jnp.stack([f(o[i]) for o in outs])
    hp_f = lambda a: heads(a, bp, sp)
    hs_f = lambda a: heads(a, bs, ss)
    return (hp.reshape(bp, sp, d), hs.reshape(bs, ss, d),
            stack(outs_p, 0, hp_f), stack(outs_p, 1, hp_f), stack(outs_p, 2, lambda a: a.reshape(bp, sp, n_heads)),
            stack(outs_p, 3, hp_f), stack(outs_p, 4, hp_f),
            stack(outs_s, 0, hs_f), stack(outs_s, 1, hs_f), stack(outs_s, 2, lambda a: a.reshape(bs, ss, n_heads)),
            stack(outs_s, 3, hs_f), stack(outs_s, 4, hs_f))
```

```python
import functools
import math

import jax
import jax.numpy as jnp
from jax import lax
from jax.experimental import pallas as pl
from jax.experimental.pallas import tpu as pltpu

F32 = jnp.float32
BF16 = jnp.bfloat16

HEAD_DIM = 128
TOP_K = 2
EPS = 1e-6
LANES = 128
MXU_DIM = 256
NEG_BIG = -1e30
VMEM_LIMIT = 56 * 1024 * 1024
LOG2E = math.log2(math.e)
SCALE = HEAD_DIM ** -0.5
SCALE_LOG2 = SCALE * LOG2E

ROW_TILE = 256
ATTN_TILE = 512
FOX_GROUP = 4
SB_GROUP = 4
LOOKAHEAD = 2
EXPERT_BLOCK = 256
DISPATCH_TILE = 512
COMBINE_TILE = 512
ISSUE_UNROLL = 8


def _params(*sem):
    return pltpu.CompilerParams(dimension_semantics=sem, vmem_limit_bytes=VMEM_LIMIT)


def _log_sigmoid(x):
    return jnp.minimum(x, 0.0) - jnp.log1p(jnp.exp(-jnp.abs(x)))


def _rms(x, g):
    return x * lax.rsqrt(jnp.mean(x * x, axis=-1, keepdims=True) + EPS) * g


def _split3(x):
    hi = x.astype(BF16).astype(F32)
    r1 = x - hi
    mid = r1.astype(BF16).astype(F32)
    lo = (r1 - mid).astype(BF16).astype(F32)
    return hi, mid, lo


def _running_sum(x):
    n = x.shape[0]
    r = lax.broadcasted_iota(jnp.int32, (n, n), 0)
    c = lax.broadcasted_iota(jnp.int32, (n, n), 1)
    tri = jnp.where(c <= r, 1.0, 0.0).astype(BF16)
    return sum(jnp.dot(tri, p.astype(BF16), preferred_element_type=F32) for p in _split3(x))


def _in_proj_kernel(x_ref, g_ref, wm_ref, wf_ref, bf_ref, lf_ref, *rest, width, n_heads, seq_tiles):
    hn = _rms(x_ref[...], g_ref[...]).astype(BF16)
    refs = iter(rest[1:-1] if seq_tiles else rest)
    for c in range(6):
        res = jnp.dot(hn, wm_ref[:, c * width:(c + 1) * width], preferred_element_type=F32)
        next(refs)[...] = (res if c % 3 else res * SCALE_LOG2).astype(BF16)
        if c % 3:
            kv_ref = next(refs)
            kv_ref[...] = res.reshape(kv_ref.shape)
    lf = _log_sigmoid(jnp.dot(hn, wf_ref[...], preferred_element_type=F32) + bf_ref[...])
    lf_ref[...] = lf[:, :n_heads]
    if seq_tiles:
        cf_ref, carry_ref = rest[0], rest[-1]

        @pl.when(pl.program_id(0) % seq_tiles == 0)
        def _():
            carry_ref[...] = jnp.zeros_like(carry_ref)

        cf = _running_sum(lf) + carry_ref[...]
        cf_ref[...] = cf[:, :n_heads]
        carry_ref[...] = cf[lf.shape[0] - 1:, :]


def _in_proj(x, g, wm, wf, bf, n_heads, seq_tiles=0):
    n, d = x.shape
    width = wm.shape[1] // 6
    tm = ROW_TILE
    row = lambda w: pl.BlockSpec((tm, w), lambda i: (i, 0))
    const = lambda a: pl.BlockSpec(a.shape, lambda i: (0, 0), pipeline_mode=pl.Buffered(1))
    per_head = pl.BlockSpec((tm, n_heads, HEAD_DIM), lambda i: (i, 0, 0))
    rows_out = jax.ShapeDtypeStruct((n, width), BF16)
    heads_out = jax.ShapeDtypeStruct((n, n_heads, HEAD_DIM), F32)
    narrow_out = jax.ShapeDtypeStruct((n, n_heads), F32)
    mixer_specs = [row(width)] + [row(width), per_head] * 2
    mixer_shapes = [rows_out] + [rows_out, heads_out] * 2
    n_narrow = 2 if seq_tiles else 1
    return pl.pallas_call(
        functools.partial(_in_proj_kernel, width=width, n_heads=n_heads, seq_tiles=seq_tiles),
        grid=(n // tm,),
        in_specs=[row(d), const(g), const(wm), const(wf), const(bf)],
        out_specs=[row(n_heads)] * n_narrow + mixer_specs * 2,
        out_shape=[narrow_out] * n_narrow + mixer_shapes * 2,
        scratch_shapes=[pltpu.VMEM((1, LANES), F32)] if seq_tiles else [],
        compiler_params=_params("arbitrary"),
        name="in_proj",
    )(x, g, wm, wf, bf)


def _cumsum_kernel(lf_ref, cf_ref, carry_ref):
    tt = lf_ref.shape[0]

    @pl.when(pl.program_id(1) == 0)
    def _():
        carry_ref[...] = jnp.zeros_like(carry_ref)

    cf = _running_sum(lf_ref[...]) + carry_ref[...]
    cf_ref[...] = cf
    carry_ref[...] = cf[tt - 1:tt, :]


def _cumsum(lf, tt):
    b, t, h = lf.shape
    return pl.pallas_call(
        _cumsum_kernel,
        grid=(b, t // tt),
        in_specs=[pl.BlockSpec((None, tt, h), lambda i, j: (i, j, 0))],
        out_specs=pl.BlockSpec((None, tt, h), lambda i, j: (i, j, 0)),
        out_shape=jax.ShapeDtypeStruct((b, t, h), F32),
        scratch_shapes=[pltpu.VMEM((1, h), F32)],
        compiler_params=_params("arbitrary", "arbitrary"),
        name="cumsum",
    )(lf)


def _bias_rows(c_rows, h, as_query):
    n = c_rows.shape[1]
    sub = lax.broadcasted_iota(jnp.int32, c_rows.shape, 0)
    c = jnp.sum(jnp.where(sub == h, c_rows, 0.0), axis=0, keepdims=True) * LOG2E
    hi, mid, lo = _split3(c)
    if as_query:
        parts, base, ones_lo = (hi, mid, lo), 0, 3
    else:
        parts, base, ones_lo = (-hi, -mid, -lo), 3, 0
    row = lax.broadcasted_iota(jnp.int32, (8, n), 0)
    blk = jnp.where((row >= ones_lo) & (row < ones_lo + 3), 1.0, 0.0)
    for i, p in enumerate(parts):
        blk = jnp.where(row == base + i, p, blk)
    return blk


def _bias_block(groups, n):
    zeros = lambda rows: jnp.zeros((rows, n), F32)
    rows = [zeros(8) if g is None else g for g in groups] + [zeros(LANES - 8 * len(groups))]
    return jnp.concatenate(rows, axis=0).T.astype(BF16)


def _fox_kernel(q_ref, k_ref, v_ref, c_ref, gt_ref, o_ref, kaug, vt, *, tq, tk, q_off, prep, group, wide, one_tile):
    hg = pl.program_id(1)
    qi = pl.program_id(2)
    sk = k_ref.shape[0]
    hd = lambda g: slice(g * HEAD_DIM, (g + 1) * HEAD_DIM)

    @pl.when(qi == 0)
    def _():
        for g in range(group):
            for r0 in range(0, sk, prep):
                rows = pl.ds(r0, prep)
                kaug[g, rows, :HEAD_DIM] = k_ref[rows, hd(g)]
                kaug[g, rows, HEAD_DIM:] = _bias_block(
                    [None] * g + [_bias_rows(c_ref[:, r0:r0 + prep], hg * group + g, False)], prep)
                vt[g, :, rows] = v_ref[rows, hd(g)].astype(F32).T.astype(BF16)

    q_lo = q_off if one_tile else pl.multiple_of(q_off + qi * tq, tq)
    c_q = c_ref[:, pl.ds(q_lo, tq)]
    q_bias = _bias_block([_bias_rows(c_q, hg * group + g, True) for g in range(group)], tq)
    qa = [jnp.concatenate([q_ref[:, hd(g)], q_bias], axis=1) for g in range(group)]
    kiota = lax.broadcasted_iota(jnp.int32, (tk, tq), 0)
    qpos = q_lo + lax.broadcasted_iota(jnp.int32, (tk, tq), 1)

    def chunk(j, carry, width, masked=False):
        k0 = pl.multiple_of(j * width, width)

        def logits(g):
            s = lax.dot_general(kaug[g, pl.ds(k0, width), :], qa[g], (((1,), (1,)), ((), ())),
                                preferred_element_type=F32)
            if masked:
                s = jnp.where(k0 + kiota <= qpos, s, NEG_BIG)
            return s

        def absorb(g, s):
            m, l, acc = carry[g]
            m_new = jnp.maximum(m, jnp.max(s, axis=0, keepdims=True))
            p = jnp.exp2(s - m_new)
            alpha = jnp.exp2(m - m_new)
            l = alpha * l + jnp.sum(p, axis=0, keepdims=True)
            acc = alpha * acc + jnp.dot(vt[g, :, pl.ds(k0, width)], p.astype(BF16), preferred_element_type=F32)
            return m_new, l, acc

        out, pending = [], {}
        for g in range(group + LOOKAHEAD):
            if g < group:
                pending[g] = logits(g)
            if g >= LOOKAHEAD:
                out.append(absorb(g - LOOKAHEAD, pending.pop(g - LOOKAHEAD)))
        return tuple(out)

    n_full = (q_lo + 1) // tk
    n_all = (q_lo + tq + tk - 1) // tk
    carry = tuple((jnp.full((1, tq), NEG_BIG, F32), jnp.zeros((1, tq), F32), jnp.zeros((HEAD_DIM, tq), F32))
                  for _ in range(group))
    n_narrow = n_full
    if wide:
        carry = lax.fori_loop(0, n_full // 2, lambda j, c: chunk(j, c, 2 * tk), carry)
        n_narrow = n_full % 2
    carry = lax.fori_loop(n_full - n_narrow, n_full, lambda j, c: chunk(j, c, tk), carry)
    carry = lax.fori_loop(n_full, n_all, lambda j, c: chunk(j, c, tk, True), carry)
    for g in range(group):
        m, l, acc = carry[g]
        o = acc / l
        y = o * lax.rsqrt(jnp.mean(o * o, axis=0, keepdims=True) + EPS) * gt_ref[hd(g), :]
        o_ref[:, hd(g)] = y.T.astype(o_ref.dtype)


def _sb_kernel(q_ref, k_ref, v_ref, g_ref, o_ref, *, tq, tk, q_off, group, wide):
    qi = pl.program_id(2)
    hd = lambda g: slice(g * HEAD_DIM, (g + 1) * HEAD_DIM)
    q = [q_ref[:, hd(g)] for g in range(group)]
    q_lo = q_off + qi * tq
    ts = MXU_DIM if tk % MXU_DIM == 0 else LANES
    r = lax.broadcasted_iota(jnp.int32, (ts, ts), 0)
    c = lax.broadcasted_iota(jnp.int32, (ts, ts), 1)
    neg_from = jnp.where(r >= c, -1.0, 0.0).astype(BF16)
    sign_bit = jnp.uint32(0x80000000)

    def chunk(j, carry, width, masked=False):
        k0 = pl.multiple_of(j * width, width)
        n_sub = width // ts
        first_row = (lambda b: b * ts) if (masked and tq == tk) else (lambda b: 0)

        def logits(g, b):
            r0 = first_row(b)
            z = lax.dot_general(q[g][r0:], k_ref[pl.ds(k0 + b * ts, ts), hd(g)], (((1,), (1,)), ((), ())),
                                preferred_element_type=F32)
            neg_abs = pltpu.bitcast(pltpu.bitcast(z, jnp.uint32) | sign_bit, F32)
            sp = jnp.maximum(z, 0.0) + jnp.log2(1.0 + jnp.exp2(neg_abs))
            valid = None
            if masked:
                qpos = q_lo + r0 + lax.broadcasted_iota(jnp.int32, z.shape, 0)
                valid = k0 + b * ts + lax.broadcasted_iota(jnp.int32, z.shape, 1) < qpos
                sp = jnp.where(valid, sp, 0.0)
            hi = sp.astype(BF16)
            lo = (sp - hi.astype(F32)).astype(BF16)
            return z, hi, lo, valid

        def weights(state, after, r0):
            z, hi, lo, valid = state
            sfx = (jnp.dot(hi, neg_from, preferred_element_type=F32)
                   + jnp.dot(lo, neg_from, preferred_element_type=F32)) + after[r0:]
            a = jnp.exp2(z + sfx)
            if masked:
                a = jnp.where(valid, a, 0.0)
            total = sfx[:, 0:1]
            return (jnp.concatenate([after[:r0], total], axis=0) if r0 else total), a.astype(BF16)

        items = [(g, b) for b in reversed(range(n_sub)) for g in range(group)]
        after = [carry[g][0] for g in range(group)]
        acc = [carry[g][1] for g in range(group)]
        staged, ready = {}, {}
        n = len(items)
        for step in range(n + 2 * LOOKAHEAD):
            if step < n:
                staged[step] = logits(*items[step])
            if 0 <= step - LOOKAHEAD < n:
                g, b = items[step - LOOKAHEAD]
                after[g], ready[step - LOOKAHEAD] = weights(staged.pop(step - LOOKAHEAD), after[g], first_row(b))
            if 0 <= step - 2 * LOOKAHEAD < n:
                g, b = items[step - 2 * LOOKAHEAD]
                r0 = first_row(b)
                rows = pl.ds(k0 + b * ts, ts)
                pv = jnp.dot(ready.pop(step - 2 * LOOKAHEAD), v_ref[rows, hd(g)], preferred_element_type=F32)
                acc[g] = jnp.concatenate([acc[g][:r0], acc[g][r0:] + pv], axis=0) if r0 else acc[g] + pv
        return tuple((after[g], acc[g]) for g in range(group))

    n_full = q_lo // tk
    n_all = (q_lo + tq - 1 + tk - 1) // tk
    carry = tuple((jnp.zeros((tq, 1), F32), jnp.zeros((tq, HEAD_DIM), F32)) for _ in range(group))
    carry = lax.fori_loop(0, n_all - n_full, lambda i, c: chunk(n_all - 1 - i, c, tk, True), carry)
    n_narrow = n_full
    if wide:
        n_narrow = n_full % 2
    carry = lax.fori_loop(0, n_narrow, lambda i, c: chunk(n_full - 1 - i, c, tk), carry)
    if wide:
        carry = lax.fori_loop(0, n_full // 2, lambda i, c: chunk(n_full // 2 - 1 - i, c, 2 * tk), carry)
    for g in range(group):
        o_ref[:, hd(g)] = _rms(carry[g][1], g_ref[:, hd(g)]).astype(o_ref.dtype)


def _attention(kind, q, k, v, g, cf, *, tq, tk, q_off):
    b, sq, width = q.shape
    sk = k.shape[1]
    n_heads = width // HEAD_DIM
    group = FOX_GROUP if kind == "fox" else SB_GROUP
    assert sq % tq == 0 and sk % tk == 0 and q_off % tq == 0 and n_heads % group == 0
    w = group * HEAD_DIM
    wide = sk % (2 * tk) == 0
    q_spec = pl.BlockSpec((None, tq, w), lambda i, h, j: (i, j, h))
    kv_spec = pl.BlockSpec((None, sk, w), lambda i, h, j: (i, 0, h))
    g_spec = pl.BlockSpec((1, w), lambda i, h, j: (0, h))
    if kind == "fox":
        body = functools.partial(_fox_kernel, tq=tq, tk=tk, q_off=q_off, prep=math.gcd(sk, 512), group=group,
                                 wide=wide, one_tile=sq == tq)
        in_specs = [q_spec, kv_spec, kv_spec, pl.BlockSpec((None, n_heads, sk), lambda i, h, j: (i, 0, 0)),
                    pl.BlockSpec((w, 1), lambda i, h, j: (h, 0))]
        args = (q, k, v, cf, g.reshape(width, 1))
        scratch = [pltpu.VMEM((group, sk, 2 * HEAD_DIM), BF16), pltpu.VMEM((group, HEAD_DIM, sk), BF16)]
    else:
        body = functools.partial(_sb_kernel, tq=tq, tk=tk, q_off=q_off, group=group, wide=wide)
        in_specs = [q_spec, kv_spec, kv_spec, g_spec]
        args = (q, k, v, g)
        scratch = []
    return pl.pallas_call(
        body,
        grid=(b, n_heads // group, sq // tq),
        in_specs=in_specs,
        out_specs=q_spec,
        out_shape=jax.ShapeDtypeStruct((b, sq, width), BF16),
        scratch_shapes=scratch,
        compiler_params=_params("arbitrary", "arbitrary", "arbitrary"),
        name=kind,
    )(*args)


def _with_cache_kernel(cache_ref, new_ref, o_ref):
    past, n_heads, _ = cache_ref.shape
    fresh = new_ref.shape[0]
    o_ref[:past, :] = cache_ref[...].reshape(past, n_heads * HEAD_DIM).astype(BF16)
    o_ref[past:past + fresh, :] = new_ref[...]
    o_ref[past + fresh:, :] = jnp.zeros((o_ref.shape[0] - past - fresh, o_ref.shape[1]), BF16)


def _with_cache(cache, layer, new, rows):
    _, b, past, n_heads, _ = cache.shape
    fresh, width = new.shape[1:]
    return pl.pallas_call(
        _with_cache_kernel,
        grid=(b,),
        in_specs=[pl.BlockSpec((None, None, past, n_heads, HEAD_DIM), lambda i: (layer, i, 0, 0, 0)),
                  pl.BlockSpec((None, fresh, width), lambda i: (i, 0, 0))],
        out_specs=pl.BlockSpec((None, rows, width), lambda i: (i, 0, 0)),
        out_shape=jax.ShapeDtypeStruct((b, rows, width), BF16),
        compiler_params=_params("arbitrary"),
        name="with_cache",
    )(cache, new)


def _merge_kernel(nfp_ref, nsp_ref, xp_ref, nfs_ref, nss_ref, xs_ref, wo_ref, g_ref, wr_ref, br_ref,
                  h_ref, hn_ref, lg_ref, *, prompt_blocks):
    def body(nf_ref, ns_ref, x_ref):
        width = nf_ref.shape[1]
        h = x_ref[...] + (jnp.dot(nf_ref[...], wo_ref[:width, :], preferred_element_type=F32)
                          + jnp.dot(ns_ref[...], wo_ref[width:, :], preferred_element_type=F32))
        h_ref[...] = h
        hn = _rms(h, g_ref[...])
        hn_ref[...] = hn
        lg_ref[...] = jnp.dot(hn.astype(BF16), wr_ref[...], preferred_element_type=F32) + br_ref[...]

    is_prompt = pl.program_id(0) < prompt_blocks
    pl.when(is_prompt)(lambda: body(nfp_ref, nsp_ref, xp_ref))
    pl.when(jnp.logical_not(is_prompt))(lambda: body(nfs_ref, nss_ref, xs_ref))


def _merge(prompt, sample, wo, g, wr, br):
    n_p, d = prompt[2].shape
    n_s = sample[2].shape[0]
    width = prompt[0].shape[1]
    tm = ROW_TILE
    pb, sb = n_p // tm, n_s // tm
    n_total = n_p + n_s
    p_row = lambda w: pl.BlockSpec((tm, w), lambda i: (jnp.minimum(i, pb - 1), 0))
    s_row = lambda w: pl.BlockSpec((tm, w), lambda i: (jnp.maximum(i - pb, 0), 0))
    out_row = lambda w: pl.BlockSpec((tm, w), lambda i: (i, 0))
    const = lambda a: pl.BlockSpec(a.shape, lambda i: (0, 0), pipeline_mode=pl.Buffered(1))
    return pl.pallas_call(
        functools.partial(_merge_kernel, prompt_blocks=pb),
        grid=(pb + sb,),
        in_specs=[p_row(width), p_row(width), p_row(d), s_row(width), s_row(width), s_row(d),
                  const(wo), const(g), const(wr), const(br)],
        out_specs=[out_row(d), out_row(d), out_row(LANES)],
        out_shape=[jax.ShapeDtypeStruct((n_total, d), F32), jax.ShapeDtypeStruct((n_total, d), F32),
                   jax.ShapeDtypeStruct((n_total, LANES), F32)],
        compiler_params=_params("arbitrary"),
        name="merge",
    )(*prompt, *sample, wo, g, wr, br)


def _route_kernel(lg_ref, idx_ref, gate_ref, cnt_ref, carry_ref, *, n_groups, per_group):
    tr = lg_ref.shape[0]

    @pl.when(pl.program_id(0) == 0)
    def _():
        carry_ref[...] = jnp.zeros_like(carry_ref)

    x = lg_ref[...]
    lane = lax.broadcasted_iota(jnp.int32, x.shape, 1).astype(F32)
    far = float(LANES)
    rmax = lambda a: jnp.max(a, axis=1, keepdims=True)
    first = lambda hit: jnp.min(jnp.where(hit, lane, far), axis=1, keepdims=True)

    gmask = lane < n_groups
    gl = jnp.where(gmask, x, -jnp.inf)
    gmax = rmax(gl)
    gsel = first(gl == gmax)
    p_g = 1.0 / jnp.sum(jnp.where(gmask, jnp.exp(x - gmax), 0.0), axis=1, keepdims=True)
    lo = n_groups + gsel * per_group
    emask = (lane >= lo) & (lane < lo + per_group)
    el = jnp.where(emask, x, -jnp.inf)
    m1 = rmax(el)
    i1 = first(el == m1)
    el2 = jnp.where(lane == i1, -jnp.inf, el)
    m2 = rmax(el2)
    i2 = first(el2 == m2)
    zsum = jnp.sum(jnp.where(emask, jnp.exp(x - m1), 0.0), axis=1, keepdims=True)
    p1 = 1.0 / zsum
    p2 = jnp.exp(m2 - m1) / zsum
    psum = p1 + p2
    gate1 = p_g * p1 / psum
    gate2 = p_g * p2 / psum
    e1 = i1 - n_groups
    e2 = i2 - n_groups
    hit1 = lane == e1
    hit2 = lane == e2
    onehot = jnp.where(hit1 | hit2, 1.0, 0.0)
    r = lax.broadcasted_iota(jnp.int32, (tr, tr), 0)
    c = lax.broadcasted_iota(jnp.int32, (tr, tr), 1)
    before = jnp.where(c < r, 1.0, 0.0).astype(BF16)
    seen = jnp.dot(before, onehot.astype(BF16), preferred_element_type=F32) + carry_ref[...]
    r1 = jnp.sum(jnp.where(hit1, seen, 0.0), axis=1, keepdims=True)
    r2 = jnp.sum(jnp.where(hit2, seen, 0.0), axis=1, keepdims=True)
    carry_ref[...] = carry_ref[...] + jnp.sum(onehot, axis=0, keepdims=True)
    cnt_ref[...] = carry_ref[...]

    col = lax.broadcasted_iota(jnp.int32, idx_ref.shape, 1)
    idx = jnp.where(col == 0, e1, jnp.where(col == 1, e2, jnp.where(col == 2, r1, r2)))
    idx_ref[...] = idx.astype(jnp.int32)
    gate_ref[...] = jnp.where(col == 0, gate1, jnp.where(col == 1, gate2, 0.0))


def _route(logits, n_groups, per_group):
    n = logits.shape[0]
    tr = ROW_TILE
    narrow = pl.BlockSpec((tr, 8), lambda i: (i, 0))
    return pl.pallas_call(
        functools.partial(_route_kernel, n_groups=n_groups, per_group=per_group),
        grid=(n // tr,),
        in_specs=[pl.BlockSpec((tr, LANES), lambda i: (i, 0))],
        out_specs=[narrow, narrow, pl.BlockSpec((1, LANES), lambda i: (0, 0))],
        out_shape=[jax.ShapeDtypeStruct((n, 8), jnp.int32), jax.ShapeDtypeStruct((n, 8), F32),
                   jax.ShapeDtypeStruct((1, LANES), F32)],
        scratch_shapes=[pltpu.VMEM((1, LANES), F32)],
        compiler_params=_params("arbitrary"),
        name="route",
    )(logits)


def _row_copy(src_ref, src_row, dst_ref, dst_row, sem):
    return pltpu.make_async_copy(src_ref.at[pl.ds(src_row, 1)], dst_ref.at[pl.ds(dst_row, 1)], sem)


def _wait_rows(src_ref, dst_ref, sem, n):
    pltpu.make_async_copy(src_ref.at[pl.ds(0, n)], dst_ref.at[pl.ds(0, n)], sem).wait()


def _dispatch_kernel(dest_ref, hn_ref, xs_ref, sem):
    td = hn_ref.shape[0]

    def start(r, _):
        for k in range(TOP_K):
            _row_copy(hn_ref, r, xs_ref, dest_ref[0, 0, TOP_K * r + k], sem).start()
        return 0

    for r in range(td):
        start(r, 0)
    for _ in range(TOP_K):
        _wait_rows(hn_ref, xs_ref, sem, td)


def _dispatch(dest, hn):
    n, d = hn.shape
    td = DISPATCH_TILE
    return pl.pallas_call(
        _dispatch_kernel,
        grid=(n // td,),
        in_specs=[pl.BlockSpec((1, 1, TOP_K * td), lambda i: (i, 0, 0), memory_space=pltpu.SMEM),
                  pl.BlockSpec((td, d), lambda i: (i, 0))],
        out_specs=pl.BlockSpec(memory_space=pl.ANY),
        out_shape=jax.ShapeDtypeStruct((n * TOP_K, d), F32),
        scratch_shapes=[pltpu.SemaphoreType.DMA(())],
        compiler_params=_params("arbitrary"),
        name="dispatch",
    )(dest.reshape(n // td, 1, TOP_K * td), hn)


def _expert_kernel(blk_ref, exp_ref, lo_ref, hi_ref, first_ref, xs_ref, w1_ref, w3_ref, w2_ref, o_ref):
    w = pl.program_id(0)
    lo, hi = lo_ref[w], hi_ref[w]

    def swiglu():
        rows = lax.broadcasted_iota(jnp.int32, xs_ref.shape, 0)
        x = jnp.where((rows >= lo) & (rows < hi), xs_ref[...], 0.0).astype(BF16)
        a = jnp.dot(x, w1_ref[...], preferred_element_type=F32)
        b = jnp.dot(x, w3_ref[...], preferred_element_type=F32)
        act = (a * jax.nn.sigmoid(a) * b).astype(BF16)
        return jnp.dot(act, w2_ref[...], preferred_element_type=F32)

    @pl.when((hi > lo) & (first_ref[w] == 1))
    def _():
        o_ref[...] = swiglu()

    @pl.when((hi > lo) & (first_ref[w] == 0))
    def _():
        o_ref[...] += swiglu()


def _experts(plan, xs, w1, w3, w2):
    m, d = xs.shape
    de = w1.shape[2]
    blk = EXPERT_BLOCK
    n_items = plan[0].shape[0]
    row_blk = lambda i, rb, ex, lo, hi, fs: (rb[i], 0)
    weight = lambda i, rb, ex, lo, hi, fs: (ex[i], 0, 0)
    return pl.pallas_call(
        _expert_kernel,
        grid_spec=pltpu.PrefetchScalarGridSpec(
            num_scalar_prefetch=5,
            grid=(n_items,),
            in_specs=[pl.BlockSpec((blk, d), row_blk),
                      pl.BlockSpec((None, d, de), weight),
                      pl.BlockSpec((None, d, de), weight),
                      pl.BlockSpec((None, de, d), weight)],
            out_specs=pl.BlockSpec((blk, d), row_blk)),
        out_shape=jax.ShapeDtypeStruct((m, d), F32),
        compiler_params=_params("arbitrary"),
        name="experts",
    )(*plan, xs, w1, w3, w2)


def _combine_kernel(dest_ref, next_ref, gate_ref, h_ref, rows_ref, g_ref, y_ref, buf, sem, *, final_norm):
    i = pl.program_id(0)
    last = pl.num_programs(0) - 1
    tc = h_ref.shape[0]
    slot = i % 2

    def gather(idx_ref, into, unrolled):
        def start(r, _):
            for k in range(TOP_K):
                _row_copy(rows_ref, idx_ref[0, 0, TOP_K * r + k], buf.at[into, k], r, sem.at[into]).start()
            return 0

        if unrolled:
            for r in range(tc):
                start(r, 0)
        else:
            lax.fori_loop(0, tc, start, 0, unroll=ISSUE_UNROLL)

    def drain(into):
        for k in range(TOP_K):
            _wait_rows(rows_ref, buf.at[into, k], sem.at[into], tc)

    pl.when(i == 0)(lambda: gather(dest_ref, 0, False))
    gather(next_ref, 1 - slot, True)
    drain(slot)
    gates = gate_ref[...]
    y = gates[:, 0:1] * buf[slot, 0] + gates[:, 1:2] * buf[slot, 1]
    h = h_ref[...] + y
    y_ref[...] = _rms(h, g_ref[...]) if final_norm else h
    pl.when(i == last)(lambda: drain(1 - slot))


def _combine(dest, gates, h, rows, g, n, row_off, final_norm):
    d = h.shape[1]
    tc = COMBINE_TILE
    off = row_off // tc
    last = off + n // tc - 1
    idx_spec = lambda f: pl.BlockSpec((1, 1, TOP_K * tc), lambda i: (f(i), 0, 0), memory_space=pltpu.SMEM)
    dest3 = dest.reshape(-1, 1, TOP_K * tc)
    return pl.pallas_call(
        functools.partial(_combine_kernel, final_norm=final_norm),
        grid=(n // tc,),
        in_specs=[idx_spec(lambda i: i + off), idx_spec(lambda i: jnp.minimum(i + off + 1, last)),
                  pl.BlockSpec((tc, 8), lambda i: (i + off, 0)),
                  pl.BlockSpec((tc, d), lambda i: (i + off, 0)),
                  pl.BlockSpec(memory_space=pl.ANY),
                  pl.BlockSpec((1, d), lambda i: (0, 0))],
        out_specs=pl.BlockSpec((tc, d), lambda i: (i, 0)),
        out_shape=jax.ShapeDtypeStruct((n, d), F32),
        scratch_shapes=[pltpu.VMEM((2, TOP_K, tc, d), F32), pltpu.SemaphoreType.DMA((2,))],
        compiler_params=_params("arbitrary"),
        name="combine",
    )(dest3, dest3, gates, h, rows, g)


def _round_up(a, m):
    return -(-a // m) * m


def _moe_plan(idx, counts, n_experts):
    blk = EXPERT_BLOCK
    n = idx.shape[0]
    i32 = lambda a: a.astype(jnp.int32)
    pick = lambda table, i: jnp.sum(jnp.where(i[..., None] == jnp.arange(table.shape[0]), table, 0), axis=-1)
    rank_of = lambda ends, x: jnp.sum(i32(ends[None, :] <= x[:, None]), axis=1)
    counts = i32(counts[0, :n_experts])
    end = jnp.cumsum(counts)
    start = end - counts
    dest = i32(pick(start, idx[:, :TOP_K]) + idx[:, TOP_K:2 * TOP_K])
    n_blocks = n * TOP_K // blk
    block_lo = jnp.arange(n_blocks, dtype=jnp.int32) * blk
    e_first = jnp.minimum(rank_of(end, block_lo), n_experts - 1)
    e_last = jnp.minimum(rank_of(end, block_lo + blk - 1), n_experts - 1)
    item_end = jnp.cumsum(e_last - e_first + 1)
    item_start = item_end - (e_last - e_first + 1)
    n_items = n_blocks + n_experts - 1
    w = jnp.arange(n_items, dtype=jnp.int32)
    live = w < item_end[-1]
    rb = jnp.minimum(rank_of(item_end, w), n_blocks - 1)
    ex = jnp.where(live, pick(e_first, rb) + (w - pick(item_start, rb)), e_last[-1])
    lo = jnp.where(live, jnp.clip(pick(start, ex) - rb * blk, 0, blk), 0)
    hi = jnp.where(live, jnp.clip(pick(end, ex) - rb * blk, 0, blk), 0)
    first = i32(w == pick(item_start, rb))
    return dest, (i32(rb), i32(ex), i32(lo), i32(hi), first)


def _split_w_in(w_in, width, n_heads):
    wm = jnp.concatenate([w_in[:, :3 * width], w_in[:, 3 * width + n_heads:]], axis=1).astype(BF16)
    wf = jnp.pad(w_in[:, 3 * width:3 * width + n_heads], ((0, 0), (0, LANES - n_heads))).astype(BF16)
    return wm, wf


def kernel(x_prompt, x_sample, cache_fox_k, cache_fox_v, cache_fox_logf, cache_sb_k, cache_sb_v,
           w_in, b_f, g_attn, g_out_fox, g_out_sb, w_out, g_ffn, w_group, b_group, w_router, b_router,
           w1, w3, w2, g_final):
    depth = w_in.shape[0]
    bp, sp, d = x_prompt.shape
    bs, ss, _ = x_sample.shape
    past = cache_fox_logf.shape[2]
    n_heads = cache_fox_logf.shape[3]
    width = n_heads * HEAD_DIM
    n_groups = w_group.shape[2]
    n_experts = w_router.shape[2]
    per_group = n_experts // n_groups
    n_p, n_s = bp * sp, bs * ss
    sk_s = _round_up(past + ss, LANES)
    pad_s = sk_s - past - ss
    row2 = lambda a: a.reshape(1, -1)

    hp = x_prompt.reshape(n_p, d)
    hs = x_sample.reshape(n_s, d)
    outs_p, outs_s = [], []
    for l in range(depth):
        wm, wf = _split_w_in(w_in[l], width, n_heads)
        bfp = jnp.pad(b_f[l], (0, LANES - n_heads)).reshape(1, LANES)
        wo = w_out[l].astype(BF16)
        wr = jnp.pad(jnp.concatenate([w_group[l], w_router[l]], axis=1),
                     ((0, 0), (0, LANES - n_groups - n_experts))).astype(BF16)
        br = jnp.pad(jnp.concatenate([b_group[l], b_router[l]]), (0, LANES - n_groups - n_experts)).reshape(1, LANES)
        g_a, g_f = row2(g_attn[l]), row2(g_ffn[l])
        g_of, g_os = row2(g_out_fox[l]), row2(g_out_sb[l])

        lf, cf, qf, kfb, kf, vfb, vf, qs, ksb, ks, vsb, vs = _in_proj(hp, g_a, wm, wf, bfp, n_heads, sp // ROW_TILE)
        b3 = lambda a: a.reshape(bp, sp, -1)
        tq = min(sp, ATTN_TILE)
        nf = _attention("fox", b3(qf), b3(kfb), b3(vfb), g_of, jnp.swapaxes(b3(cf), 1, 2), tq=tq, tk=tq, q_off=0)
        ns = _attention("sb", b3(qs), b3(ksb), b3(vsb), g_os, None, tq=tq, tk=tq, q_off=0)
        outs_p.append((kf, vf, lf, ks, vs))

        lf2, qf2, kfb2, kf2, vfb2, vf2, qs2, ksb2, ks2, vsb2, vs2 = _in_proj(hs, g_a, wm, wf, bfp, n_heads)
        c3 = lambda a: a.reshape(bs, ss, -1)
        keys = lambda cache, new: _with_cache(cache, l, c3(new), sk_s)
        cf_all = _cumsum(jnp.concatenate([cache_fox_logf[l].astype(F32), c3(lf2)], axis=1), past + ss)
        cf_all = jnp.swapaxes(jnp.pad(cf_all, ((0, 0), (0, pad_s), (0, 0))), 1, 2)
        nf2 = _attention("fox", c3(qf2), keys(cache_fox_k, kfb2), keys(cache_fox_v, vfb2), g_of, cf_all,
                         tq=ss, tk=sk_s, q_off=past)
        ns2 = _attention("sb", c3(qs2), keys(cache_sb_k, ksb2), keys(cache_sb_v, vsb2), g_os, None,
                         tq=ss, tk=sk_s, q_off=past)
        outs_s.append((kf2, vf2, lf2, ks2, vs2))

        h_all, hn_all, logits = _merge((nf.reshape(n_p, width), ns.reshape(n_p, width), hp),
                                       (nf2.reshape(n_s, width), ns2.reshape(n_s, width), hs), wo, g_f, wr, br)
        idx, gates, counts = _route(logits, n_groups, per_group)
        dest, plan = _moe_plan(idx, counts, n_experts)
        xs = _dispatch(dest, hn_all)
        rows = _experts(plan, xs, w1[l].astype(BF16), w3[l].astype(BF16), w2[l].astype(BF16))
        last = l == depth - 1
        g_last = row2(g_final)
        hp = _combine(dest, gates, h_all, rows, g_last, n_p, 0, last)
        hs = _combine(dest, gates, h_all, rows, g_last, n_s, n_p, last)

    heads = lambda a, b, s: a.reshape(b, s, n_heads, HEAD_DIM)
    stack = lambda outs, i, f: jnp.stack([f(o[i]) for o in outs])
    hp_f = lambda a: heads(a, bp, sp)
    hs_f = lambda a: heads(a, bs, ss)
    return (hp.reshape(bp, sp, d), hs.reshape(bs, ss, d),
            stack(outs_p, 0, hp_f), stack(outs_p, 1, hp_f), stack(outs_p, 2, lambda a: a.reshape(bp, sp, n_heads)),
            stack(outs_p, 3, hp_f), stack(outs_p, 4, hp_f),
            stack(outs_s, 0, hs_f), stack(outs_s, 1, hs_f), stack(outs_s, 2, lambda a: a.reshape(bs, ss, n_heads)),
            stack(outs_s, 3, hs_f), stack(outs_s, 4, hs_f))
```

```python
import functools
import math

import jax
import jax.numpy as jnp
from jax import lax
from jax.experimental import pallas as pl
from jax.experimental.pallas import tpu as pltpu

F32 = jnp.float32
BF16 = jnp.bfloat16

HEAD_DIM = 128
TOP_K = 2
EPS = 1e-6
LANES = 128
MXU_DIM = 256
NEG_BIG = -1e30
VMEM_LIMIT = 60 * 1024 * 1024
LOG2E = math.log2(math.e)
SCALE = HEAD_DIM ** -0.5
SCALE_LOG2 = SCALE * LOG2E

ROW_TILE = 256
ATTN_TILE = 512
FOX_GROUP = 4
SB_GROUP = 8
LOOKAHEAD = 2
EXPERT_BLOCK = 256
DISPATCH_TILE = 512
COMBINE_TILE = 256
ISSUE_UNROLL = 8


def _params(*sem):
    return pltpu.CompilerParams(dimension_semantics=sem, vmem_limit_bytes=VMEM_LIMIT)


def _log_sigmoid(x):
    return jnp.minimum(x, 0.0) - jnp.log1p(jnp.exp(-jnp.abs(x)))


def _rms(x, g):
    return x * lax.rsqrt(jnp.mean(x * x, axis=-1, keepdims=True) + EPS) * g


def _split3(x):
    hi = x.astype(BF16).astype(F32)
    r1 = x - hi
    mid = r1.astype(BF16).astype(F32)
    lo = (r1 - mid).astype(BF16).astype(F32)
    return hi, mid, lo


def _running_sum(x):
    n = x.shape[0]
    r = lax.broadcasted_iota(jnp.int32, (n, n), 0)
    c = lax.broadcasted_iota(jnp.int32, (n, n), 1)
    tri = jnp.where(c <= r, 1.0, 0.0).astype(BF16)
    return sum(jnp.dot(tri, p.astype(BF16), preferred_element_type=F32) for p in _split3(x))


def _in_proj_kernel(x_ref, g_ref, wm_ref, wf_ref, bf_ref, lf_ref, *rest, width, n_heads, seq_tiles):
    hn = _rms(x_ref[...], g_ref[...]).astype(BF16)
    refs = iter(rest[1:-1] if seq_tiles else rest)
    for c in range(6):
        res = jnp.dot(hn, wm_ref[:, c * width:(c + 1) * width], preferred_element_type=F32)
        next(refs)[...] = (res if c % 3 else res * SCALE_LOG2).astype(BF16)
        if c % 3:
            kv_ref = next(refs)
            kv_ref[...] = res.reshape(kv_ref.shape)
    lf = _log_sigmoid(jnp.dot(hn, wf_ref[...], preferred_element_type=F32) + bf_ref[...])
    lf_ref[...] = lf[:, :n_heads]
    if seq_tiles:
        cf_ref, carry_ref = rest[0], rest[-1]

        @pl.when(pl.program_id(0) % seq_tiles == 0)
        def _():
            carry_ref[...] = jnp.zeros_like(carry_ref)

        cf = _running_sum(lf) + carry_ref[...]
        cf_ref[...] = cf[:, :n_heads]
        carry_ref[...] = cf[lf.shape[0] - 1:, :]


def _in_proj(x, g, wm, wf, bf, n_heads, seq_tiles=0):
    n, d = x.shape
    width = wm.shape[1] // 6
    tm = ROW_TILE
    row = lambda w: pl.BlockSpec((tm, w), lambda i: (i, 0))
    const = lambda a: pl.BlockSpec(a.shape, lambda i: (0, 0), pipeline_mode=pl.Buffered(1))
    per_head = pl.BlockSpec((tm, n_heads, HEAD_DIM), lambda i: (i, 0, 0))
    rows_out = jax.ShapeDtypeStruct((n, width), BF16)
    heads_out = jax.ShapeDtypeStruct((n, n_heads, HEAD_DIM), F32)
    narrow_out = jax.ShapeDtypeStruct((n, n_heads), F32)
    mixer_specs = [row(width)] + [row(width), per_head] * 2
    mixer_shapes = [rows_out] + [rows_out, heads_out] * 2
    n_narrow = 2 if seq_tiles else 1
    return pl.pallas_call(
        functools.partial(_in_proj_kernel, width=width, n_heads=n_heads, seq_tiles=seq_tiles),
        grid=(n // tm,),
        in_specs=[row(d), const(g), const(wm), const(wf), const(bf)],
        out_specs=[row(n_heads)] * n_narrow + mixer_specs * 2,
        out_shape=[narrow_out] * n_narrow + mixer_shapes * 2,
        scratch_shapes=[pltpu.VMEM((1, LANES), F32)] if seq_tiles else [],
        compiler_params=_params("arbitrary"),
        name="in_proj",
    )(x, g, wm, wf, bf)


def _cumsum_kernel(lf_ref, cf_ref, carry_ref):
    tt = lf_ref.shape[0]

    @pl.when(pl.program_id(1) == 0)
    def _():
        carry_ref[...] = jnp.zeros_like(carry_ref)

    cf = _running_sum(lf_ref[...]) + carry_ref[...]
    cf_ref[...] = cf
    carry_ref[...] = cf[tt - 1:tt, :]


def _cumsum(lf, tt):
    b, t, h = lf.shape
    return pl.pallas_call(
        _cumsum_kernel,
        grid=(b, t // tt),
        in_specs=[pl.BlockSpec((None, tt, h), lambda i, j: (i, j, 0))],
        out_specs=pl.BlockSpec((None, tt, h), lambda i, j: (i, j, 0)),
        out_shape=jax.ShapeDtypeStruct((b, t, h), F32),
        scratch_shapes=[pltpu.VMEM((1, h), F32)],
        compiler_params=_params("arbitrary", "arbitrary"),
        name="cumsum",
    )(lf)


def _bias_rows(c_rows, h, as_query):
    n = c_rows.shape[1]
    sub = lax.broadcasted_iota(jnp.int32, c_rows.shape, 0)
    c = jnp.sum(jnp.where(sub == h, c_rows, 0.0), axis=0, keepdims=True) * LOG2E
    hi, mid, lo = _split3(c)
    if as_query:
        parts, base, ones_lo = (hi, mid, lo), 0, 3
    else:
        parts, base, ones_lo = (-hi, -mid, -lo), 3, 0
    row = lax.broadcasted_iota(jnp.int32, (8, n), 0)
    blk = jnp.where((row >= ones_lo) & (row < ones_lo + 3), 1.0, 0.0)
    for i, p in enumerate(parts):
        blk = jnp.where(row == base + i, p, blk)
    return blk


def _bias_block(groups, n):
    zeros = lambda rows: jnp.zeros((rows, n), F32)
    rows = [zeros(8) if g is None else g for g in groups] + [zeros(LANES - 8 * len(groups))]
    return jnp.concatenate(rows, axis=0).T.astype(BF16)


def _fox_kernel(q_ref, k_ref, v_ref, c_ref, gt_ref, o_ref, kaug, vt, *, tq, tk, q_off, prep, group, wide, one_tile):
    hg = pl.program_id(1)
    qi = pl.program_id(2)
    sk = k_ref.shape[0]
    hd = lambda g: slice(g * HEAD_DIM, (g + 1) * HEAD_DIM)

    @pl.when(qi == 0)
    def _():
        for g in range(group):
            for r0 in range(0, sk, prep):
                rows = pl.ds(r0, prep)
                kaug[g, rows, :HEAD_DIM] = k_ref[rows, hd(g)]
                kaug[g, rows, HEAD_DIM:] = _bias_block(
                    [None] * g + [_bias_rows(c_ref[:, r0:r0 + prep], hg * group + g, False)], prep)
                vt[g, :, rows] = v_ref[rows, hd(g)].astype(F32).T.astype(BF16)

    q_lo = q_off if one_tile else pl.multiple_of(q_off + qi * tq, tq)
    c_q = c_ref[:, pl.ds(q_lo, tq)]
    q_bias = _bias_block([_bias_rows(c_q, hg * group + g, True) for g in range(group)], tq)
    qa = [jnp.concatenate([q_ref[:, hd(g)], q_bias], axis=1) for g in range(group)]
    kiota = lax.broadcasted_iota(jnp.int32, (tk, tq), 0)
    qpos = q_lo + lax.broadcasted_iota(jnp.int32, (tk, tq), 1)

    def chunk(j, carry, width, masked=False):
        k0 = pl.multiple_of(j * width, width)

        def logits(g):
            s = lax.dot_general(kaug[g, pl.ds(k0, width), :], qa[g], (((1,), (1,)), ((), ())),
                                preferred_element_type=F32)
            if masked:
                s = jnp.where(k0 + kiota <= qpos, s, NEG_BIG)
            return s

        def absorb(g, s):
            m, l, acc = carry[g]
            m_new = jnp.maximum(m, jnp.max(s, axis=0, keepdims=True))
            p = jnp.exp2(s - m_new)
            alpha = jnp.exp2(m - m_new)
            l = alpha * l + jnp.sum(p, axis=0, keepdims=True)
            acc = alpha * acc + jnp.dot(vt[g, :, pl.ds(k0, width)], p.astype(BF16), preferred_element_type=F32)
            return m_new, l, acc

        out, pending = [], {}
        for g in range(group + LOOKAHEAD):
            if g < group:
                pending[g] = logits(g)
            if g >= LOOKAHEAD:
                out.append(absorb(g - LOOKAHEAD, pending.pop(g - LOOKAHEAD)))
        return tuple(out)

    n_full = (q_lo + 1) // tk
    n_all = (q_lo + tq + tk - 1) // tk
    carry = tuple((jnp.full((1, tq), NEG_BIG, F32), jnp.zeros((1, tq), F32), jnp.zeros((HEAD_DIM, tq), F32))
                  for _ in range(group))
    n_narrow = n_full
    if wide:
        carry = lax.fori_loop(0, n_full // 2, lambda j, c: chunk(j, c, 2 * tk), carry)
        n_narrow = n_full % 2
    carry = lax.fori_loop(n_full - n_narrow, n_full, lambda j, c: chunk(j, c, tk), carry)
    carry = lax.fori_loop(n_full, n_all, lambda j, c: chunk(j, c, tk, True), carry)
    for g in range(group):
        m, l, acc = carry[g]
        o = acc / l
        y = o * lax.rsqrt(jnp.mean(o * o, axis=0, keepdims=True) + EPS) * gt_ref[hd(g), :]
        o_ref[:, hd(g)] = y.T.astype(o_ref.dtype)


def _sb_kernel(q_ref, k_ref, v_ref, g_ref, o_ref, *, tq, tk, q_off, group, wide):
    qi = pl.program_id(2)
    hd = lambda g: slice(g * HEAD_DIM, (g + 1) * HEAD_DIM)
    q = [q_ref[:, hd(g)] for g in range(group)]
    q_lo = q_off + qi * tq
    ts = MXU_DIM if tk % MXU_DIM == 0 else LANES
    r = lax.broadcasted_iota(jnp.int32, (ts, ts), 0)
    c = lax.broadcasted_iota(jnp.int32, (ts, ts), 1)
    neg_from = jnp.where(r >= c, -1.0, 0.0).astype(BF16)
    sign_bit = jnp.uint32(0x80000000)

    def chunk(j, carry, width, masked=False):
        k0 = pl.multiple_of(j * width, width)
        n_sub = width // ts
        first_row = (lambda b: b * ts) if (masked and tq == tk) else (lambda b: 0)

        def logits(g, b):
            r0 = first_row(b)
            z = lax.dot_general(q[g][r0:], k_ref[pl.ds(k0 + b * ts, ts), hd(g)], (((1,), (1,)), ((), ())),
                                preferred_element_type=F32)
            neg_abs = pltpu.bitcast(pltpu.bitcast(z, jnp.uint32) | sign_bit, F32)
            sp = jnp.maximum(z, 0.0) + jnp.log2(1.0 + jnp.exp2(neg_abs))
            valid = None
            if masked:
                qpos = q_lo + r0 + lax.broadcasted_iota(jnp.int32, z.shape, 0)
                valid = k0 + b * ts + lax.broadcasted_iota(jnp.int32, z.shape, 1) < qpos
                sp = jnp.where(valid, sp, 0.0)
            hi = sp.astype(BF16)
            lo = (sp - hi.astype(F32)).astype(BF16)
            return z, hi, lo, valid

        def weights(state, after, r0):
            z, hi, lo, valid = state
            sfx = (jnp.dot(hi, neg_from, preferred_element_type=F32)
                   + jnp.dot(lo, neg_from, preferred_element_type=F32)) + after[r0:]
            a = jnp.exp2(z + sfx)
            if masked:
                a = jnp.where(valid, a, 0.0)
            total = sfx[:, 0:1]
            return (jnp.concatenate([after[:r0], total], axis=0) if r0 else total), a.astype(BF16)

        items = [(g, b) for b in reversed(range(n_sub)) for g in range(group)]
        after = [carry[g][0] for g in range(group)]
        acc = [carry[g][1] for g in range(group)]
        staged, ready = {}, {}
        n = len(items)
        for step in range(n + 2 * LOOKAHEAD):
            if step < n:
                staged[step] = logits(*items[step])
            if 0 <= step - LOOKAHEAD < n:
                g, b = items[step - LOOKAHEAD]
                after[g], ready[step - LOOKAHEAD] = weights(staged.pop(step - LOOKAHEAD), after[g], first_row(b))
            if 0 <= step - 2 * LOOKAHEAD < n:
                g, b = items[step - 2 * LOOKAHEAD]
                r0 = first_row(b)
                rows = pl.ds(k0 + b * ts, ts)
                pv = jnp.dot(ready.pop(step - 2 * LOOKAHEAD), v_ref[rows, hd(g)], preferred_element_type=F32)
                acc[g] = jnp.concatenate([acc[g][:r0], acc[g][r0:] + pv], axis=0) if r0 else acc[g] + pv
        return tuple((after[g], acc[g]) for g in range(group))

    n_full = q_lo // tk
    n_all = (q_lo + tq - 1 + tk - 1) // tk
    carry = tuple((jnp.zeros((tq, 1), F32), jnp.zeros((tq, HEAD_DIM), F32)) for _ in range(group))
    carry = lax.fori_loop(0, n_all - n_full, lambda i, c: chunk(n_all - 1 - i, c, tk, True), carry)
    n_narrow = n_full
    if wide:
        n_narrow = n_full % 2
    carry = lax.fori_loop(0, n_narrow, lambda i, c: chunk(n_full - 1 - i, c, tk), carry)
    if wide:
        carry = lax.fori_loop(0, n_full // 2, lambda i, c: chunk(n_full // 2 - 1 - i, c, 2 * tk), carry)
    for g in range(group):
        o_ref[:, hd(g)] = _rms(carry[g][1], g_ref[:, hd(g)]).astype(o_ref.dtype)


def _attention(kind, q, k, v, g, cf, *, tq, tk, q_off):
    b, sq, width = q.shape
    sk = k.shape[1]
    n_heads = width // HEAD_DIM
    group = FOX_GROUP if kind == "fox" else SB_GROUP
    assert sq % tq == 0 and sk % tk == 0 and q_off % tq == 0 and n_heads % group == 0
    w = group * HEAD_DIM
    wide = sk % (2 * tk) == 0
    q_spec = pl.BlockSpec((None, tq, w), lambda i, h, j: (i, j, h))
    kv_spec = pl.BlockSpec((None, sk, w), lambda i, h, j: (i, 0, h))
    g_spec = pl.BlockSpec((1, w), lambda i, h, j: (0, h))
    if kind == "fox":
        body = functools.partial(_fox_kernel, tq=tq, tk=tk, q_off=q_off, prep=math.gcd(sk, 512), group=group,
                                 wide=wide, one_tile=sq == tq)
        in_specs = [q_spec, kv_spec, kv_spec, pl.BlockSpec((None, n_heads, sk), lambda i, h, j: (i, 0, 0)),
                    pl.BlockSpec((w, 1), lambda i, h, j: (h, 0))]
        args = (q, k, v, cf, g.reshape(width, 1))
        scratch = [pltpu.VMEM((group, sk, 2 * HEAD_DIM), BF16), pltpu.VMEM((group, HEAD_DIM, sk), BF16)]
    else:
        body = functools.partial(_sb_kernel, tq=tq, tk=tk, q_off=q_off, group=group, wide=wide)
        in_specs = [q_spec, kv_spec, kv_spec, g_spec]
        args = (q, k, v, g)
        scratch = []
    return pl.pallas_call(
        body,
        grid=(b, n_heads // group, sq // tq),
        in_specs=in_specs,
        out_specs=q_spec,
        out_shape=jax.ShapeDtypeStruct((b, sq, width), BF16),
        scratch_shapes=scratch,
        compiler_params=_params("arbitrary", "arbitrary", "arbitrary"),
        name=kind,
    )(*args)


def _with_cache_kernel(cache_ref, new_ref, o_ref):
    past, n_heads, _ = cache_ref.shape
    fresh = new_ref.shape[0]
    o_ref[:past, :] = cache_ref[...].reshape(past, n_heads * HEAD_DIM).astype(BF16)
    o_ref[past:past + fresh, :] = new_ref[...]
    o_ref[past + fresh:, :] = jnp.zeros((o_ref.shape[0] - past - fresh, o_ref.shape[1]), BF16)


def _with_cache(cache, layer, new, rows):
    _, b, past, n_heads, _ = cache.shape
    fresh, width = new.shape[1:]
    return pl.pallas_call(
        _with_cache_kernel,
        grid=(b,),
        in_specs=[pl.BlockSpec((None, None, past, n_heads, HEAD_DIM), lambda i: (layer, i, 0, 0, 0)),
                  pl.BlockSpec((None, fresh, width), lambda i: (i, 0, 0))],
        out_specs=pl.BlockSpec((None, rows, width), lambda i: (i, 0, 0)),
        out_shape=jax.ShapeDtypeStruct((b, rows, width), BF16),
        compiler_params=_params("arbitrary"),
        name="with_cache",
    )(cache, new)


def _merge_kernel(nfp_ref, nsp_ref, xp_ref, nfs_ref, nss_ref, xs_ref, wo_ref, g_ref, wr_ref, br_ref,
                  h_ref, hn_ref, lg_ref, *, prompt_blocks):
    def body(nf_ref, ns_ref, x_ref):
        width = nf_ref.shape[1]
        h = x_ref[...] + (jnp.dot(nf_ref[...], wo_ref[:width, :], preferred_element_type=F32)
                          + jnp.dot(ns_ref[...], wo_ref[width:, :], preferred_element_type=F32))
        h_ref[...] = h
        hn = _rms(h, g_ref[...])
        hn_ref[...] = hn
        lg_ref[...] = jnp.dot(hn.astype(BF16), wr_ref[...], preferred_element_type=F32) + br_ref[...]

    is_prompt = pl.program_id(0) < prompt_blocks
    pl.when(is_prompt)(lambda: body(nfp_ref, nsp_ref, xp_ref))
    pl.when(jnp.logical_not(is_prompt))(lambda: body(nfs_ref, nss_ref, xs_ref))


def _merge(prompt, sample, wo, g, wr, br):
    n_p, d = prompt[2].shape
    n_s = sample[2].shape[0]
    width = prompt[0].shape[1]
    tm = ROW_TILE
    pb, sb = n_p // tm, n_s // tm
    n_total = n_p + n_s
    p_row = lambda w: pl.BlockSpec((tm, w), lambda i: (jnp.minimum(i, pb - 1), 0))
    s_row = lambda w: pl.BlockSpec((tm, w), lambda i: (jnp.maximum(i - pb, 0), 0))
    out_row = lambda w: pl.BlockSpec((tm, w), lambda i: (i, 0))
    const = lambda a: pl.BlockSpec(a.shape, lambda i: (0, 0), pipeline_mode=pl.Buffered(1))
    return pl.pallas_call(
        functools.partial(_merge_kernel, prompt_blocks=pb),
        grid=(pb + sb,),
        in_specs=[p_row(width), p_row(width), p_row(d), s_row(width), s_row(width), s_row(d),
                  const(wo), const(g), const(wr), const(br)],
        out_specs=[out_row(d), out_row(d), out_row(LANES)],
        out_shape=[jax.ShapeDtypeStruct((n_total, d), F32), jax.ShapeDtypeStruct((n_total, d), F32),
                   jax.ShapeDtypeStruct((n_total, LANES), F32)],
        compiler_params=_params("arbitrary"),
        name="merge",
    )(*prompt, *sample, wo, g, wr, br)


def _route_kernel(lg_ref, idx_ref, gate_ref, cnt_ref, carry_ref, *, n_groups, per_group):
    tr = lg_ref.shape[0]

    @pl.when(pl.program_id(0) == 0)
    def _():
        carry_ref[...] = jnp.zeros_like(carry_ref)

    x = lg_ref[...]
    lane = lax.broadcasted_iota(jnp.int32, x.shape, 1).astype(F32)
    far = float(LANES)
    rmax = lambda a: jnp.max(a, axis=1, keepdims=True)
    first = lambda hit: jnp.min(jnp.where(hit, lane, far), axis=1, keepdims=True)

    gmask = lane < n_groups
    gl = jnp.where(gmask, x, -jnp.inf)
    gmax = rmax(gl)
    gsel = first(gl == gmax)
    p_g = 1.0 / jnp.sum(jnp.where(gmask, jnp.exp(x - gmax), 0.0), axis=1, keepdims=True)
    lo = n_groups + gsel * per_group
    emask = (lane >= lo) & (lane < lo + per_group)
    el = jnp.where(emask, x, -jnp.inf)
    m1 = rmax(el)
    i1 = first(el == m1)
    el2 = jnp.where(lane == i1, -jnp.inf, el)
    m2 = rmax(el2)
    i2 = first(el2 == m2)
    zsum = jnp.sum(jnp.where(emask, jnp.exp(x - m1), 0.0), axis=1, keepdims=True)
    p1 = 1.0 / zsum
    p2 = jnp.exp(m2 - m1) / zsum
    psum = p1 + p2
    gate1 = p_g * p1 / psum
    gate2 = p_g * p2 / psum
    e1 = i1 - n_groups
    e2 = i2 - n_groups
    hit1 = lane == e1
    hit2 = lane == e2
    onehot = jnp.where(hit1 | hit2, 1.0, 0.0)
    r = lax.broadcasted_iota(jnp.int32, (tr, tr), 0)
    c = lax.broadcasted_iota(jnp.int32, (tr, tr), 1)
    before = jnp.where(c < r, 1.0, 0.0).astype(BF16)
    seen = jnp.dot(before, onehot.astype(BF16), preferred_element_type=F32) + carry_ref[...]
    r1 = jnp.sum(jnp.where(hit1, seen, 0.0), axis=1, keepdims=True)
    r2 = jnp.sum(jnp.where(hit2, seen, 0.0), axis=1, keepdims=True)
    carry_ref[...] = carry_ref[...] + jnp.sum(onehot, axis=0, keepdims=True)
    cnt_ref[...] = carry_ref[...]

    col = lax.broadcasted_iota(jnp.int32, idx_ref.shape, 1)
    idx = jnp.where(col == 0, e1, jnp.where(col == 1, e2, jnp.where(col == 2, r1, r2)))
    idx_ref[...] = idx.astype(jnp.int32)
    gate_ref[...] = jnp.where(col == 0, gate1, jnp.where(col == 1, gate2, 0.0))


def _route(logits, n_groups, per_group):
    n = logits.shape[0]
    tr = ROW_TILE
    narrow = pl.BlockSpec((tr, 8), lambda i: (i, 0))
    return pl.pallas_call(
        functools.partial(_route_kernel, n_groups=n_groups, per_group=per_group),
        grid=(n // tr,),
        in_specs=[pl.BlockSpec((tr, LANES), lambda i: (i, 0))],
        out_specs=[narrow, narrow, pl.BlockSpec((1, LANES), lambda i: (0, 0))],
        out_shape=[jax.ShapeDtypeStruct((n, 8), jnp.int32), jax.ShapeDtypeStruct((n, 8), F32),
                   jax.ShapeDtypeStruct((1, LANES), F32)],
        scratch_shapes=[pltpu.VMEM((1, LANES), F32)],
        compiler_params=_params("arbitrary"),
        name="route",
    )(logits)


def _row_copy(src_ref, src_row, dst_ref, dst_row, sem):
    return pltpu.make_async_copy(src_ref.at[pl.ds(src_row, 1)], dst_ref.at[pl.ds(dst_row, 1)], sem)


def _wait_rows(src_ref, dst_ref, sem, n):
    pltpu.make_async_copy(src_ref.at[pl.ds(0, n)], dst_ref.at[pl.ds(0, n)], sem).wait()


def _dispatch_kernel(dest_ref, hn_ref, xs_ref, sem):
    td = hn_ref.shape[0]

    def start(r, _):
        for k in range(TOP_K):
            _row_copy(hn_ref, r, xs_ref, dest_ref[0, 0, TOP_K * r + k], sem).start()
        return 0

    for r in range(td):
        start(r, 0)
    for _ in range(TOP_K):
        _wait_rows(hn_ref, xs_ref, sem, td)


def _dispatch(dest, hn):
    n, d = hn.shape
    td = DISPATCH_TILE
    return pl.pallas_call(
        _dispatch_kernel,
        grid=(n // td,),
        in_specs=[pl.BlockSpec((1, 1, TOP_K * td), lambda i: (i, 0, 0), memory_space=pltpu.SMEM),
                  pl.BlockSpec((td, d), lambda i: (i, 0))],
        out_specs=pl.BlockSpec(memory_space=pl.ANY),
        out_shape=jax.ShapeDtypeStruct((n * TOP_K, d), F32),
        scratch_shapes=[pltpu.SemaphoreType.DMA(())],
        compiler_params=_params("arbitrary"),
        name="dispatch",
    )(dest.reshape(n // td, 1, TOP_K * td), hn)


def _expert_kernel(blk_ref, exp_ref, lo_ref, hi_ref, first_ref, xs_ref, w1_ref, w3_ref, w2_ref, o_ref):
    w = pl.program_id(0)
    lo, hi = lo_ref[w], hi_ref[w]

    def swiglu():
        rows = lax.broadcasted_iota(jnp.int32, xs_ref.shape, 0)
        x = jnp.where((rows >= lo) & (rows < hi), xs_ref[...], 0.0).astype(BF16)
        a = jnp.dot(x, w1_ref[...], preferred_element_type=F32)
        b = jnp.dot(x, w3_ref[...], preferred_element_type=F32)
        act = (a * jax.nn.sigmoid(a) * b).astype(BF16)
        return jnp.dot(act, w2_ref[...], preferred_element_type=F32)

    @pl.when((hi > lo) & (first_ref[w] == 1))
    def _():
        o_ref[...] = swiglu()

    @pl.when((hi > lo) & (first_ref[w] == 0))
    def _():
        o_ref[...] += swiglu()


def _experts(plan, xs, w1, w3, w2):
    m, d = xs.shape
    de = w1.shape[2]
    blk = EXPERT_BLOCK
    n_items = plan[0].shape[0]
    row_blk = lambda i, rb, ex, lo, hi, fs: (rb[i], 0)
    weight = lambda i, rb, ex, lo, hi, fs: (ex[i], 0, 0)
    return pl.pallas_call(
        _expert_kernel,
        grid_spec=pltpu.PrefetchScalarGridSpec(
            num_scalar_prefetch=5,
            grid=(n_items,),
            in_specs=[pl.BlockSpec((blk, d), row_blk),
                      pl.BlockSpec((None, d, de), weight),
                      pl.BlockSpec((None, d, de), weight),
                      pl.BlockSpec((None, de, d), weight)],
            out_specs=pl.BlockSpec((blk, d), row_blk)),
        out_shape=jax.ShapeDtypeStruct((m, d), F32),
        compiler_params=_params("arbitrary"),
        name="experts",
    )(*plan, xs, w1, w3, w2)


def _combine_kernel(dest_ref, next_ref, gate_ref, h_ref, rows_ref, g_ref, y_ref, buf, sem, *, final_norm):
    i = pl.program_id(0)
    last = pl.num_programs(0) - 1
    tc = h_ref.shape[0]
    slot = i % 2

    def gather(idx_ref, into, unrolled):
        def start(r, _):
            for k in range(TOP_K):
                _row_copy(rows_ref, idx_ref[0, 0, TOP_K * r + k], buf.at[into, k], r, sem.at[into]).start()
            return 0

        if unrolled:
            for r in range(tc):
                start(r, 0)
        else:
            lax.fori_loop(0, tc, start, 0, unroll=ISSUE_UNROLL)

    def drain(into):
        for k in range(TOP_K):
            _wait_rows(rows_ref, buf.at[into, k], sem.at[into], tc)

    pl.when(i == 0)(lambda: gather(dest_ref, 0, False))
    gather(next_ref, 1 - slot, True)
    drain(slot)
    gates = gate_ref[...]
    y = gates[:, 0:1] * buf[slot, 0] + gates[:, 1:2] * buf[slot, 1]
    h = h_ref[...] + y
    y_ref[...] = _rms(h, g_ref[...]) if final_norm else h
    pl.when(i == last)(lambda: drain(1 - slot))


def _combine(dest, gates, h, rows, g, n, row_off, final_norm):
    d = h.shape[1]
    tc = COMBINE_TILE
    off = row_off // tc
    last = off + n // tc - 1
    idx_spec = lambda f: pl.BlockSpec((1, 1, TOP_K * tc), lambda i: (f(i), 0, 0), memory_space=pltpu.SMEM)
    dest3 = dest.reshape(-1, 1, TOP_K * tc)
    return pl.pallas_call(
        functools.partial(_combine_kernel, final_norm=final_norm),
        grid=(n // tc,),
        in_specs=[idx_spec(lambda i: i + off), idx_spec(lambda i: jnp.minimum(i + off + 1, last)),
                  pl.BlockSpec((tc, 8), lambda i: (i + off, 0)),
                  pl.BlockSpec((tc, d), lambda i: (i + off, 0)),
                  pl.BlockSpec(memory_space=pl.ANY),
                  pl.BlockSpec((1, d), lambda i: (0, 0))],
        out_specs=pl.BlockSpec((tc, d), lambda i: (i, 0)),
        out_shape=jax.ShapeDtypeStruct((n, d), F32),
        scratch_shapes=[pltpu.VMEM((2, TOP_K, tc, d), F32), pltpu.SemaphoreType.DMA((2,))],
        compiler_params=_params("arbitrary"),
        name="combine",
    )(dest3, dest3, gates, h, rows, g)


def _round_up(a, m):
    return -(-a // m) * m


def _moe_plan(idx, counts, n_experts):
    blk = EXPERT_BLOCK
    n = idx.shape[0]
    i32 = lambda a: a.astype(jnp.int32)
    pick = lambda table, i: jnp.sum(jnp.where(i[..., None] == jnp.arange(table.shape[0]), table, 0), axis=-1)
    rank_of = lambda ends, x: jnp.sum(i32(ends[None, :] <= x[:, None]), axis=1)
    counts = i32(counts[0, :n_experts])
    end = jnp.cumsum(counts)
    start = end - counts
    dest = i32(pick(start, idx[:, :TOP_K]) + idx[:, TOP_K:2 * TOP_K])
    n_blocks = n * TOP_K // blk
    block_lo = jnp.arange(n_blocks, dtype=jnp.int32) * blk
    e_first = jnp.minimum(rank_of(end, block_lo), n_experts - 1)
    e_last = jnp.minimum(rank_of(end, block_lo + blk - 1), n_experts - 1)
    item_end = jnp.cumsum(e_last - e_first + 1)
    item_start = item_end - (e_last - e_first + 1)
    n_items = n_blocks + n_experts - 1
    w = jnp.arange(n_items, dtype=jnp.int32)
    live = w < item_end[-1]
    rb = jnp.minimum(rank_of(item_end, w), n_blocks - 1)
    ex = jnp.where(live, pick(e_first, rb) + (w - pick(item_start, rb)), e_last[-1])
    lo = jnp.where(live, jnp.clip(pick(start, ex) - rb * blk, 0, blk), 0)
    hi = jnp.where(live, jnp.clip(pick(end, ex) - rb * blk, 0, blk), 0)
    first = i32(w == pick(item_start, rb))
    return dest, (i32(rb), i32(ex), i32(lo), i32(hi), first)


def _split_w_in(w_in, width, n_heads):
    wm = jnp.concatenate([w_in[:, :3 * width], w_in[:, 3 * width + n_heads:]], axis=1).astype(BF16)
    wf = jnp.pad(w_in[:, 3 * width:3 * width + n_heads], ((0, 0), (0, LANES - n_heads))).astype(BF16)
    return wm, wf


def kernel(x_prompt, x_sample, cache_fox_k, cache_fox_v, cache_fox_logf, cache_sb_k, cache_sb_v,
           w_in, b_f, g_attn, g_out_fox, g_out_sb, w_out, g_ffn, w_group, b_group, w_router, b_router,
           w1, w3, w2, g_final):
    depth = w_in.shape[0]
    bp, sp, d = x_prompt.shape
    bs, ss, _ = x_sample.shape
    past = cache_fox_logf.shape[2]
    n_heads = cache_fox_logf.shape[3]
    width = n_heads * HEAD_DIM
    n_groups = w_group.shape[2]
    n_experts = w_router.shape[2]
    per_group = n_experts // n_groups
    n_p, n_s = bp * sp, bs * ss
    sk_s = _round_up(past + ss, LANES)
    pad_s = sk_s - past - ss
    row2 = lambda a: a.reshape(1, -1)

    hp = x_prompt.reshape(n_p, d)
    hs = x_sample.reshape(n_s, d)
    outs_p, outs_s = [], []
    for l in range(depth):
        wm, wf = _split_w_in(w_in[l], width, n_heads)
        bfp = jnp.pad(b_f[l], (0, LANES - n_heads)).reshape(1, LANES)
        wo = w_out[l].astype(BF16)
        wr = jnp.pad(jnp.concatenate([w_group[l], w_router[l]], axis=1),
                     ((0, 0), (0, LANES - n_groups - n_experts))).astype(BF16)
        br = jnp.pad(jnp.concatenate([b_group[l], b_router[l]]), (0, LANES - n_groups - n_experts)).reshape(1, LANES)
        g_a, g_f = row2(g_attn[l]), row2(g_ffn[l])
        g_of, g_os = row2(g_out_fox[l]), row2(g_out_sb[l])

        lf, cf, qf, kfb, kf, vfb, vf, qs, ksb, ks, vsb, vs = _in_proj(hp, g_a, wm, wf, bfp, n_heads, sp // ROW_TILE)
        b3 = lambda a: a.reshape(bp, sp, -1)
        tq = min(sp, ATTN_TILE)
        nf = _attention("fox", b3(qf), b3(kfb), b3(vfb), g_of, jnp.swapaxes(b3(cf), 1, 2), tq=tq, tk=tq, q_off=0)
        ns = _attention("sb", b3(qs), b3(ksb), b3(vsb), g_os, None, tq=tq, tk=tq, q_off=0)
        outs_p.append((kf, vf, lf, ks, vs))

        lf2, qf2, kfb2, kf2, vfb2, vf2, qs2, ksb2, ks2, vsb2, vs2 = _in_proj(hs, g_a, wm, wf, bfp, n_heads)
        c3 = lambda a: a.reshape(bs, ss, -1)
        keys = lambda cache, new: _with_cache(cache, l, c3(new), sk_s)
        cf_all = _cumsum(jnp.concatenate([cache_fox_logf[l].astype(F32), c3(lf2)], axis=1), past + ss)
        cf_all = jnp.swapaxes(jnp.pad(cf_all, ((0, 0), (0, pad_s), (0, 0))), 1, 2)
        nf2 = _attention("fox", c3(qf2), keys(cache_fox_k, kfb2), keys(cache_fox_v, vfb2), g_of, cf_all,
                         tq=ss, tk=sk_s, q_off=past)
        ns2 = _attention("sb", c3(qs2), keys(cache_sb_k, ksb2), keys(cache_sb_v, vsb2), g_os, None,
                         tq=ss, tk=sk_s, q_off=past)
        outs_s.append((kf2, vf2, lf2, ks2, vs2))

        h_all, hn_all, logits = _merge((nf.reshape(n_p, width), ns.reshape(n_p, width), hp),
                                       (nf2.reshape(n_s, width), ns2.reshape(n_s, width), hs), wo, g_f, wr, br)
        idx, gates, counts = _route(logits, n_groups, per_group)
        dest, plan = _moe_plan(idx, counts, n_experts)
        xs = _dispatch(dest, hn_all)
        rows = _experts(plan, xs, w1[l].astype(BF16), w3[l].astype(BF16), w2[l].astype(BF16))
        last = l == depth - 1
        g_last = row2(g_final)
        hp = _combine(dest, gates, h_all, rows, g_last, n_p, 0, last)
        hs = _combine(dest, gates, h_all, rows, g_last, n_s, n_p, last)

    heads = lambda a, b, s: a.reshape(b, s, n_heads, HEAD_DIM)
    stack = lambda outs, i, f: jnp.stack([f(o[i]) for o in outs])
    hp_f = lambda a: heads(a, bp, sp)
    hs_f = lambda a: heads(a, bs, ss)
    return (hp.reshape(bp, sp, d), hs.reshape(bs, ss, d),
            stack(outs_p, 0, hp_f), stack(outs_p, 1, hp_f), stack(outs_p, 2, lambda a: a.reshape(bp, sp, n_heads)),
            stack(outs_p, 3, hp_f), stack(outs_p, 4, hp_f),
            stack(outs_s, 0, hs_f), stack(outs_s, 1, hs_f), stack(outs_s, 2, lambda a: a.reshape(bs, ss, n_heads)),
            stack(outs_s, 3, hs_f), stack(outs_s, 4, hs_f))
```

```python
import functools
import math

import jax
import jax.numpy as jnp
from jax import lax
from jax.experimental import pallas as pl
from jax.experimental.pallas import tpu as pltpu

F32 = jnp.float32
BF16 = jnp.bfloat16

HEAD_DIM = 128
TOP_K = 2
EPS = 1e-6
LANES = 128
MXU_DIM = 256
NEG_BIG = -1e30
VMEM_LIMIT = 60 * 1024 * 1024
LOG2E = math.log2(math.e)
SCALE = HEAD_DIM ** -0.5
SCALE_LOG2 = SCALE * LOG2E

ROW_TILE = 256
ATTN_TILE = 512
FOX_GROUP = 4
SB_GROUP = 8
LOOKAHEAD = 2
EXPERT_BLOCK = 256
DISPATCH_TILE = 512
COMBINE_TILE = 256
ISSUE_UNROLL = 8


def _params(*sem):
    return pltpu.CompilerParams(dimension_semantics=sem, vmem_limit_bytes=VMEM_LIMIT)


def _log_sigmoid(x):
    return jnp.minimum(x, 0.0) - jnp.log1p(jnp.exp(-jnp.abs(x)))


def _rms(x, g):
    return x * lax.rsqrt(jnp.mean(x * x, axis=-1, keepdims=True) + EPS) * g


def _split3(x):
    hi = x.astype(BF16).astype(F32)
    r1 = x - hi
    mid = r1.astype(BF16).astype(F32)
    lo = (r1 - mid).astype(BF16).astype(F32)
    return hi, mid, lo


def _running_sum(x):
    n = x.shape[0]
    r = lax.broadcasted_iota(jnp.int32, (n, n), 0)
    c = lax.broadcasted_iota(jnp.int32, (n, n), 1)
    tri = jnp.where(c <= r, 1.0, 0.0).astype(BF16)
    return sum(jnp.dot(tri, p.astype(BF16), preferred_element_type=F32) for p in _split3(x))


def _in_proj_kernel(x_ref, g_ref, wm_ref, wf_ref, bf_ref, lf_ref, *rest, width, n_heads, seq_tiles):
    hn = _rms(x_ref[...], g_ref[...]).astype(BF16)
    refs = iter(rest[1:-1] if seq_tiles else rest)
    for c in range(6):
        res = jnp.dot(hn, wm_ref[:, c * width:(c + 1) * width], preferred_element_type=F32)
        next(refs)[...] = (res if c % 3 else res * SCALE_LOG2).astype(BF16)
        if c % 3:
            kv_ref = next(refs)
            kv_ref[...] = res.reshape(kv_ref.shape)
    lf = _log_sigmoid(jnp.dot(hn, wf_ref[...], preferred_element_type=F32) + bf_ref[...])
    lf_ref[...] = lf[:, :n_heads]
    if seq_tiles:
        cf_ref, carry_ref = rest[0], rest[-1]

        @pl.when(pl.program_id(0) % seq_tiles == 0)
        def _():
            carry_ref[...] = jnp.zeros_like(carry_ref)

        cf = _running_sum(lf) + carry_ref[...]
        cf_ref[...] = cf[:, :n_heads]
        carry_ref[...] = cf[lf.shape[0] - 1:, :]


def _in_proj(x, g, wm, wf, bf, n_heads, seq_tiles=0):
    n, d = x.shape
    width = wm.shape[1] // 6
    tm = ROW_TILE
    row = lambda w: pl.BlockSpec((tm, w), lambda i: (i, 0))
    const = lambda a: pl.BlockSpec(a.shape, lambda i: (0, 0), pipeline_mode=pl.Buffered(1))
    per_head = pl.BlockSpec((tm, n_heads, HEAD_DIM), lambda i: (i, 0, 0))
    rows_out = jax.ShapeDtypeStruct((n, width), BF16)
    heads_out = jax.ShapeDtypeStruct((n, n_heads, HEAD_DIM), F32)
    narrow_out = jax.ShapeDtypeStruct((n, n_heads), F32)
    mixer_specs = [row(width)] + [row(width), per_head] * 2
    mixer_shapes = [rows_out] + [rows_out, heads_out] * 2
    n_narrow = 2 if seq_tiles else 1
    return pl.pallas_call(
        functools.partial(_in_proj_kernel, width=width, n_heads=n_heads, seq_tiles=seq_tiles),
        grid=(n // tm,),
        in_specs=[row(d), const(g), const(wm), const(wf), const(bf)],
        out_specs=[row(n_heads)] * n_narrow + mixer_specs * 2,
        out_shape=[narrow_out] * n_narrow + mixer_shapes * 2,
        scratch_shapes=[pltpu.VMEM((1, LANES), F32)] if seq_tiles else [],
        compiler_params=_params("arbitrary"),
        name="in_proj",
    )(x, g, wm, wf, bf)


def _cumsum_kernel(lf_ref, cf_ref, carry_ref):
    tt = lf_ref.shape[0]

    @pl.when(pl.program_id(1) == 0)
    def _():
        carry_ref[...] = jnp.zeros_like(carry_ref)

    cf = _running_sum(lf_ref[...]) + carry_ref[...]
    cf_ref[...] = cf
    carry_ref[...] = cf[tt - 1:tt, :]


def _cumsum(lf, tt):
    b, t, h = lf.shape
    return pl.pallas_call(
        _cumsum_kernel,
        grid=(b, t // tt),
        in_specs=[pl.BlockSpec((None, tt, h), lambda i, j: (i, j, 0))],
        out_specs=pl.BlockSpec((None, tt, h), lambda i, j: (i, j, 0)),
        out_shape=jax.ShapeDtypeStruct((b, t, h), F32),
        scratch_shapes=[pltpu.VMEM((1, h), F32)],
        compiler_params=_params("arbitrary", "arbitrary"),
        name="cumsum",
    )(lf)


def _bias_rows(c_rows, h, as_query):
    n = c_rows.shape[1]
    sub = lax.broadcasted_iota(jnp.int32, c_rows.shape, 0)
    c = jnp.sum(jnp.where(sub == h, c_rows, 0.0), axis=0, keepdims=True) * LOG2E
    hi, mid, lo = _split3(c)
    if as_query:
        parts, base, ones_lo = (hi, mid, lo), 0, 3
    else:
        parts, base, ones_lo = (-hi, -mid, -lo), 3, 0
    row = lax.broadcasted_iota(jnp.int32, (8, n), 0)
    blk = jnp.where((row >= ones_lo) & (row < ones_lo + 3), 1.0, 0.0)
    for i, p in enumerate(parts):
        blk = jnp.where(row == base + i, p, blk)
    return blk


def _bias_block(groups, n):
    zeros = lambda rows: jnp.zeros((rows, n), F32)
    rows = [zeros(8) if g is None else g for g in groups] + [zeros(LANES - 8 * len(groups))]
    return jnp.concatenate(rows, axis=0).T.astype(BF16)


def _fox_kernel(q_ref, k_ref, v_ref, c_ref, gt_ref, o_ref, kaug, vt, *, tq, tk, q_off, prep, group, wide, one_tile):
    hg = pl.program_id(1)
    qi = pl.program_id(2)
    sk = k_ref.shape[0]
    hd = lambda g: slice(g * HEAD_DIM, (g + 1) * HEAD_DIM)

    @pl.when(qi == 0)
    def _():
        for g in range(group):
            for r0 in range(0, sk, prep):
                rows = pl.ds(r0, prep)
                kaug[g, rows, :HEAD_DIM] = k_ref[rows, hd(g)]
                kaug[g, rows, HEAD_DIM:] = _bias_block(
                    [None] * g + [_bias_rows(c_ref[:, r0:r0 + prep], hg * group + g, False)], prep)
                vt[g, :, rows] = v_ref[rows, hd(g)].astype(F32).T.astype(BF16)

    q_lo = q_off if one_tile else pl.multiple_of(q_off + qi * tq, tq)
    c_q = c_ref[:, pl.ds(q_lo, tq)]
    q_bias = _bias_block([_bias_rows(c_q, hg * group + g, True) for g in range(group)], tq)
    qa = [jnp.concatenate([q_ref[:, hd(g)], q_bias], axis=1) for g in range(group)]
    kiota = lax.broadcasted_iota(jnp.int32, (tk, tq), 0)
    qpos = q_lo + lax.broadcasted_iota(jnp.int32, (tk, tq), 1)

    def chunk(j, carry, width, masked=False):
        k0 = pl.multiple_of(j * width, width)

        def logits(g):
            s = lax.dot_general(kaug[g, pl.ds(k0, width), :], qa[g], (((1,), (1,)), ((), ())),
                                preferred_element_type=F32)
            if masked:
                s = jnp.where(k0 + kiota <= qpos, s, NEG_BIG)
            return s

        def absorb(g, s):
            m, l, acc = carry[g]
            m_new = jnp.maximum(m, jnp.max(s, axis=0, keepdims=True))
            p = jnp.exp2(s - m_new)
            alpha = jnp.exp2(m - m_new)
            l = alpha * l + jnp.sum(p, axis=0, keepdims=True)
            acc = alpha * acc + jnp.dot(vt[g, :, pl.ds(k0, width)], p.astype(BF16), preferred_element_type=F32)
            return m_new, l, acc

        out, pending = [], {}
        for g in range(group + LOOKAHEAD):
            if g < group:
                pending[g] = logits(g)
            if g >= LOOKAHEAD:
                out.append(absorb(g - LOOKAHEAD, pending.pop(g - LOOKAHEAD)))
        return tuple(out)

    n_full = (q_lo + 1) // tk
    n_all = (q_lo + tq + tk - 1) // tk
    carry = tuple((jnp.full((1, tq), NEG_BIG, F32), jnp.zeros((1, tq), F32), jnp.zeros((HEAD_DIM, tq), F32))
                  for _ in range(group))
    n_narrow = n_full
    if wide:
        carry = lax.fori_loop(0, n_full // 2, lambda j, c: chunk(j, c, 2 * tk), carry)
        n_narrow = n_full % 2
    carry = lax.fori_loop(n_full - n_narrow, n_full, lambda j, c: chunk(j, c, tk), carry)
    carry = lax.fori_loop(n_full, n_all, lambda j, c: chunk(j, c, tk, True), carry)
    for g in range(group):
        m, l, acc = carry[g]
        o = acc / l
        y = o * lax.rsqrt(jnp.mean(o * o, axis=0, keepdims=True) + EPS) * gt_ref[hd(g), :]
        o_ref[:, hd(g)] = y.T.astype(o_ref.dtype)


def _sb_kernel(q_ref, k_ref, v_ref, g_ref, o_ref, *, tq, tk, q_off, group, wide):
    qi = pl.program_id(2)
    hd = lambda g: slice(g * HEAD_DIM, (g + 1) * HEAD_DIM)
    q = [q_ref[:, hd(g)] for g in range(group)]
    q_lo = q_off + qi * tq
    ts = MXU_DIM if tk % MXU_DIM == 0 else LANES
    r = lax.broadcasted_iota(jnp.int32, (ts, ts), 0)
    c = lax.broadcasted_iota(jnp.int32, (ts, ts), 1)
    neg_from = jnp.where(r >= c, -1.0, 0.0).astype(BF16)
    sign_bit = jnp.uint32(0x80000000)

    def chunk(j, carry, width, masked=False):
        k0 = pl.multiple_of(j * width, width)
        n_sub = width // ts
        first_row = (lambda b: b * ts) if (masked and tq == tk) else (lambda b: 0)

        def logits(g, b):
            r0 = first_row(b)
            z = lax.dot_general(q[g][r0:], k_ref[pl.ds(k0 + b * ts, ts), hd(g)], (((1,), (1,)), ((), ())),
                                preferred_element_type=F32)
            neg_abs = pltpu.bitcast(pltpu.bitcast(z, jnp.uint32) | sign_bit, F32)
            sp = jnp.maximum(z, 0.0) + jnp.log2(1.0 + jnp.exp2(neg_abs))
            valid = None
            if masked:
                qpos = q_lo + r0 + lax.broadcasted_iota(jnp.int32, z.shape, 0)
                valid = k0 + b * ts + lax.broadcasted_iota(jnp.int32, z.shape, 1) < qpos
                sp = jnp.where(valid, sp, 0.0)
            hi = sp.astype(BF16)
            lo = (sp - hi.astype(F32)).astype(BF16)
            return z, hi, lo, valid

        def weights(state, after, r0):
            z, hi, lo, valid = state
            sfx = (jnp.dot(hi, neg_from, preferred_element_type=F32)
                   + jnp.dot(lo, neg_from, preferred_element_type=F32)) + after[r0:]
            a = jnp.exp2(z + sfx)
            if masked:
                a = jnp.where(valid, a, 0.0)
            total = sfx[:, 0:1]
            return (jnp.concatenate([after[:r0], total], axis=0) if r0 else total), a.astype(BF16)

        items = [(g, b) for b in reversed(range(n_sub)) for g in range(group)]
        after = [carry[g][0] for g in range(group)]
        acc = [carry[g][1] for g in range(group)]
        staged, ready = {}, {}
        n = len(items)
        for step in range(n + 2 * LOOKAHEAD):
            if step < n:
                staged[step] = logits(*items[step])
            if 0 <= step - LOOKAHEAD < n:
                g, b = items[step - LOOKAHEAD]
                after[g], ready[step - LOOKAHEAD] = weights(staged.pop(step - LOOKAHEAD), after[g], first_row(b))
            if 0 <= step - 2 * LOOKAHEAD < n:
                g, b = items[step - 2 * LOOKAHEAD]
                r0 = first_row(b)
                rows = pl.ds(k0 + b * ts, ts)
                pv = jnp.dot(ready.pop(step - 2 * LOOKAHEAD), v_ref[rows, hd(g)], preferred_element_type=F32)
                acc[g] = jnp.concatenate([acc[g][:r0], acc[g][r0:] + pv], axis=0) if r0 else acc[g] + pv
        return tuple((after[g], acc[g]) for g in range(group))

    n_full = q_lo // tk
    n_all = (q_lo + tq - 1 + tk - 1) // tk
    carry = tuple((jnp.zeros((tq, 1), F32), jnp.zeros((tq, HEAD_DIM), F32)) for _ in range(group))
    carry = lax.fori_loop(0, n_all - n_full, lambda i, c: chunk(n_all - 1 - i, c, tk, True), carry)
    n_narrow = n_full
    if wide:
        n_narrow = n_full % 2
    carry = lax.fori_loop(0, n_narrow, lambda i, c: chunk(n_full - 1 - i, c, tk), carry)
    if wide:
        carry = lax.fori_loop(0, n_full // 2, lambda i, c: chunk(n_full // 2 - 1 - i, c, 2 * tk), carry)
    for g in range(group):
        o_ref[:, hd(g)] = _rms(carry[g][1], g_ref[:, hd(g)]).astype(o_ref.dtype)


def _attention(kind, q, k, v, g, cf, *, tq, tk, q_off):
    b, sq, width = q.shape
    sk = k.shape[1]
    n_heads = width // HEAD_DIM
    group = FOX_GROUP if kind == "fox" else SB_GROUP
    assert sq % tq == 0 and sk % tk == 0 and q_off % tq == 0 and n_heads % group == 0
    w = group * HEAD_DIM
    wide = sk % (2 * tk) == 0
    q_spec = pl.BlockSpec((None, tq, w), lambda i, h, j: (i, j, h))
    kv_spec = pl.BlockSpec((None, sk, w), lambda i, h, j: (i, 0, h))
    g_spec = pl.BlockSpec((1, w), lambda i, h, j: (0, h))
    if kind == "fox":
        body = functools.partial(_fox_kernel, tq=tq, tk=tk, q_off=q_off, prep=math.gcd(sk, 512), group=group,
                                 wide=wide, one_tile=sq == tq)
        in_specs = [q_spec, kv_spec, kv_spec, pl.BlockSpec((None, n_heads, sk), lambda i, h, j: (i, 0, 0)),
                    pl.BlockSpec((w, 1), lambda i, h, j: (h, 0))]
        args = (q, k, v, cf, g.reshape(width, 1))
        scratch = [pltpu.VMEM((group, sk, 2 * HEAD_DIM), BF16), pltpu.VMEM((group, HEAD_DIM, sk), BF16)]
    else:
        body = functools.partial(_sb_kernel, tq=tq, tk=tk, q_off=q_off, group=group, wide=wide)
        in_specs = [q_spec, kv_spec, kv_spec, g_spec]
        args = (q, k, v, g)
        scratch = []
    return pl.pallas_call(
        body,
        grid=(b, n_heads // group, sq // tq),
        in_specs=in_specs,
        out_specs=q_spec,
        out_shape=jax.ShapeDtypeStruct((b, sq, width), BF16),
        scratch_shapes=scratch,
        compiler_params=_params("arbitrary", "arbitrary", "arbitrary"),
        name=kind,
    )(*args)


def _with_cache_kernel(cache_ref, new_ref, o_ref):
    past, n_heads, _ = cache_ref.shape
    fresh = new_ref.shape[0]
    o_ref[:past, :] = cache_ref[...].reshape(past, n_heads * HEAD_DIM).astype(BF16)
    o_ref[past:past + fresh, :] = new_ref[...]
    o_ref[past + fresh:, :] = jnp.zeros((o_ref.shape[0] - past - fresh, o_ref.shape[1]), BF16)


def _with_cache(cache, layer, new, rows):
    _, b, past, n_heads, _ = cache.shape
    fresh, width = new.shape[1:]
    return pl.pallas_call(
        _with_cache_kernel,
        grid=(b,),
        in_specs=[pl.BlockSpec((None, None, past, n_heads, HEAD_DIM), lambda i: (layer, i, 0, 0, 0)),
                  pl.BlockSpec((None, fresh, width), lambda i: (i, 0, 0))],
        out_specs=pl.BlockSpec((None, rows, width), lambda i: (i, 0, 0)),
        out_shape=jax.ShapeDtypeStruct((b, rows, width), BF16),
        compiler_params=_params("arbitrary"),
        name="with_cache",
    )(cache, new)


def _merge_kernel(nfp_ref, nsp_ref, xp_ref, nfs_ref, nss_ref, xs_ref, wo_ref, g_ref, wr_ref, br_ref,
                  h_ref, hn_ref, lg_ref, *, prompt_blocks):
    def body(nf_ref, ns_ref, x_ref):
        width = nf_ref.shape[1]
        h = x_ref[...] + (jnp.dot(nf_ref[...], wo_ref[:width, :], preferred_element_type=F32)
                          + jnp.dot(ns_ref[...], wo_ref[width:, :], preferred_element_type=F32))
        h_ref[...] = h
        hn = _rms(h, g_ref[...])
        hn_ref[...] = hn
        lg_ref[...] = jnp.dot(hn.astype(BF16), wr_ref[...], preferred_element_type=F32) + br_ref[...]

    is_prompt = pl.program_id(0) < prompt_blocks
    pl.when(is_prompt)(lambda: body(nfp_ref, nsp_ref, xp_ref))
    pl.when(jnp.logical_not(is_prompt))(lambda: body(nfs_ref, nss_ref, xs_ref))


def _merge(prompt, sample, wo, g, wr, br):
    n_p, d = prompt[2].shape
    n_s = sample[2].shape[0]
    width = prompt[0].shape[1]
    tm = ROW_TILE
    pb, sb = n_p // tm, n_s // tm
    n_total = n_p + n_s
    p_row = lambda w: pl.BlockSpec((tm, w), lambda i: (jnp.minimum(i, pb - 1), 0))
    s_row = lambda w: pl.BlockSpec((tm, w), lambda i: (jnp.maximum(i - pb, 0), 0))
    out_row = lambda w: pl.BlockSpec((tm, w), lambda i: (i, 0))
    const = lambda a: pl.BlockSpec(a.shape, lambda i: (0, 0), pipeline_mode=pl.Buffered(1))
    return pl.pallas_call(
        functools.partial(_merge_kernel, prompt_blocks=pb),
        grid=(pb + sb,),
        in_specs=[p_row(width), p_row(width), p_row(d), s_row(width), s_row(width), s_row(d),
                  const(wo), const(g), const(wr), const(br)],
        out_specs=[out_row(d), out_row(d), out_row(LANES)],
        out_shape=[jax.ShapeDtypeStruct((n_total, d), F32), jax.ShapeDtypeStruct((n_total, d), F32),
                   jax.ShapeDtypeStruct((n_total, LANES), F32)],
        compiler_params=_params("arbitrary"),
        name="merge",
    )(*prompt, *sample, wo, g, wr, br)


def _route_kernel(lg_ref, idx_ref, gate_ref, cnt_ref, carry_ref, *, n_groups, per_group):
    tr = lg_ref.shape[0]

    @pl.when(pl.program_id(0) == 0)
    def _():
        carry_ref[...] = jnp.zeros_like(carry_ref)

    x = lg_ref[...]
    lane = lax.broadcasted_iota(jnp.int32, x.shape, 1).astype(F32)
    far = float(LANES)
    rmax = lambda a: jnp.max(a, axis=1, keepdims=True)
    first = lambda hit: jnp.min(jnp.where(hit, lane, far), axis=1, keepdims=True)

    gmask = lane < n_groups
    gl = jnp.where(gmask, x, -jnp.inf)
    gmax = rmax(gl)
    gsel = first(gl == gmax)
    p_g = 1.0 / jnp.sum(jnp.where(gmask, jnp.exp(x - gmax), 0.0), axis=1, keepdims=True)
    lo = n_groups + gsel * per_group
    emask = (lane >= lo) & (lane < lo + per_group)
    el = jnp.where(emask, x, -jnp.inf)
    m1 = rmax(el)
    i1 = first(el == m1)
    el2 = jnp.where(lane == i1, -jnp.inf, el)
    m2 = rmax(el2)
    i2 = first(el2 == m2)
    zsum = jnp.sum(jnp.where(emask, jnp.exp(x - m1), 0.0), axis=1, keepdims=True)
    p1 = 1.0 / zsum
    p2 = jnp.exp(m2 - m1) / zsum
    psum = p1 + p2
    gate1 = p_g * p1 / psum
    gate2 = p_g * p2 / psum
    e1 = i1 - n_groups
    e2 = i2 - n_groups
    hit1 = lane == e1
    hit2 = lane == e2
    onehot = jnp.where(hit1 | hit2, 1.0, 0.0)
    r = lax.broadcasted_iota(jnp.int32, (tr, tr), 0)
    c = lax.broadcasted_iota(jnp.int32, (tr, tr), 1)
    before = jnp.where(c < r, 1.0, 0.0).astype(BF16)
    seen = jnp.dot(before, onehot.astype(BF16), preferred_element_type=F32) + carry_ref[...]
    r1 = jnp.sum(jnp.where(hit1, seen, 0.0), axis=1, keepdims=True)
    r2 = jnp.sum(jnp.where(hit2, seen, 0.0), axis=1, keepdims=True)
    carry_ref[...] = carry_ref[...] + jnp.sum(onehot, axis=0, keepdims=True)
    cnt_ref[...] = carry_ref[...]

    col = lax.broadcasted_iota(jnp.int32, idx_ref.shape, 1)
    idx = jnp.where(col == 0, e1, jnp.where(col == 1, e2, jnp.where(col == 2, r1, r2)))
    idx_ref[...] = idx.astype(jnp.int32)
    gate_ref[...] = jnp.where(col == 0, gate1, jnp.where(col == 1, gate2, 0.0))


def _route(logits, n_groups, per_group):
    n = logits.shape[0]
    tr = ROW_TILE
    narrow = pl.BlockSpec((tr, 8), lambda i: (i, 0))
    return pl.pallas_call(
        functools.partial(_route_kernel, n_groups=n_groups, per_group=per_group),
        grid=(n // tr,),
        in_specs=[pl.BlockSpec((tr, LANES), lambda i: (i, 0))],
        out_specs=[narrow, narrow, pl.BlockSpec((1, LANES), lambda i: (0, 0))],
        out_shape=[jax.ShapeDtypeStruct((n, 8), jnp.int32), jax.ShapeDtypeStruct((n, 8), F32),
                   jax.ShapeDtypeStruct((1, LANES), F32)],
        scratch_shapes=[pltpu.VMEM((1, LANES), F32)],
        compiler_params=_params("arbitrary"),
        name="route",
    )(logits)


def _row_copy(src_ref, src_row, dst_ref, dst_row, sem):
    return pltpu.make_async_copy(src_ref.at[pl.ds(src_row, 1)], dst_ref.at[pl.ds(dst_row, 1)], sem)


def _wait_rows(src_ref, dst_ref, sem, n):
    pltpu.make_async_copy(src_ref.at[pl.ds(0, n)], dst_ref.at[pl.ds(0, n)], sem).wait()


def _dispatch_kernel(dest_ref, hn_ref, xs_ref, sem):
    td = hn_ref.shape[0]

    def start(r, _):
        for k in range(TOP_K):
            _row_copy(hn_ref, r, xs_ref, dest_ref[0, 0, TOP_K * r + k], sem).start(priority=k % 2)
        return 0

    for r in range(td):
        start(r, 0)
    for _ in range(TOP_K):
        _wait_rows(hn_ref, xs_ref, sem, td)


def _dispatch(dest, hn):
    n, d = hn.shape
    td = DISPATCH_TILE
    return pl.pallas_call(
        _dispatch_kernel,
        grid=(n // td,),
        in_specs=[pl.BlockSpec((1, 1, TOP_K * td), lambda i: (i, 0, 0), memory_space=pltpu.SMEM),
                  pl.BlockSpec((td, d), lambda i: (i, 0))],
        out_specs=pl.BlockSpec(memory_space=pl.ANY),
        out_shape=jax.ShapeDtypeStruct((n * TOP_K, d), F32),
        scratch_shapes=[pltpu.SemaphoreType.DMA(())],
        compiler_params=_params("arbitrary"),
        name="dispatch",
    )(dest.reshape(n // td, 1, TOP_K * td), hn)


def _expert_kernel(blk_ref, exp_ref, lo_ref, hi_ref, first_ref, xs_ref, w1_ref, w3_ref, w2_ref, o_ref):
    w = pl.program_id(0)
    lo, hi = lo_ref[w], hi_ref[w]

    def swiglu():
        rows = lax.broadcasted_iota(jnp.int32, xs_ref.shape, 0)
        x = jnp.where((rows >= lo) & (rows < hi), xs_ref[...], 0.0).astype(BF16)
        a = jnp.dot(x, w1_ref[...], preferred_element_type=F32)
        b = jnp.dot(x, w3_ref[...], preferred_element_type=F32)
        act = (a * jax.nn.sigmoid(a) * b).astype(BF16)
        return jnp.dot(act, w2_ref[...], preferred_element_type=F32)

    @pl.when((hi > lo) & (first_ref[w] == 1))
    def _():
        o_ref[...] = swiglu()

    @pl.when((hi > lo) & (first_ref[w] == 0))
    def _():
        o_ref[...] += swiglu()


def _experts(plan, xs, w1, w3, w2):
    m, d = xs.shape
    de = w1.shape[2]
    blk = EXPERT_BLOCK
    n_items = plan[0].shape[0]
    row_blk = lambda i, rb, ex, lo, hi, fs: (rb[i], 0)
    weight = lambda i, rb, ex, lo, hi, fs: (ex[i], 0, 0)
    return pl.pallas_call(
        _expert_kernel,
        grid_spec=pltpu.PrefetchScalarGridSpec(
            num_scalar_prefetch=5,
            grid=(n_items,),
            in_specs=[pl.BlockSpec((blk, d), row_blk),
                      pl.BlockSpec((None, d, de), weight),
                      pl.BlockSpec((None, d, de), weight),
                      pl.BlockSpec((None, de, d), weight)],
            out_specs=pl.BlockSpec((blk, d), row_blk)),
        out_shape=jax.ShapeDtypeStruct((m, d), F32),
        compiler_params=_params("arbitrary"),
        name="experts",
    )(*plan, xs, w1, w3, w2)


def _combine_kernel(dest_ref, next_ref, gate_ref, h_ref, rows_ref, g_ref, y_ref, buf, sem, *, final_norm):
    i = pl.program_id(0)
    last = pl.num_programs(0) - 1
    tc = h_ref.shape[0]
    slot = i % 2

    def gather(idx_ref, into, unrolled):
        def start(r, _):
            for k in range(TOP_K):
                _row_copy(rows_ref, idx_ref[0, 0, TOP_K * r + k], buf.at[into, k], r,
                          sem.at[into]).start(priority=k % 2)
            return 0

        if unrolled:
            for r in range(tc):
                start(r, 0)
        else:
            lax.fori_loop(0, tc, start, 0, unroll=ISSUE_UNROLL)

    def drain(into):
        for k in range(TOP_K):
            _wait_rows(rows_ref, buf.at[into, k], sem.at[into], tc)

    pl.when(i == 0)(lambda: gather(dest_ref, 0, False))
    gather(next_ref, 1 - slot, True)
    drain(slot)
    gates = gate_ref[...]
    y = gates[:, 0:1] * buf[slot, 0] + gates[:, 1:2] * buf[slot, 1]
    h = h_ref[...] + y
    y_ref[...] = _rms(h, g_ref[...]) if final_norm else h
    pl.when(i == last)(lambda: drain(1 - slot))


def _combine(dest, gates, h, rows, g, n, row_off, final_norm):
    d = h.shape[1]
    tc = COMBINE_TILE
    off = row_off // tc
    last = off + n // tc - 1
    idx_spec = lambda f: pl.BlockSpec((1, 1, TOP_K * tc), lambda i: (f(i), 0, 0), memory_space=pltpu.SMEM)
    dest3 = dest.reshape(-1, 1, TOP_K * tc)
    return pl.pallas_call(
        functools.partial(_combine_kernel, final_norm=final_norm),
        grid=(n // tc,),
        in_specs=[idx_spec(lambda i: i + off), idx_spec(lambda i: jnp.minimum(i + off + 1, last)),
                  pl.BlockSpec((tc, 8), lambda i: (i + off, 0)),
                  pl.BlockSpec((tc, d), lambda i: (i + off, 0)),
                  pl.BlockSpec(memory_space=pl.ANY),
                  pl.BlockSpec((1, d), lambda i: (0, 0))],
        out_specs=pl.BlockSpec((tc, d), lambda i: (i, 0)),
        out_shape=jax.ShapeDtypeStruct((n, d), F32),
        scratch_shapes=[pltpu.VMEM((2, TOP_K, tc, d), F32), pltpu.SemaphoreType.DMA((2,))],
        compiler_params=_params("arbitrary"),
        name="combine",
    )(dest3, dest3, gates, h, rows, g)


def _round_up(a, m):
    return -(-a // m) * m


def _moe_plan(idx, counts, n_experts):
    blk = EXPERT_BLOCK
    n = idx.shape[0]
    i32 = lambda a: a.astype(jnp.int32)
    pick = lambda table, i: jnp.sum(jnp.where(i[..., None] == jnp.arange(table.shape[0]), table, 0), axis=-1)
    rank_of = lambda ends, x: jnp.sum(i32(ends[None, :] <= x[:, None]), axis=1)
    counts = i32(counts[0, :n_experts])
    end = jnp.cumsum(counts)
    start = end - counts
    dest = i32(pick(start, idx[:, :TOP_K]) + idx[:, TOP_K:2 * TOP_K])
    n_blocks = n * TOP_K // blk
    block_lo = jnp.arange(n_blocks, dtype=jnp.int32) * blk
    e_first = jnp.minimum(rank_of(end, block_lo), n_experts - 1)
    e_last = jnp.minimum(rank_of(end, block_lo + blk - 1), n_experts - 1)
    item_end = jnp.cumsum(e_last - e_first + 1)
    item_start = item_end - (e_last - e_first + 1)
    n_items = n_blocks + n_experts - 1
    w = jnp.arange(n_items, dtype=jnp.int32)
    live = w < item_end[-1]
    rb = jnp.minimum(rank_of(item_end, w), n_blocks - 1)
    ex = jnp.where(live, pick(e_first, rb) + (w - pick(item_start, rb)), e_last[-1])
    lo = jnp.where(live, jnp.clip(pick(start, ex) - rb * blk, 0, blk), 0)
    hi = jnp.where(live, jnp.clip(pick(end, ex) - rb * blk, 0, blk), 0)
    first = i32(w == pick(item_start, rb))
    return dest, (i32(rb), i32(ex), i32(lo), i32(hi), first)


def _split_w_in(w_in, width, n_heads):
    wm = jnp.concatenate([w_in[:, :3 * width], w_in[:, 3 * width + n_heads:]], axis=1).astype(BF16)
    wf = jnp.pad(w_in[:, 3 * width:3 * width + n_heads], ((0, 0), (0, LANES - n_heads))).astype(BF16)
    return wm, wf


def kernel(x_prompt, x_sample, cache_fox_k, cache_fox_v, cache_fox_logf, cache_sb_k, cache_sb_v,
           w_in, b_f, g_attn, g_out_fox, g_out_sb, w_out, g_ffn, w_group, b_group, w_router, b_router,
           w1, w3, w2, g_final):
    depth = w_in.shape[0]
    bp, sp, d = x_prompt.shape
    bs, ss, _ = x_sample.shape
    past = cache_fox_logf.shape[2]
    n_heads = cache_fox_logf.shape[3]
    width = n_heads * HEAD_DIM
    n_groups = w_group.shape[2]
    n_experts = w_router.shape[2]
    per_group = n_experts // n_groups
    n_p, n_s = bp * sp, bs * ss
    sk_s = _round_up(past + ss, LANES)
    pad_s = sk_s - past - ss
    row2 = lambda a: a.reshape(1, -1)

    hp = x_prompt.reshape(n_p, d)
    hs = x_sample.reshape(n_s, d)
    outs_p, outs_s = [], []
    for l in range(depth):
        wm, wf = _split_w_in(w_in[l], width, n_heads)
        bfp = jnp.pad(b_f[l], (0, LANES - n_heads)).reshape(1, LANES)
        wo = w_out[l].astype(BF16)
        wr = jnp.pad(jnp.concatenate([w_group[l], w_router[l]], axis=1),
                     ((0, 0), (0, LANES - n_groups - n_experts))).astype(BF16)
        br = jnp.pad(jnp.concatenate([b_group[l], b_router[l]]), (0, LANES - n_groups - n_experts)).reshape(1, LANES)
        g_a, g_f = row2(g_attn[l]), row2(g_ffn[l])
        g_of, g_os = row2(g_out_fox[l]), row2(g_out_sb[l])

        lf, cf, qf, kfb, kf, vfb, vf, qs, ksb, ks, vsb, vs = _in_proj(hp, g_a, wm, wf, bfp, n_heads, sp // ROW_TILE)
        b3 = lambda a: a.reshape(bp, sp, -1)
        tq = min(sp, ATTN_TILE)
        nf = _attention("fox", b3(qf), b3(kfb), b3(vfb), g_of, jnp.swapaxes(b3(cf), 1, 2), tq=tq, tk=tq, q_off=0)
        ns = _attention("sb", b3(qs), b3(ksb), b3(vsb), g_os, None, tq=tq, tk=tq, q_off=0)
        outs_p.append((kf, vf, lf, ks, vs))

        lf2, qf2, kfb2, kf2, vfb2, vf2, qs2, ksb2, ks2, vsb2, vs2 = _in_proj(hs, g_a, wm, wf, bfp, n_heads)
        c3 = lambda a: a.reshape(bs, ss, -1)
        keys = lambda cache, new: _with_cache(cache, l, c3(new), sk_s)
        cf_all = _cumsum(jnp.concatenate([cache_fox_logf[l].astype(F32), c3(lf2)], axis=1), past + ss)
        cf_all = jnp.swapaxes(jnp.pad(cf_all, ((0, 0), (0, pad_s), (0, 0))), 1, 2)
        nf2 = _attention("fox", c3(qf2), keys(cache_fox_k, kfb2), keys(cache_fox_v, vfb2), g_of, cf_all,
                         tq=ss, tk=sk_s, q_off=past)
        ns2 = _attention("sb", c3(qs2), keys(cache_sb_k, ksb2), keys(cache_sb_v, vsb2), g_os, None,
                         tq=ss, tk=sk_s, q_off=past)
        outs_s.append((kf2, vf2, lf2, ks2, vs2))

        h_all, hn_all, logits = _merge((nf.reshape(n_p, width), ns.reshape(n_p, width), hp),
                                       (nf2.reshape(n_s, width), ns2.reshape(n_s, width), hs), wo, g_f, wr, br)
        idx, gates, counts = _route(logits, n_groups, per_group)
        dest, plan = _moe_plan(idx, counts, n_experts)
        xs = _dispatch(dest, hn_all)
        rows = _experts(plan, xs, w1[l].astype(BF16), w3[l].astype(BF16), w2[l].astype(BF16))
        last = l == depth - 1
        g_last = row2(g_final)
        hp = _combine(dest, gates, h_all, rows, g_last, n_p, 0, last)
        hs = _combine(dest, gates, h_all, rows, g_last, n_s, n_p, last)

    heads = lambda a, b, s: a.reshape(b, s, n_heads, HEAD_DIM)
    stack = lambda outs, i, f: jnp.stack([f(o[i]) for o in outs])
    hp_f = lambda a: heads(a, bp, sp)
    hs_f = lambda a: heads(a, bs, ss)
    return (hp.reshape(bp, sp, d), hs.reshape(bs, ss, d),
            stack(outs_p, 0, hp_f), stack(outs_p, 1, hp_f), stack(outs_p, 2, lambda a: a.reshape(bp, sp, n_heads)),
            stack(outs_p, 3, hp_f), stack(outs_p, 4, hp_f),
            stack(outs_s, 0, hs_f), stack(outs_s, 1, hs_f), stack(outs_s, 2, lambda a: a.reshape(bs, ss, n_heads)),
            stack(outs_s, 3, hs_f), stack(outs_s, 4, hs_f))
```
